```python
import math
import functools
import jax
import jax.numpy as jnp
from jax import lax
import numpy as np

D_MODEL = 4096
BATCH = 4
SEQ = 2048
DEPTH = 2
DEC_BATCH = 8
DEC_SEQ = 4
PAST_LEN = 16384
PAGE_SIZE = 128

N_A_LAYERS = DEPTH // 2
N_B_LAYERS = DEPTH - N_A_LAYERS

RET_HEADS = 16
RET_DK = D_MODEL // RET_HEADS
RET_DV = 2 * RET_DK
RET_CHUNK = 128
RET_THETA = 10000.0

NSA_HEADS = 32
NSA_HD = D_MODEL // NSA_HEADS
NSA_KV_HEADS = 4
NSA_GROUP = NSA_HEADS // NSA_KV_HEADS
CMP_LEN = 32
CMP_STRIDE = 16
CMP_HIDDEN = NSA_HD
SLC_BLOCK = 64
N_SELECT = 16
WINDOW = 512
WIN_QBLK = 128
SLC_QBLK = 16
ROPE_DIMS = NSA_HD // 4
ROPE_THETA = 500000.0

D_FF = 11008
CONV_W = 3

EPS = 1e-6
NEG = -1e30
FORCE = 1e9
TINY = 1e-20

kernel_name = 'yoco_retnet_nsa_convffn_step'


def rmsnorm(x, g):
    xf = x.astype(jnp.float32)
    y = xf * lax.rsqrt(jnp.mean(xf * xf, axis=-1, keepdims=True) + EPS)
    return (y * g.astype(jnp.float32)).astype(x.dtype)


def apply_rope(x, pos, dim, theta):
    half = dim // 2
    inv = jnp.power(theta, -jnp.arange(half, dtype=jnp.float32) / half)
    ang = pos.astype(jnp.float32)[:, None] * inv[None, :]
    cos = jnp.cos(ang)[None, :, None, :]
    sin = jnp.sin(ang)[None, :, None, :]
    x1 = x[..., :half].astype(jnp.float32)
    x2 = x[..., half:dim].astype(jnp.float32)
    rot = jnp.concatenate([x1 * cos - x2 * sin, x2 * cos + x1 * sin], axis=-1).astype(x.dtype)
    return jnp.concatenate([rot, x[..., dim:]], axis=-1)


def retention_chunkwise(q, k, v, s0, chunk):
    B, T, H, DK = q.shape
    DV = v.shape[-1]
    n = T // chunk
    log_g = jnp.log1p(-jnp.exp2(-5.0 - jnp.arange(H, dtype=jnp.float32)))
    i = jnp.arange(chunk, dtype=jnp.float32)
    read_decay = jnp.exp(log_g[:, None] * (i + 1.0))[None, :, :, None]
    write_decay = jnp.exp(log_g[:, None] * (chunk - 1.0 - i))[None, :, :, None]
    chunk_decay = jnp.exp(log_g * chunk)[None, :, None, None]
    diff = i[:, None] - i[None, :]
    intra = jnp.where(diff >= 0, jnp.exp(log_g[:, None, None] * jnp.maximum(diff, 0.0)), 0.0)[None]

    def blocks(a):
        return a.astype(jnp.float32).reshape(B, n, chunk, H, a.shape[-1]).transpose(1, 0, 3, 2, 4)

    def step(s, qkv):
        qc, kc, vc = qkv
        att = jnp.einsum('bhid,bhjd->bhij', qc, kc) * intra
        o = jnp.einsum('bhij,bhjv->bhiv', att, vc) + jnp.einsum('bhid,bhdv->bhiv', qc * read_decay, s)
        s = s * chunk_decay + jnp.einsum('bhjd,bhjv->bhdv', kc * write_decay, vc)
        return s, o

    s, o = lax.scan(step, s0.astype(jnp.float32), (blocks(q), blocks(k), blocks(v)))
    return o.transpose(1, 0, 3, 2, 4).reshape(B, T, H, DV), s


def retention_mixer(h, pos, s0, w_in, w_out):
    B, T, _ = h.shape
    qd = RET_HEADS * RET_DK
    vd = RET_HEADS * RET_DV
    q, k, v, g = jnp.split(h @ w_in, [qd, 2 * qd, 2 * qd + vd], axis=-1)
    q = apply_rope(q.reshape(B, T, RET_HEADS, RET_DK), pos, RET_DK, RET_THETA)
    k = apply_rope(k.reshape(B, T, RET_HEADS, RET_DK), pos, RET_DK, RET_THETA) * (RET_DK ** -0.5)
    v = v.reshape(B, T, RET_HEADS, RET_DV)
    o, s = retention_chunkwise(q, k, v, s0, math.gcd(T, RET_CHUNK))
    o = o * lax.rsqrt(jnp.mean(o * o, axis=-1, keepdims=True) + EPS)
    o = o.reshape(B, T, vd) * jax.nn.silu(g.astype(jnp.float32))
    return o.astype(h.dtype) @ w_out, s


def conv_ffn(h, conv_state, w_up, conv_w, conv_b, w_down):
    T = h.shape[1]
    val, gate = jnp.split(h @ w_up, 2, axis=-1)
    ext = jnp.concatenate([conv_state.astype(gate.dtype), gate], axis=1)
    conv = conv_b + sum(ext[:, j:j + T] * conv_w[j] for j in range(CONV_W))
    y = (val * jax.nn.silu(conv)) @ w_down
    return y, ext[:, T:]


def masked_gqa(q, k, v, mask):
    B, T, H, HD = q.shape
    kvh = k.shape[2]
    qg = q.reshape(B, T, kvh, H // kvh, HD)
    s = jnp.einsum('btgqd,bsgd->btgqs', qg, k).astype(jnp.float32) * (HD ** -0.5)
    m = mask[None, :, None, None, :]
    s = jnp.where(m, s, NEG)
    e = jnp.where(m, jnp.exp(s - jnp.max(s, axis=-1, keepdims=True)), 0.0)
    p = e / jnp.maximum(jnp.sum(e, axis=-1, keepdims=True), TINY)
    o = jnp.einsum('btgqs,bsgd->btgqd', p.astype(v.dtype), v)
    return o.reshape(B, T, H, HD), p


def band_mask(qpos, kpos):
    d = qpos[:, None] - kpos[None, :]
    return (d >= 0) & (d < WINDOW) & (kpos[None, :] >= 0)


def compress(rows, pe, w1, w2):
    B, L = rows.shape[:2]
    r = CMP_LEN // CMP_STRIDE
    n_seg = L // CMP_STRIDE
    n_c = n_seg - r + 1
    seg = rows[:, :n_seg * CMP_STRIDE].reshape(B, n_seg, CMP_STRIDE, NSA_KV_HEADS, NSA_HD)
    pe_s = pe.reshape(r, CMP_STRIDE, NSA_HD)
    w1_s = w1.reshape(r, CMP_STRIDE, NSA_HD, CMP_HIDDEN)
    hid = sum(jnp.einsum('bnsgd,sde->bnge', seg[:, m:m + n_c] + pe_s[m][:, None, :], w1_s[m]) for m in range(r))
    return jax.nn.gelu(hid) @ w2


def cmp_mask(qpos, n_c):
    end = jnp.arange(n_c) * CMP_STRIDE + CMP_LEN - 1
    return end[None, :] <= qpos[:, None]


def slc_aggregation(n_c, n_s):
    rs = SLC_BLOCK // CMP_STRIDE
    rc = CMP_LEN // CMP_STRIDE
    i = jnp.arange(n_c)[:, None]
    j = jnp.arange(n_s)[None, :]
    return sum((i == j * rs + m - n).astype(jnp.float32) for m in range(rs) for n in range(rc))


def select_blocks(p_cmp, qpos, n_slc):
    imp = jnp.einsum('btgqn,ns->btgs', p_cmp, slc_aggregation(p_cmp.shape[-1], n_slc))
    blk = jnp.arange(n_slc)[None, :]
    cur = (qpos // SLC_BLOCK)[:, None]
    visible = blk * SLC_BLOCK <= qpos[:, None]
    forced = (blk == 0) | (blk == cur) | (blk == cur - 1)
    score = jnp.where(forced[None, :, None, :], FORCE, imp)
    score = jnp.where(visible[None, :, None, :], score, NEG)
    vals, idx = lax.top_k(score, min(N_SELECT, n_slc))
    return idx, vals > NEG / 2


def slc_attention(q, qpos, idx, ok, fetch):
    B, T, H, HD = q.shape
    qblk = math.gcd(T, SLC_QBLK)
    nb = T // qblk

    def split(a):
        return a.reshape((B, nb, qblk) + a.shape[2:]).swapaxes(0, 1)

    def one(args):
        qi, pi, ii, oki = args
        pos = ii[..., None] * SLC_BLOCK + jnp.arange(SLC_BLOCK)
        kk, vv = fetch(pos)
        qg = qi.reshape(B, qblk, NSA_KV_HEADS, NSA_GROUP, HD)
        s = jnp.einsum('btgqd,btgksd->btgqks', qg, kk).astype(jnp.float32) * (HD ** -0.5)
        mask = (pos <= pi[None, :, None, None, None]) & oki[..., None]
        s = jnp.where(mask[:, :, :, None], s, NEG)
        shp = s.shape
        p = jax.nn.softmax(s.reshape(shp[:4] + (-1,)), axis=-1).reshape(shp)
        o = jnp.einsum('btgqks,btgksd->btgqd', p.astype(vv.dtype), vv)
        return o.reshape(B, qblk, H, HD)

    o = lax.map(one, (split(q), qpos.reshape(nb, qblk), split(idx), split(ok)))
    return o.swapaxes(0, 1).reshape(B, T, H, HD)


def window_prompt(q, qpos, k, v):
    B, T, H, HD = q.shape
    qblk = math.gcd(T, WIN_QBLK)
    nb = T // qblk
    span = WINDOW + qblk
    kp = jnp.pad(k, ((0, 0), (WINDOW, 0), (0, 0), (0, 0)))
    vp = jnp.pad(v, ((0, 0), (WINDOW, 0), (0, 0), (0, 0)))
    qb = q.reshape(B, nb, qblk, H, HD).swapaxes(0, 1)

    def one(args):
        qi, qp = args
        start = qp[0]
        ki = lax.dynamic_slice_in_dim(kp, start, span, axis=1)
        vi = lax.dynamic_slice_in_dim(vp, start, span, axis=1)
        kpos = start - WINDOW + jnp.arange(span)
        o, _ = masked_gqa(qi, ki, vi, band_mask(qp, kpos))
        return o

    o = lax.map(one, (qb, qpos.reshape(nb, qblk)))
    return o.swapaxes(0, 1).reshape(B, T, H, HD)


def nsa_branches(q, qpos, kcmp, vcmp, fetch, window, n_slc):
    o_cmp, p_cmp = masked_gqa(q, kcmp, vcmp, cmp_mask(qpos, kcmp.shape[1]))
    idx, ok = select_blocks(p_cmp, qpos, n_slc)
    o_slc = slc_attention(q, qpos, idx, ok, fetch)
    o_win = window(q, qpos)
    return o_cmp, o_slc, o_win


def nsa_mixer(h, pos, branches, w_in, w_out):
    B, T, _ = h.shape
    qw = NSA_HEADS * NSA_HD
    proj = h @ w_in
    q = apply_rope(proj[..., :qw].reshape(B, T, NSA_HEADS, NSA_HD), pos, ROPE_DIMS, ROPE_THETA)
    gate = jax.nn.sigmoid(proj[..., qw:].astype(jnp.float32)).reshape(B, T, 3, NSA_HEADS, 1)
    o_cmp, o_slc, o_win = branches(q, pos)
    o = gate[:, :, 0] * o_cmp + gate[:, :, 1] * o_slc + gate[:, :, 2] * o_win
    return o.reshape(B, T, qw).astype(h.dtype) @ w_out


def shared_kv(x, pos, kv_norm, kv_w):
    B, T, _ = x.shape
    kv = (rmsnorm(x, kv_norm) @ kv_w).reshape(B, T, 6, NSA_KV_HEADS, NSA_HD)
    kc = apply_rope(kv[:, :, 0], pos, ROPE_DIMS, ROPE_THETA)
    ks = apply_rope(kv[:, :, 2], pos, ROPE_DIMS, ROPE_THETA)
    kw = apply_rope(kv[:, :, 4], pos, ROPE_DIMS, ROPE_THETA)
    return kc, kv[:, :, 1], ks, kv[:, :, 3], kw, kv[:, :, 5]


def make_prompt_context(kv, cmp_pe, cmp_w1, cmp_w2):
    kc, vc, ks, vs, kw, vw = kv
    B, T = kc.shape[:2]
    kcmp = compress(kc, cmp_pe[0], cmp_w1[0], cmp_w2[0])
    vcmp = compress(vc, cmp_pe[1], cmp_w1[1], cmp_w2[1])
    bi = jnp.arange(B)[:, None, None, None, None]
    gi = jnp.arange(NSA_KV_HEADS)[None, None, :, None, None]

    def fetch(pos):
        p = jnp.clip(pos, 0, T - 1)
        return ks[bi, p, gi], vs[bi, p, gi]

    def window(q, qpos):
        return window_prompt(q, qpos, kw, vw)

    branches = functools.partial(nsa_branches, kcmp=kcmp, vcmp=vcmp, fetch=fetch, window=window, n_slc=-(-T // SLC_BLOCK))
    kv_rows = jnp.stack([kc, vc, ks, vs], axis=2)
    win = jnp.stack([kw, vw], axis=2)[:, T - min(WINDOW, T):]
    return branches, kv_rows, win


def make_sample_context(kv, cache_kv, cache_win, page_table, cmp_pe, cmp_w1, cmp_w2):
    kc, vc, ks, vs, kw, vw = kv
    DB, T = kc.shape[:2]
    past = cache_kv[page_table, :, :2].reshape(DB, PAST_LEN, 2, NSA_KV_HEADS, NSA_HD)
    kcmp = compress(jnp.concatenate([past[:, :, 0], kc], axis=1), cmp_pe[0], cmp_w1[0], cmp_w2[0])
    vcmp = compress(jnp.concatenate([past[:, :, 1], vc], axis=1), cmp_pe[1], cmp_w1[1], cmp_w2[1])
    bi = jnp.arange(DB)[:, None, None, None, None]
    gi = jnp.arange(NSA_KV_HEADS)[None, None, :, None, None]

    def fetch(pos):
        in_past = (pos < PAST_LEN)[..., None]
        pp = jnp.clip(pos, 0, PAST_LEN - 1)
        phys = page_table[bi, pp // PAGE_SIZE]
        off = pp % PAGE_SIZE
        pn = jnp.clip(pos - PAST_LEN, 0, T - 1)
        k_sel = jnp.where(in_past, cache_kv[phys, off, 2, gi], ks[bi, pn, gi])
        v_sel = jnp.where(in_past, cache_kv[phys, off, 3, gi], vs[bi, pn, gi])
        return k_sel, v_sel

    w_rows = cache_win.shape[1]
    wk = jnp.concatenate([cache_win[:, :, 0].astype(kw.dtype), kw], axis=1)
    wv = jnp.concatenate([cache_win[:, :, 1].astype(vw.dtype), vw], axis=1)
    wpos = PAST_LEN - w_rows + jnp.arange(w_rows + T)

    def window(q, qpos):
        o, _ = masked_gqa(q, wk, wv, band_mask(qpos, wpos))
        return o

    branches = functools.partial(nsa_branches, kcmp=kcmp, vcmp=vcmp, fetch=fetch, window=window, n_slc=-(-(PAST_LEN + T) // SLC_BLOCK))
    kv_rows = jnp.stack([kc, vc, ks, vs], axis=2)
    win = jnp.stack([wk, wv], axis=2)[:, T:]
    return branches, kv_rows, win


def run_trunk(x, pos0, ret_s0, conv_s0, build_context, norm_mix, norm_ffn, ret_w_in, ret_w_out, kv_norm, kv_w,
              nsa_w_in, nsa_w_out, ffn_w_up, ffn_conv_w, ffn_conv_b, ffn_w_down, norm_final):
    T = x.shape[1]
    pos = pos0 + jnp.arange(T)
    ret_out, conv_out = [], []
    branches, kv_rows, win = None, None, None
    for layer in range(DEPTH):
        h = rmsnorm(x, norm_mix[layer])
        if layer < N_A_LAYERS:
            y, s = retention_mixer(h, pos, ret_s0[layer], ret_w_in[layer], ret_w_out[layer])
            ret_out.append(s)
        else:
            b = layer - N_A_LAYERS
            y = nsa_mixer(h, pos, branches, nsa_w_in[b], nsa_w_out[b])
        x = x + y
        h = rmsnorm(x, norm_ffn[layer])
        y, cs = conv_ffn(h, conv_s0[layer], ffn_w_up[layer], ffn_conv_w[layer], ffn_conv_b[layer], ffn_w_down[layer])
        conv_out.append(cs)
        x = x + y
        if layer == N_A_LAYERS - 1:
            branches, kv_rows, win = build_context(shared_kv(x, pos, kv_norm, kv_w))
    return rmsnorm(x, norm_final), jnp.stack(ret_out), jnp.stack(conv_out), kv_rows, win


def setup_inputs(seed: int = 0) -> dict:
    key = jax.random.key(seed)
    k = jax.random.split(key, 24)
    f32 = jnp.float32

    def nrm(kk, shape, scale):
        return jax.random.normal(kk, shape, f32) * scale

    n_pages = PAST_LEN // PAGE_SIZE
    n_used = DEC_BATCH * n_pages
    n_pool = n_used + max(1, n_used // 4)
    w_rows = min(WINDOW, PAST_LEN)
    qw = NSA_HEADS * NSA_HD
    ret_in = 2 * RET_HEADS * RET_DK + 2 * RET_HEADS * RET_DV
    return {
        'x_prompt': nrm(k[0], (BATCH, SEQ, D_MODEL), 1.0),
        'x_sample': nrm(k[1], (DEC_BATCH, DEC_SEQ, D_MODEL), 1.0),
        'cache_kv': nrm(k[2], (n_pool, PAGE_SIZE, 4, NSA_KV_HEADS, NSA_HD), 1.0),
        'cache_win': nrm(k[3], (DEC_BATCH, w_rows, 2, NSA_KV_HEADS, NSA_HD), 1.0),
        'state_ret': nrm(k[4], (N_A_LAYERS, DEC_BATCH, RET_HEADS, RET_DK, RET_DV), 1.0),
        'state_conv': nrm(k[5], (DEPTH, DEC_BATCH, CONV_W - 1, D_FF), 1.0),
        'page_table': jax.random.permutation(k[6], n_pool)[:n_used].reshape(DEC_BATCH, n_pages).astype(jnp.int32),
        'norm_mix': 1.0 + nrm(k[7], (DEPTH, D_MODEL), 0.02),
        'norm_ffn': 1.0 + nrm(k[8], (DEPTH, D_MODEL), 0.02),
        'ret_w_in': nrm(k[9], (N_A_LAYERS, D_MODEL, ret_in), D_MODEL ** -0.5),
        'ret_w_out': nrm(k[10], (N_A_LAYERS, RET_HEADS * RET_DV, D_MODEL), (RET_HEADS * RET_DV) ** -0.5),
        'kv_norm': 1.0 + nrm(k[11], (D_MODEL,), 0.02),
        'kv_w': nrm(k[12], (D_MODEL, 6 * NSA_KV_HEADS * NSA_HD), D_MODEL ** -0.5),
        'cmp_pe': nrm(k[13], (2, CMP_LEN, NSA_HD), 0.1),
        'cmp_w1': nrm(k[14], (2, CMP_LEN, NSA_HD, CMP_HIDDEN), (CMP_LEN * NSA_HD) ** -0.5),
        'cmp_w2': nrm(k[15], (2, CMP_HIDDEN, NSA_HD), CMP_HIDDEN ** -0.5),
        'nsa_w_in': nrm(k[16], (N_B_LAYERS, D_MODEL, qw + 3 * NSA_HEADS), D_MODEL ** -0.5),
        'nsa_w_out': nrm(k[17], (N_B_LAYERS, qw, D_MODEL), qw ** -0.5),
        'ffn_w_up': nrm(k[18], (DEPTH, D_MODEL, 2 * D_FF), D_MODEL ** -0.5),
        'ffn_conv_w': nrm(k[19], (DEPTH, CONV_W, D_FF), CONV_W ** -0.5),
        'ffn_conv_b': nrm(k[20], (DEPTH, D_FF), 0.02),
        'ffn_w_down': nrm(k[21], (DEPTH, D_FF, D_MODEL), D_FF ** -0.5),
        'norm_final': 1.0 + nrm(k[22], (D_MODEL,), 0.02),
    }


def reference(x_prompt, x_sample, cache_kv, cache_win, state_ret, state_conv, page_table, norm_mix, norm_ffn,
              ret_w_in, ret_w_out, kv_norm, kv_w, cmp_pe, cmp_w1, cmp_w2, nsa_w_in, nsa_w_out, ffn_w_up, ffn_conv_w,
              ffn_conv_b, ffn_w_down, norm_final):
    B = x_prompt.shape[0]
    ret0 = jnp.zeros((N_A_LAYERS, B, RET_HEADS, RET_DK, RET_DV), jnp.float32)
    conv0 = jnp.zeros((DEPTH, B, CONV_W - 1, D_FF), x_prompt.dtype)
    prompt_ctx = functools.partial(make_prompt_context, cmp_pe=cmp_pe, cmp_w1=cmp_w1, cmp_w2=cmp_w2)
    sample_ctx = functools.partial(make_sample_context, cache_kv=cache_kv, cache_win=cache_win, page_table=page_table,
                                   cmp_pe=cmp_pe, cmp_w1=cmp_w1, cmp_w2=cmp_w2)
    y_prompt, ret_p, conv_p, kv_p, win_p = run_trunk(
        x_prompt, 0, ret0, conv0, prompt_ctx, norm_mix, norm_ffn, ret_w_in, ret_w_out, kv_norm, kv_w,
        nsa_w_in, nsa_w_out, ffn_w_up, ffn_conv_w, ffn_conv_b, ffn_w_down, norm_final)
    y_sample, ret_s, conv_s, kv_s, win_s = run_trunk(
        x_sample, PAST_LEN, state_ret, state_conv, sample_ctx, norm_mix, norm_ffn, ret_w_in, ret_w_out, kv_norm, kv_w,
        nsa_w_in, nsa_w_out, ffn_w_up, ffn_conv_w, ffn_conv_b, ffn_w_down, norm_final)
    return (y_prompt, y_sample, kv_p, kv_s, win_p, win_s, ret_p, ret_s, conv_p, conv_s)
```

```python
import functools
import math

import jax
import jax.numpy as jnp
from jax import lax
from jax.experimental import pallas as pl
from jax.experimental.pallas import tpu as pltpu

F32 = jnp.float32
BF16 = jnp.bfloat16

RET_HEADS = 16
RET_CHUNK = 128
RET_THETA = 10000.0
NSA_HEADS = 32
NSA_KV_HEADS = 4
CMP_LEN = 32
CMP_STRIDE = 16
SLC_BLOCK = 64
N_SELECT = 16
WINDOW = 512
ROPE_THETA = 500000.0
PAGE_SIZE = 128
EPS = 1e-6
NEG = -1e30
FORCE = 1e9
TINY = 1e-20

LANES = 128
SUBLANES = 8
VMEM_LIMIT = 56 * 1024 * 1024


def _params(sem):
    return pltpu.CompilerParams(dimension_semantics=sem, vmem_limit_bytes=VMEM_LIMIT)


def _tile(dim, pref):
    if dim <= pref:
        return dim
    t = pref
    while dim % t:
        t //= 2
    return t


def _sigmoid(x):
    return 1.0 / (1.0 + jnp.exp(-x))


def _dot(a, b):
    return jnp.dot(a, b, preferred_element_type=F32)


def _dot_nt(a, b):
    return lax.dot_general(a, b, (((1,), (1,)), ((), ())), preferred_element_type=F32)


def _rmsnorm_kernel(x_ref, g_ref, o_ref):
    x = x_ref[...]
    y = x * lax.rsqrt(jnp.mean(x * x, axis=-1, keepdims=True) + EPS)
    o_ref[...] = (y * g_ref[...]).astype(o_ref.dtype)


def rmsnorm(x, g, out_dtype):
    m, d = x.shape
    tm = _tile(m, 256)
    return pl.pallas_call(
        _rmsnorm_kernel,
        grid=(m // tm,),
        in_specs=[pl.BlockSpec((tm, d), lambda i: (i, 0)),
                  pl.BlockSpec((1, d), lambda i: (0, 0))],
        out_specs=pl.BlockSpec((tm, d), lambda i: (i, 0)),
        out_shape=jax.ShapeDtypeStruct((m, d), out_dtype),
        compiler_params=_params(("parallel",)),
        name="rmsnorm",
    )(x, g.reshape(1, d).astype(F32))


def _rope_lanes(x, c, s_up, s_dn, half):
    return x * c + pltpu.roll(x, LANES - half, 1) * s_up + pltpu.roll(x, half, 1) * s_dn


def _mm_kernel(*refs, nk, has_res, rope_every, rope_half, act, n_out):
    x_ref, w_ref = refs[0], refs[1]
    pos = 2
    res_ref = None
    if has_res:
        res_ref = refs[pos]
        pos += 1
    if rope_every:
        c_ref, su_ref, sd_ref = refs[pos:pos + 3]
        pos += 3
    out_refs = refs[pos:pos + n_out]
    acc_ref = refs[pos + n_out] if nk > 1 else None
    j = pl.program_id(1)
    k = pl.program_id(2)

    part = _dot(x_ref[...], w_ref[...])

    def store(val):
        for o in out_refs:
            o[...] = val.astype(o.dtype)

    def finish(acc):
        if has_res:
            acc = acc + res_ref[...]
        if act == "sigmoid":
            acc = _sigmoid(acc)
        if not rope_every:
            store(acc)
            return
        tn = acc.shape[1]

        def roped():
            c, su, sd = c_ref[...], su_ref[...], sd_ref[...]
            pieces = [_rope_lanes(acc[:, a:a + LANES], c, su, sd, rope_half) for a in range(0, tn, LANES)]
            store(jnp.concatenate(pieces, axis=1))

        if rope_every == 1:
            roped()
        else:
            pl.when(j % rope_every == 0)(roped)
            pl.when(j % rope_every != 0)(lambda: store(acc))

    if nk == 1:
        finish(part)
    else:
        @pl.when(k == 0)
        def _():
            acc_ref[...] = part

        @pl.when(k > 0)
        def _():
            acc_ref[...] += part

        @pl.when(k == nk - 1)
        def _():
            finish(acc_ref[...])


def matmul(x, w, n, *, tm, tn, tk, out_dtypes, res=None, rope=None, rope_every=0, rope_half=0, act=None):
    m, kdim = x.shape
    tm, tn, tk = _tile(m, tm), _tile(n, tn), _tile(kdim, tk)
    nk = kdim // tk
    in_specs = [pl.BlockSpec((tm, tk), lambda i, j, k: (i, k)),
                pl.BlockSpec((tk, tn), lambda i, j, k: (k, j))]
    args = [x, w]
    if res is not None:
        in_specs.append(pl.BlockSpec((tm, tn), lambda i, j, k: (i, j)))
        args.append(res)
    if rope is not None:
        period = rope[0].shape[0]
        nper = period // tm
        for t in rope:
            in_specs.append(pl.BlockSpec((tm, LANES), lambda i, j, k: (i % nper, 0)))
            args.append(t)
    out_specs = [pl.BlockSpec((tm, tn), lambda i, j, k: (i, j)) for _ in out_dtypes]
    out_shape = [jax.ShapeDtypeStruct((m, n), dt) for dt in out_dtypes]
    scratch = [pltpu.VMEM((tm, tn), F32)] if nk > 1 else []
    outs = pl.pallas_call(
        functools.partial(_mm_kernel, nk=nk, has_res=res is not None,
                          rope_every=rope_every if rope is not None else 0, rope_half=rope_half,
                          act=act, n_out=len(out_dtypes)),
        grid=(m // tm, n // tn, nk),
        in_specs=in_specs, out_specs=out_specs, out_shape=out_shape,
        scratch_shapes=scratch,
        compiler_params=_params(("parallel", "parallel", "arbitrary")),
        name="matmul",
    )(*args)
    return outs if len(outs) > 1 else outs[0]


def _ffn_up_kernel(*refs, tiles_per_seq, seg, carry_mode):
    if carry_mode:
        x_ref, wv_ref, wg_ref, cw_ref, cb_ref, st_ref, a_ref, tail_ref, carry_ref = refs
    else:
        x_ref, wv_ref, wg_ref, cw_ref, cb_ref, f1_ref, f2_ref, a_ref, tail_ref = refs
    i = pl.program_id(1)
    x = x_ref[...]
    val = _dot(x, wv_ref[...])
    gate = _dot(x, wg_ref[...])
    tm = gate.shape[0]
    row = lax.broadcasted_iota(jnp.int32, (tm, 1), 0)
    g1 = pltpu.roll(gate, 1, 0)
    g2 = pltpu.roll(gate, 2, 0)
    if carry_mode:
        @pl.when(i % tiles_per_seq == 0)
        def _():
            carry_ref[...] = st_ref[0]

        c0 = carry_ref[0:1, :]
        c1 = carry_ref[1:2, :]
        g1 = jnp.where(row == 0, c1, g1)
        g2 = jnp.where(row == 0, c0, jnp.where(row == 1, c1, g2))
        carry_ref[...] = gate[tm - 2:tm, :]
    else:
        t = row % seg
        g1 = jnp.where(t == 0, f1_ref[...], g1)
        g2 = jnp.where(t < 2, f2_ref[...], g2)
    conv = cb_ref[...] + cw_ref[0:1, :] * g2 + cw_ref[1:2, :] * g1 + cw_ref[2:3, :] * gate
    a_ref[...] = (val * (conv * _sigmoid(conv))).astype(a_ref.dtype)
    nt = tail_ref.shape[1]
    tail_ref[0] = gate[tm - nt:tm, :]


def ffn_up(h, w_up, d_ff, conv_w, conv_b, *, tm, state=None, fill=None, seg=None):
    m, kdim = h.shape
    tn = _tile(d_ff, 256)
    nj = d_ff // tn
    tm = _tile(m, tm)
    carry_mode = state is not None
    in_specs = [pl.BlockSpec((tm, kdim), lambda j, i: (i, 0)),
                pl.BlockSpec((kdim, tn), lambda j, i: (0, j)),
                pl.BlockSpec((kdim, tn), lambda j, i: (0, nj + j)),
                pl.BlockSpec((3, tn), lambda j, i: (0, j)),
                pl.BlockSpec((1, tn), lambda j, i: (0, j))]
    args = [h, w_up, w_up, conv_w, conv_b.reshape(1, d_ff)]
    if carry_mode:
        nseq = state.shape[0]
        tps = (m // nseq) // tm
        in_specs.append(pl.BlockSpec((1, 2, tn), lambda j, i: (i // tps, 0, j)))
        args.append(state)
        ngrp, nt = nseq, SUBLANES
        tail_map = lambda j, i: (i // tps, 0, j)
        scratch = [pltpu.VMEM((2, tn), F32)]
    else:
        tps = 1
        for f in fill:
            in_specs.append(pl.BlockSpec((tm, tn), lambda j, i: (i, j)))
            args.append(f)
        ngrp, nt = m // tm, tm
        tail_map = lambda j, i: (i, 0, j)
        scratch = []
    a, tail = pl.pallas_call(
        functools.partial(_ffn_up_kernel, tiles_per_seq=tps, seg=seg, carry_mode=carry_mode),
        grid=(nj, m // tm),
        in_specs=in_specs,
        out_specs=[pl.BlockSpec((tm, tn), lambda j, i: (i, j)),
                   pl.BlockSpec((1, nt, tn), tail_map)],
        out_shape=[jax.ShapeDtypeStruct((m, d_ff), BF16),
                   jax.ShapeDtypeStruct((ngrp, nt, d_ff), F32)],
        scratch_shapes=scratch,
        compiler_params=_params(("parallel", "arbitrary")),
        name="ffn_up",
    )(*args)
    return a, tail


def _ret_kernel(lg_ref, q_ref, k_ref, v_ref, g_ref, cos_ref, sin_ref, s0_ref,
                o_ref, s_out_ref, s_scr, *, c_valid, dk):
    h = pl.program_id(1)
    c = pl.program_id(2)
    nc = pl.num_programs(2)
    half = dk // 2

    @pl.when(c == 0)
    def _():
        s_scr[...] = s0_ref[0, 0]

    lg = lg_ref[h]
    cos = cos_ref[...]
    sin = sin_ref[...]

    def rope(x):
        x1 = x[:, :half]
        x2 = x[:, half:]
        return jnp.concatenate([x1 * cos - x2 * sin, x2 * cos + x1 * sin], axis=1)

    q = rope(q_ref[...].astype(F32))
    k = rope(k_ref[...].astype(F32)) * (dk ** -0.5)
    v = v_ref[...]
    cp = q.shape[0]
    ri = lax.broadcasted_iota(jnp.int32, (cp, 1), 0)
    rif = ri.astype(F32)
    read_decay = jnp.exp(lg * (rif + 1.0))
    write_decay = jnp.where(ri < c_valid, jnp.exp(lg * jnp.maximum(c_valid - 1.0 - rif, 0.0)), 0.0)
    di = lax.broadcasted_iota(jnp.int32, (cp, cp), 0)
    dj = lax.broadcasted_iota(jnp.int32, (cp, cp), 1)
    diff = di - dj
    intra = jnp.where((diff >= 0) & (dj < c_valid),
                      jnp.exp(lg * jnp.maximum(diff, 0).astype(F32)), 0.0)

    s = s_scr[...]
    att = _dot_nt(q.astype(BF16), k.astype(BF16)) * intra
    o = _dot(att.astype(BF16), v) + _dot((q * read_decay).astype(BF16), s.astype(BF16))
    kw = (k * write_decay).astype(BF16)
    chunk_decay = jnp.exp(lg * jnp.full((1, 1), float(c_valid), F32))
    s_new = s * chunk_decay + lax.dot_general(kw, v, (((0,), (0,)), ((), ())), preferred_element_type=F32)
    s_scr[...] = s_new

    @pl.when(c == nc - 1)
    def _():
        s_out_ref[0, 0] = s_new

    on = o * lax.rsqrt(jnp.mean(o * o, axis=-1, keepdims=True) + EPS)
    gate = g_ref[...].astype(F32)
    o_ref[...] = (on * (gate * _sigmoid(gate))).astype(o_ref.dtype)


def retention(proj, s0, cos, sin, *, nb, heads, dk, dv, chunk, c_valid):
    m = proj.shape[0]
    t = m // nb
    nc = t // chunk
    qb = heads
    vb = 2 * heads * dk // dv
    log_g = jnp.log1p(-jnp.exp2(-5.0 - jnp.arange(heads, dtype=F32)))
    grid_spec = pltpu.PrefetchScalarGridSpec(
        num_scalar_prefetch=1,
        grid=(nb, heads, nc),
        in_specs=[
            pl.BlockSpec((chunk, dk), lambda b, h, c, lg: (b * nc + c, h)),
            pl.BlockSpec((chunk, dk), lambda b, h, c, lg: (b * nc + c, qb + h)),
            pl.BlockSpec((chunk, dv), lambda b, h, c, lg: (b * nc + c, vb + h)),
            pl.BlockSpec((chunk, dv), lambda b, h, c, lg: (b * nc + c, vb + heads + h)),
            pl.BlockSpec((chunk, dk // 2), lambda b, h, c, lg: (c, 0)),
            pl.BlockSpec((chunk, dk // 2), lambda b, h, c, lg: (c, 0)),
            pl.BlockSpec((1, 1, dk, dv), lambda b, h, c, lg: (b, h, 0, 0)),
        ],
        out_specs=[
            pl.BlockSpec((chunk, dv), lambda b, h, c, lg: (b * nc + c, h)),
            pl.BlockSpec((1, 1, dk, dv), lambda b, h, c, lg: (b, h, 0, 0)),
        ],
        scratch_shapes=[pltpu.VMEM((dk, dv), F32)],
    )
    o, s = pl.pallas_call(
        functools.partial(_ret_kernel, c_valid=c_valid, dk=dk),
        grid_spec=grid_spec,
        out_shape=[jax.ShapeDtypeStruct((m, heads * dv), BF16),
                   jax.ShapeDtypeStruct((nb, heads, dk, dv), F32)],
        compiler_params=_params(("parallel", "parallel", "arbitrary")),
        name="retention",
    )(log_g, proj, proj, proj, proj, cos, sin, s0)
    return o, s


def _cmp_proj_kernel(*refs, n_in, paged, kvh, hd):
    pos = 1 if paged else 0
    in_refs = refs[pos:pos + n_in]
    w_ref = refs[pos + n_in]
    p_ref = refs[pos + n_in + 1]
    r_ref = refs[pos + n_in + 2]
    rows = r_ref.shape[1] // n_in
    for u in range(n_in):
        src = in_refs[u][0] if paged else in_refs[u][...]
        for hh in range(2 * kvh):
            r_ref[hh, u * rows:(u + 1) * rows, :] = src[:, hh * hd:(hh + 1) * hd]
    n_seg = r_ref.shape[1] // CMP_STRIDE
    for br in range(2):
        acc = jnp.zeros((kvh * n_seg, 2 * hd), F32)
        for sp in range(CMP_STRIDE // 2):
            pieces = []
            for g in range(kvh):
                plane = r_ref.at[br * kvh + g]
                a = plane[pl.ds(2 * sp, n_seg, stride=CMP_STRIDE), :]
                b = plane[pl.ds(2 * sp + 1, n_seg, stride=CMP_STRIDE), :]
                pieces.append(jnp.concatenate([a, b], axis=1))
            lhs = jnp.concatenate(pieces, axis=0).astype(BF16)
            acc = acc + _dot(lhs, w_ref[br, sp])
        for g in range(kvh):
            p_ref[0, br * kvh + g] = acc[g * n_seg:(g + 1) * n_seg, :]


def _cmp_w1_pairs(cmp_w1, hd):
    r = CMP_LEN // CMP_STRIDE
    e = cmp_w1.shape[-1]
    w = cmp_w1.reshape(2, r, CMP_STRIDE // 2, 2, hd, e)
    w = w.transpose(0, 2, 3, 4, 1, 5)
    return w.reshape(2, CMP_STRIDE // 2, 2 * hd, r * e).astype(BF16)


def cmp_proj_rows(rows2d, nb, t, w_pairs, kvh, hd):
    n_seg = t // CMP_STRIDE
    width = 2 * kvh * hd
    return pl.pallas_call(
        functools.partial(_cmp_proj_kernel, n_in=1, paged=False, kvh=kvh, hd=hd),
        grid=(nb,),
        in_specs=[pl.BlockSpec((t, width), lambda b: (b, 0)),
                  pl.BlockSpec(w_pairs.shape, lambda b: (0, 0, 0, 0))],
        out_specs=pl.BlockSpec((1, 2 * kvh, n_seg, 2 * hd), lambda b: (b, 0, 0, 0)),
        out_shape=jax.ShapeDtypeStruct((nb, 2 * kvh, n_seg, 2 * hd), F32),
        scratch_shapes=[pltpu.VMEM((2 * kvh, t, hd), F32)],
        compiler_params=_params(("parallel",)),
        name="cmp_proj_prompt",
    )(rows2d, w_pairs)


def cmp_proj_pages(cache3d, page_table, w_pairs, kvh, hd, pages_per_step):
    nb, n_pages = page_table.shape
    page = cache3d.shape[1]
    width = 2 * kvh * hd
    n_seg = pages_per_step * page // CMP_STRIDE
    nchunk = n_pages // pages_per_step
    in_specs = [pl.BlockSpec((1, page, width),
                             functools.partial(lambda b, c, pt, u: (pt[b, c * pages_per_step + u], 0, 0), u=u))
                for u in range(pages_per_step)]
    in_specs.append(pl.BlockSpec(w_pairs.shape, lambda b, c, pt: (0, 0, 0, 0)))
    grid_spec = pltpu.PrefetchScalarGridSpec(
        num_scalar_prefetch=1,
        grid=(nb, nchunk),
        in_specs=in_specs,
        out_specs=pl.BlockSpec((1, 2 * kvh, n_seg, 2 * hd), lambda b, c, pt: (b, 0, c, 0)),
        scratch_shapes=[pltpu.VMEM((2 * kvh, pages_per_step * page, hd), F32)],
    )
    return pl.pallas_call(
        functools.partial(_cmp_proj_kernel, n_in=pages_per_step, paged=True, kvh=kvh, hd=hd),
        grid_spec=grid_spec,
        out_shape=jax.ShapeDtypeStruct((nb, 2 * kvh, nchunk * n_seg, 2 * hd), F32),
        compiler_params=_params(("parallel", "parallel")),
        name="cmp_proj_sample",
    )(page_table, *([cache3d] * pages_per_step), w_pairs)


def _cmp_pe_kernel(pe_ref, w_ref, o_ref):
    o_ref[0] = _dot(pe_ref[0], w_ref[0])


def cmp_pe_term(cmp_pe, cmp_w1):
    _, n, hd = cmp_pe.shape
    e = cmp_w1.shape[-1]
    pe_flat = jnp.zeros((2, SUBLANES, n * hd), BF16).at[:, 0].set(cmp_pe.reshape(2, n * hd).astype(BF16))
    w_flat = cmp_w1.reshape(2, n * hd, e).astype(BF16)
    return pl.pallas_call(
        _cmp_pe_kernel,
        grid=(2,),
        in_specs=[pl.BlockSpec((1, SUBLANES, n * hd), lambda b: (b, 0, 0)),
                  pl.BlockSpec((1, n * hd, e), lambda b: (b, 0, 0))],
        out_specs=pl.BlockSpec((1, SUBLANES, e), lambda b: (b, 0, 0)),
        out_shape=jax.ShapeDtypeStruct((2, SUBLANES, e), F32),
        compiler_params=_params(("parallel",)),
        name="cmp_pe",
    )(pe_flat, w_flat)


def _cmp_finish_kernel(p_ref, pe_ref, w2_ref, o_ref):
    p = p_ref[0, 0]
    n_seg = p.shape[0]
    e = p.shape[1] // 2
    hid = p[:, :e] + pltpu.roll(p[:, e:], n_seg - 1, 0) + pe_ref[0, 0:1, :]
    c = math.sqrt(2.0 / math.pi)
    act = 0.5 * hid * (1.0 + jnp.tanh(c * (hid + 0.044715 * (hid * hid * hid))))
    o_ref[0, 0] = _dot(act.astype(BF16), w2_ref[0]).astype(o_ref.dtype)


def cmp_finish(p, pe_term, cmp_w2, kvh):
    nb, nu, n_seg, e2 = p.shape
    e = e2 // 2
    hd = cmp_w2.shape[-1]
    return pl.pallas_call(
        _cmp_finish_kernel,
        grid=(nb, nu),
        in_specs=[pl.BlockSpec((1, 1, n_seg, e2), lambda b, u: (b, u, 0, 0)),
                  pl.BlockSpec((1, SUBLANES, e), lambda b, u: (u // kvh, 0, 0)),
                  pl.BlockSpec((1, e, hd), lambda b, u: (u // kvh, 0, 0))],
        out_specs=pl.BlockSpec((1, 1, n_seg, hd), lambda b, u: (b, u, 0, 0)),
        out_shape=jax.ShapeDtypeStruct((nb, nu, n_seg, hd), BF16),
        compiler_params=_params(("parallel", "parallel")),
        name="cmp_finish",
    )(p, pe_term, cmp_w2.astype(BF16))


def _cmp_select_kernel(q_ref, kc_ref, vc_ref, agg_ref, o_ref, *rest,
                       group, hd, n_c, n_slc, pos0, emit_idx):
    if emit_idx:
        idx_ref, ok_ref, sc_ref = rest
    else:
        sel_ref, sc_ref = rest
    qi = pl.program_id(2)
    tq = q_ref.shape[0]
    kc = kc_ref[0, 0]
    vc = vc_ref[0, 0]
    agg = agg_ref[...]
    n_cp = kc.shape[0]
    npad = agg.shape[0]
    base = pos0 + qi * tq
    qpos_c = base + lax.broadcasted_iota(jnp.int32, (tq, 1), 0)
    ci = lax.broadcasted_iota(jnp.int32, (1, n_cp), 1)
    cm = ((ci * CMP_STRIDE + (CMP_LEN - 1)) <= qpos_c) & (ci < n_c)
    imp_t = jnp.zeros((npad, tq), F32)
    for hh in range(group):
        qh = q_ref[:, hh * hd:(hh + 1) * hd]
        s = _dot_nt(qh, kc) * (hd ** -0.5)
        s = jnp.where(cm, s, NEG)
        e = jnp.where(cm, jnp.exp(s - jnp.max(s, axis=-1, keepdims=True)), 0.0)
        p = e / jnp.maximum(jnp.sum(e, axis=-1, keepdims=True), TINY)
        pb = p.astype(BF16)
        o_ref[:, hh * hd:(hh + 1) * hd] = _dot(pb, vc).astype(o_ref.dtype)
        imp_t = imp_t + _dot_nt(agg, pb)

    blk = lax.broadcasted_iota(jnp.int32, (npad, tq), 0)
    qpos_r = base + lax.broadcasted_iota(jnp.int32, (1, tq), 1)
    cur = jnp.right_shift(qpos_r, SLC_BLOCK.bit_length() - 1)
    visible = blk * SLC_BLOCK <= qpos_r
    forced = (blk == 0) | (blk == cur) | (blk == cur - 1)
    score = jnp.where(visible, jnp.where(forced, FORCE, imp_t), NEG)
    sc_ref[...] = score

    def count(i, cnt):
        si = sc_ref[pl.ds(i, 1), :]
        beats = (si > score) | ((si == score) & (i < blk))
        return cnt + jnp.where(beats, 1, 0)

    rank = lax.fori_loop(0, n_slc, count, jnp.zeros((npad, tq), jnp.int32))
    in_range = blk < n_slc
    if emit_idx:
        blk_f = blk.astype(F32)
        for r in range(min(N_SELECT, n_slc)):
            hit = (rank == r) & in_range
            idx_r = jnp.sum(jnp.where(hit, blk_f, 0.0), axis=0, keepdims=True)
            idx_ref[0, 0, r:r + 1, :] = idx_r.astype(jnp.int32)
            ok_ref[0, 0, r:r + 1, :] = jnp.sum(jnp.where(hit & visible, 1.0, 0.0), axis=0, keepdims=True)
    else:
        sel_t = jnp.where((rank < N_SELECT) & visible & in_range, 1.0, 0.0)
        sel_ref[0, 0] = sel_t.T.astype(sel_ref.dtype)


def _slc_aggregation_t(n_cp, n_slc, npad):
    rs = SLC_BLOCK // CMP_STRIDE
    rc = CMP_LEN // CMP_STRIDE
    i = jnp.arange(n_cp)[None, :]
    j = jnp.arange(npad)[:, None]
    w = sum((i == j * rs + m - n).astype(F32) for m in range(rs) for n in range(rc))
    return jnp.where(j < n_slc, w, 0.0).astype(BF16)


def cmp_select(q, kvc, *, nb, kvh, group, hd, n_c, n_slc, pos0, tq, emit_idx):
    m = q.shape[0]
    t = m // nb
    tq = _tile(t, tq)
    nq = t // tq
    n_cp = kvc.shape[2]
    npad = -(-n_slc // LANES) * LANES
    agg_t = _slc_aggregation_t(n_cp, n_slc, npad)
    gw = group * hd
    in_specs = [pl.BlockSpec((tq, gw), lambda b, g, i: (b * nq + i, g)),
                pl.BlockSpec((1, 1, n_cp, hd), lambda b, g, i: (b, g, 0, 0)),
                pl.BlockSpec((1, 1, n_cp, hd), lambda b, g, i: (b, kvh + g, 0, 0)),
                pl.BlockSpec((npad, n_cp), lambda b, g, i: (0, 0))]
    o_spec = pl.BlockSpec((tq, gw), lambda b, g, i: (b * nq + i, g))
    o_shape = jax.ShapeDtypeStruct((m, kvh * gw), BF16)
    if emit_idx:
        nsel = min(N_SELECT, n_slc)
        out_specs = [o_spec,
                     pl.BlockSpec((1, 1, nsel, tq), lambda b, g, i: (b, g, 0, i)),
                     pl.BlockSpec((1, 1, nsel, tq), lambda b, g, i: (b, g, 0, i))]
        out_shape = [o_shape,
                     jax.ShapeDtypeStruct((nb, kvh, nsel, t), jnp.int32),
                     jax.ShapeDtypeStruct((nb, kvh, nsel, t), F32)]
    else:
        out_specs = [o_spec, pl.BlockSpec((1, 1, tq, npad), lambda b, g, i: (b, g, i, 0))]
        out_shape = [o_shape, jax.ShapeDtypeStruct((nb, kvh, t, npad), BF16)]
    return pl.pallas_call(
        functools.partial(_cmp_select_kernel, group=group, hd=hd, n_c=n_c, n_slc=n_slc,
                          pos0=pos0, emit_idx=emit_idx),
        grid=(nb, kvh, nq),
        in_specs=in_specs, out_specs=out_specs, out_shape=out_shape,
        scratch_shapes=[pltpu.VMEM((npad, tq), F32)],
        compiler_params=_params(("parallel", "parallel", "parallel")),
        name="cmp_select",
    )(q, kvc, kvc, agg_t)


def _flash_kernel(*refs, mode, tk, hd):
    if mode == "slc":
        q_ref, k_ref, v_ref, sel_ref, e_ref, o_ref = refs
    else:
        q_ref, k_ref, v_ref, o_ref = refs
    qi = pl.program_id(3)
    q = q_ref[...]
    tq = q.shape[0]
    q0 = qi * tq
    qpos = q0 + lax.broadcasted_iota(jnp.int32, (tq, 1), 0)
    hi = (q0 + tq + tk - 1) // tk
    if mode == "slc":
        lo = 0
        sel = sel_ref[0, 0]
    else:
        lo = jnp.maximum(q0 - (WINDOW - 1), 0) // tk

    def body(kt, carry):
        m_i, l_i, acc = carry
        ks = pl.multiple_of(kt * tk, tk)
        k = k_ref[pl.ds(ks, tk), :]
        v = v_ref[pl.ds(ks, tk), :]
        s = _dot_nt(q, k) * (hd ** -0.5)
        kpos = ks + lax.broadcasted_iota(jnp.int32, (1, tk), 1)
        if mode == "slc":
            mask = (kpos <= qpos) & (_dot(sel, e_ref[kt]) > 0.5)
        else:
            d = qpos - kpos
            mask = (d >= 0) & (d < WINDOW)
        s = jnp.where(mask, s, NEG)
        m_new = jnp.maximum(m_i, jnp.max(s, axis=-1, keepdims=True))
        alpha = jnp.exp(m_i - m_new)
        p = jnp.where(mask, jnp.exp(s - m_new), 0.0)
        l_new = alpha * l_i + jnp.sum(p, axis=-1, keepdims=True)
        acc = alpha * acc + _dot(p.astype(BF16), v)
        return m_new, l_new, acc

    init = (jnp.full((tq, 1), NEG, F32), jnp.zeros((tq, 1), F32), jnp.zeros((tq, hd), F32))
    _, l_f, acc = lax.fori_loop(lo, hi, body, init)
    o_ref[...] = (acc / l_f).astype(o_ref.dtype)


def flash_prompt(q, kv16, sel, *, mode, nb, kvh, group, hd, kcol, vcol, tq, tk):
    m = q.shape[0]
    t = m // nb
    tq = _tile(t, tq)
    tk = _tile(t, tk)
    nq = t // tq
    in_specs = [pl.BlockSpec((tq, hd), lambda b, g, h, i: (b * nq + i, g * group + h)),
                pl.BlockSpec((t, hd), lambda b, g, h, i: (b, kcol * kvh + g)),
                pl.BlockSpec((t, hd), lambda b, g, h, i: (b, vcol * kvh + g))]
    args = [q, kv16, kv16]
    if mode == "slc":
        npad = sel.shape[-1]
        n_kt = t // tk
        per = tk // SLC_BLOCK
        jj = jnp.arange(npad)[None, :, None]
        cc = jnp.arange(tk)[None, None, :]
        kk = jnp.arange(n_kt)[:, None, None]
        expand = (jj == kk * per + cc // SLC_BLOCK).astype(BF16)
        in_specs += [pl.BlockSpec((1, 1, tq, npad), lambda b, g, h, i: (b, g, i, 0)),
                     pl.BlockSpec((n_kt, npad, tk), lambda b, g, h, i: (0, 0, 0))]
        args += [sel, expand]
    return pl.pallas_call(
        functools.partial(_flash_kernel, mode=mode, tk=tk, hd=hd),
        grid=(nb, kvh, group, nq),
        in_specs=in_specs,
        out_specs=pl.BlockSpec((tq, hd), lambda b, g, h, i: (b * nq + i, g * group + h)),
        out_shape=jax.ShapeDtypeStruct((m, kvh * group * hd), BF16),
        compiler_params=_params(("parallel", "parallel", "parallel", "parallel")),
        name="flash_" + mode,
    )(*args)


def _sample_attn_kernel(idx_ref, pt_ref, q_ref, kpos_ref, okx_ref, kn_ref, vn_ref, wk_ref, wv_ref,
                        wkn_ref, wvn_ref, cache_ref, oslc_ref, owin_ref, kbuf, vbuf, sem,
                        *, kvh, group, hd, t_valid, past, nsel, ks_sec, vs_sec):
    b = pl.program_id(0)
    g = pl.program_id(1)
    tp = q_ref.shape[2] // group
    n_blocks = past // SLC_BLOCK
    per_page = PAGE_SIZE // SLC_BLOCK
    n_fetch = t_valid * nsel

    def copies(f):
        t = f // nsel
        r = f % nsel
        blk = idx_ref[((b * kvh + g) * nsel + r) * tp + t]
        blk = jnp.minimum(blk, n_blocks - 1)
        page = pt_ref[b * (past // PAGE_SIZE) + blk // per_page]
        row0 = (blk % per_page) * SLC_BLOCK
        dst = pl.ds(f * SLC_BLOCK, SLC_BLOCK)
        ck = pltpu.make_async_copy(cache_ref.at[page, pl.ds(row0, SLC_BLOCK), ks_sec * kvh + g, :],
                                   kbuf.at[dst, :], sem.at[0])
        cv = pltpu.make_async_copy(cache_ref.at[page, pl.ds(row0, SLC_BLOCK), vs_sec * kvh + g, :],
                                   vbuf.at[dst, :], sem.at[1])
        return ck, cv

    def start(f, c):
        ck, cv = copies(f)
        ck.start()
        cv.start()
        return c

    def wait(f, c):
        ck, cv = copies(f)
        ck.wait()
        cv.wait()
        return c

    lax.fori_loop(0, n_fetch, start, 0)

    scale = hd ** -0.5
    qa = q_ref[0, 0].astype(BF16)
    trow = jnp.right_shift(lax.broadcasted_iota(jnp.int32, (tp * group, 1), 0), group.bit_length() - 1)
    wk = wk_ref[0].astype(BF16)
    wrows = wk.shape[0]
    s_c = _dot_nt(qa, wk) * scale
    s_n = _dot_nt(qa, wkn_ref[...].astype(BF16)) * scale
    d_c = (past + trow) - (past - wrows + lax.broadcasted_iota(jnp.int32, (1, wrows), 1))
    jn = lax.broadcasted_iota(jnp.int32, (1, tp), 1)
    d_n = trow - jn
    m_c = (d_c >= 0) & (d_c < WINDOW)
    m_n = (d_n >= 0) & (d_n < WINDOW) & (jn < t_valid)
    s_c = jnp.where(m_c, s_c, NEG)
    s_n = jnp.where(m_n, s_n, NEG)
    mx = jnp.maximum(jnp.max(s_c, axis=-1, keepdims=True), jnp.max(s_n, axis=-1, keepdims=True))
    e_c = jnp.where(m_c, jnp.exp(s_c - mx), 0.0)
    e_n = jnp.where(m_n, jnp.exp(s_n - mx), 0.0)
    den = jnp.maximum(jnp.sum(e_c, axis=-1, keepdims=True) + jnp.sum(e_n, axis=-1, keepdims=True), TINY)
    o_w = _dot((e_c / den).astype(BF16), wv_ref[0].astype(BF16)) + \
        _dot((e_n / den).astype(BF16), wvn_ref[...].astype(BF16))
    owin_ref[0, 0] = o_w.astype(owin_ref.dtype)

    lax.fori_loop(0, n_fetch, wait, 0)

    kn = kn_ref[...].astype(BF16)
    vn = vn_ref[...].astype(BF16)
    span = nsel * SLC_BLOCK
    oslc_ref[...] = jnp.zeros(oslc_ref.shape, oslc_ref.dtype)
    for t in range(t_valid):
        qt = q_ref[0, 0, t * group:(t + 1) * group, :].astype(BF16)
        kt = kbuf[t * span:(t + 1) * span, :].astype(BF16)
        vt = vbuf[t * span:(t + 1) * span, :].astype(BF16)
        kp = kpos_ref[0, 0, t:t + 1, :]
        okv = okx_ref[0, 0, t:t + 1, :] > 0.5
        qpos = past + t
        m_g = (kp <= qpos) & okv & (kp < past)
        s_g = jnp.where(m_g, _dot_nt(qt, kt) * scale, NEG)
        selrow = jnp.zeros((1, tp), F32)
        for j in range(t_valid):
            hit = jnp.max(jnp.where((kp == past + j) & okv, 1.0, 0.0), axis=-1, keepdims=True)
            selrow = selrow + jnp.where(jn == j, hit, 0.0)
        m_w = (selrow > 0.5) & (jn <= t)
        s_w = jnp.where(m_w, _dot_nt(qt, kn) * scale, NEG)
        mx = jnp.maximum(jnp.max(s_g, axis=-1, keepdims=True), jnp.max(s_w, axis=-1, keepdims=True))
        e_g = jnp.where(m_g, jnp.exp(s_g - mx), 0.0)
        e_w = jnp.where(m_w, jnp.exp(s_w - mx), 0.0)
        den = jnp.sum(e_g, axis=-1, keepdims=True) + jnp.sum(e_w, axis=-1, keepdims=True)
        o_t = _dot((e_g / den).astype(BF16), vt) + _dot((e_w / den).astype(BF16), vn)
        oslc_ref[0, 0, t * group:(t + 1) * group, :] = o_t.astype(oslc_ref.dtype)


def sample_attn(q4, idx, okf, kv_new, cache_kv4, cache_win3, page_table, *, kvh, group, hd, t_valid, past):
    nb = q4.shape[0]
    tp = q4.shape[2] // group
    nsel = idx.shape[2]
    span = nsel * SLC_BLOCK
    wrows = cache_win3.shape[1]
    idx_t = idx.transpose(0, 1, 3, 2)
    kpos = (idx_t[..., None] * SLC_BLOCK + jnp.arange(SLC_BLOCK, dtype=jnp.int32)).reshape(nb, kvh, tp, span)
    okx = jnp.broadcast_to(okf.transpose(0, 1, 3, 2)[..., None], (nb, kvh, tp, nsel, SLC_BLOCK)).reshape(
        nb, kvh, tp, span)
    grid_spec = pltpu.PrefetchScalarGridSpec(
        num_scalar_prefetch=2,
        grid=(nb, kvh),
        in_specs=[
            pl.BlockSpec((1, 1, tp * group, hd), lambda b, g, *_: (b, g, 0, 0)),
            pl.BlockSpec((1, 1, tp, span), lambda b, g, *_: (b, g, 0, 0)),
            pl.BlockSpec((1, 1, tp, span), lambda b, g, *_: (b, g, 0, 0)),
            pl.BlockSpec((tp, hd), lambda b, g, *_: (b, 2 * kvh + g)),
            pl.BlockSpec((tp, hd), lambda b, g, *_: (b, 3 * kvh + g)),
            pl.BlockSpec((1, wrows, hd), lambda b, g, *_: (b, 0, g)),
            pl.BlockSpec((1, wrows, hd), lambda b, g, *_: (b, 0, kvh + g)),
            pl.BlockSpec((tp, hd), lambda b, g, *_: (b, 4 * kvh + g)),
            pl.BlockSpec((tp, hd), lambda b, g, *_: (b, 5 * kvh + g)),
            pl.BlockSpec(memory_space=pl.ANY),
        ],
        out_specs=[pl.BlockSpec((1, 1, tp * group, hd), lambda b, g, *_: (b, g, 0, 0)),
                   pl.BlockSpec((1, 1, tp * group, hd), lambda b, g, *_: (b, g, 0, 0))],
        scratch_shapes=[pltpu.VMEM((t_valid * span, hd), F32),
                        pltpu.VMEM((t_valid * span, hd), F32),
                        pltpu.SemaphoreType.DMA((2,))],
    )
    return pl.pallas_call(
        functools.partial(_sample_attn_kernel, kvh=kvh, group=group, hd=hd, t_valid=t_valid, past=past,
                          nsel=nsel, ks_sec=2, vs_sec=3),
        grid_spec=grid_spec,
        out_shape=[jax.ShapeDtypeStruct(q4.shape, F32), jax.ShapeDtypeStruct(q4.shape, F32)],
        compiler_params=_params(("arbitrary", "arbitrary")),
        name="sample_attn",
    )(idx.reshape(-1), page_table.reshape(-1), q4, kpos, okx, kv_new, kv_new, cache_win3, cache_win3,
      kv_new, kv_new, cache_kv4)


def _combine_kernel(oc_ref, os_ref, ow_ref, g_ref, o_ref, *, heads, hd):
    gates = g_ref[...]
    for h in range(heads):
        sl = slice(h * hd, (h + 1) * hd)
        o = (gates[:, h:h + 1] * oc_ref[:, sl].astype(F32)
             + gates[:, heads + h:heads + h + 1] * os_ref[:, sl].astype(F32)
             + gates[:, 2 * heads + h:2 * heads + h + 1] * ow_ref[:, sl].astype(F32))
        o_ref[:, sl] = o.astype(o_ref.dtype)


def combine(o_cmp, o_slc, o_win, gates, heads, hd):
    m, d = o_cmp.shape
    tm = _tile(m, 256)
    spec = pl.BlockSpec((tm, d), lambda i: (i, 0))
    return pl.pallas_call(
        functools.partial(_combine_kernel, heads=heads, hd=hd),
        grid=(m // tm,),
        in_specs=[spec, spec, spec, pl.BlockSpec((tm, 3 * heads), lambda i: (i, 0))],
        out_specs=spec,
        out_shape=jax.ShapeDtypeStruct((m, d), BF16),
        compiler_params=_params(("parallel",)),
        name="combine",
    )(o_cmp, o_slc, o_win, gates)


def _rope_tables_half(pos, half, theta):
    inv = jnp.power(theta, -jnp.arange(half, dtype=F32) / half)
    ang = pos.astype(F32)[:, None] * inv[None, :]
    return jnp.cos(ang), jnp.sin(ang)


def _nsa_rope_tables(pos, hd):
    half = hd // 8
    cos, sin = _rope_tables_half(pos, half, ROPE_THETA)
    n = pos.shape[0]
    ones = jnp.ones((n, hd - 2 * half), F32)
    zeros = jnp.zeros((n, hd - 2 * half), F32)
    zh = jnp.zeros((n, half), F32)
    c = jnp.concatenate([cos, cos, ones], axis=1)
    s_up = jnp.concatenate([-sin, zh, zeros], axis=1)
    s_dn = jnp.concatenate([zh, sin, zeros], axis=1)
    return c, s_up, s_dn


def _trunk(x2d, w, *, nb, t_rows, t_valid, pos0, ret_s0, conv_state, sample_ctx):
    m, d = x2d.shape
    is_sample = sample_ctx is not None
    tm = 1024 if not is_sample else m
    d_ff = w["ffn_conv_w"].shape[-1]
    heads_r = RET_HEADS
    dk = d // heads_r
    dv = 2 * dk
    hd = d // NSA_HEADS
    kvh = NSA_KV_HEADS
    group = NSA_HEADS // kvh
    qw = NSA_HEADS * hd
    pos = pos0 + jnp.arange(t_rows)

    def ffn(x, layer):
        h = rmsnorm(x, w["norm_ffn"][layer], BF16)
        if is_sample:
            st = conv_state[layer]
            f1 = jnp.zeros((nb, t_rows, d_ff), F32).at[:, 0].set(st[:, 1])
            f2 = jnp.zeros((nb, t_rows, d_ff), F32).at[:, 0].set(st[:, 0]).at[:, 1].set(st[:, 1])
            a, tail = ffn_up(h, w["ffn_w_up16"][layer], d_ff, w["ffn_conv_w"][layer], w["ffn_conv_b"][layer],
                             tm=tm, fill=(f1.reshape(m, d_ff), f2.reshape(m, d_ff)), seg=t_rows)
            cs = tail.reshape(nb, t_rows, d_ff)[:, t_valid - 2:t_valid]
        else:
            a, tail = ffn_up(h, w["ffn_w_up16"][layer], d_ff, w["ffn_conv_w"][layer], w["ffn_conv_b"][layer],
                             tm=tm, state=conv_state[layer])
            cs = tail[:, SUBLANES - 2:]
        y = matmul(a, w["ffn_w_down16"][layer], d, tm=tm, tn=512, tk=d_ff // 2, out_dtypes=[F32], res=x)
        return y, cs

    h = rmsnorm(x2d, w["norm_mix"][0], BF16)
    proj = matmul(h, w["ret_w_in16"], w["ret_w_in16"].shape[-1], tm=tm, tn=1024, tk=d, out_dtypes=[BF16])
    chunk = math.gcd(t_valid, RET_CHUNK)
    cpad = t_rows if is_sample else chunk
    cos_r, sin_r = _rope_tables_half(pos, dk // 2, RET_THETA)
    o_ret, s_ret = retention(proj, ret_s0, cos_r, sin_r, nb=nb, heads=heads_r, dk=dk, dv=dv,
                             chunk=cpad, c_valid=chunk)
    x1 = matmul(o_ret, w["ret_w_out16"], d, tm=tm, tn=1024, tk=2048, out_dtypes=[F32], res=x2d)
    x2, conv0 = ffn(x1, 0)

    hk = rmsnorm(x2, w["kv_norm"], BF16)
    rope_t = _nsa_rope_tables(pos, hd)
    if is_sample:
        rope_t = tuple(jnp.tile(tb, (nb, 1)) for tb in rope_t)
    kv32, kv16 = matmul(hk, w["kv_w16"], 6 * kvh * hd, tm=tm, tn=kvh * hd, tk=d, out_dtypes=[F32, BF16],
                        rope=rope_t, rope_every=2, rope_half=hd // 8)
    kv_rows = kv32[:, :4 * kvh * hd].reshape(nb, t_rows, 4, kvh, hd)[:, :t_valid]
    win_new = kv32[:, 4 * kvh * hd:].reshape(nb, t_rows, 2, kvh, hd)[:, :t_valid]

    h = rmsnorm(x2, w["norm_mix"][1], BF16)
    q = matmul(h, w["nsa_w_in16"], qw, tm=tm, tn=1024, tk=d, out_dtypes=[BF16], rope=rope_t, rope_every=1,
               rope_half=hd // 8)
    gates = matmul(h, w["nsa_w_gate16"], 3 * NSA_HEADS, tm=tm, tn=3 * NSA_HEADS, tk=d, out_dtypes=[F32],
                   act="sigmoid")
    w_pairs = _cmp_w1_pairs(w["cmp_w1"], hd)
    pe_term = cmp_pe_term(w["cmp_pe"], w["cmp_w1"])
    if not is_sample:
        p = cmp_proj_rows(kv32, nb, t_rows, w_pairs, kvh, hd)
        kvc = cmp_finish(p, pe_term, w["cmp_w2"], kvh)
        n_seg = t_rows // CMP_STRIDE
        n_c = n_seg - CMP_LEN // CMP_STRIDE + 1
        n_slc = -(-t_rows // SLC_BLOCK)
        o_cmp, sel = cmp_select(q, kvc, nb=nb, kvh=kvh, group=group, hd=hd, n_c=n_c, n_slc=n_slc,
                                pos0=pos0, tq=256, emit_idx=False)
        o_slc = flash_prompt(q, kv16, sel, mode="slc", nb=nb, kvh=kvh, group=group, hd=hd, kcol=2, vcol=3,
                             tq=512, tk=512)
        o_win = flash_prompt(q, kv16, None, mode="win", nb=nb, kvh=kvh, group=group, hd=hd, kcol=4, vcol=5,
                             tq=512, tk=512)
        win = win_new[:, t_valid - min(WINDOW, t_valid):]
    else:
        cache_kv, cache_win, page_table = sample_ctx
        past = pos0
        n_pool, page = cache_kv.shape[:2]
        cache3 = cache_kv.reshape(n_pool, page, 4 * kvh * hd)
        p = cmp_proj_pages(cache3, page_table, w_pairs, kvh, hd, pages_per_step=min(16, page_table.shape[1]))
        kvc = cmp_finish(p, pe_term, w["cmp_w2"], kvh)
        n_seg = (past + t_valid) // CMP_STRIDE
        n_c = n_seg - CMP_LEN // CMP_STRIDE + 1
        n_slc = -(-(past + t_valid) // SLC_BLOCK)
        o_cmp, idx, okf = cmp_select(q, kvc, nb=nb, kvh=kvh, group=group, hd=hd, n_c=n_c, n_slc=n_slc,
                                     pos0=pos0, tq=t_rows, emit_idx=True)
        q4 = q.astype(F32).reshape(nb, t_rows, kvh, group, hd).transpose(0, 2, 1, 3, 4).reshape(
            nb, kvh, t_rows * group, hd)
        wrows = cache_win.shape[1]
        o_slc4, o_win4 = sample_attn(q4, idx, okf, kv32, cache_kv.reshape(n_pool, page, 4 * kvh, hd),
                                     cache_win.reshape(nb, wrows, 2 * kvh * hd), page_table,
                                     kvh=kvh, group=group, hd=hd, t_valid=t_valid, past=past)

        def rows(o4):
            return o4.reshape(nb, kvh, t_rows, group, hd).transpose(0, 2, 1, 3, 4).reshape(m, qw).astype(BF16)

        o_slc, o_win = rows(o_slc4), rows(o_win4)
        win = jnp.concatenate([cache_win, win_new], axis=1)[:, t_valid:]
    o = combine(o_cmp, o_slc, o_win, gates, NSA_HEADS, hd)
    x3 = matmul(o, w["nsa_w_out16"], d, tm=tm, tn=512, tk=d, out_dtypes=[F32], res=x2)
    x4, conv1 = ffn(x3, 1)
    y = rmsnorm(x4, w["norm_final"], F32)
    return y, s_ret[None], jnp.stack([conv0, conv1]), kv_rows, win


def kernel(x_prompt, x_sample, cache_kv, cache_win, state_ret, state_conv, page_table, norm_mix, norm_ffn,
           ret_w_in, ret_w_out, kv_norm, kv_w, cmp_pe, cmp_w1, cmp_w2, nsa_w_in, nsa_w_out, ffn_w_up, ffn_conv_w,
           ffn_conv_b, ffn_w_down, norm_final):
    b, t, d = x_prompt.shape
    db, dt, _ = x_sample.shape
    past = page_table.shape[1] * cache_kv.shape[1]
    qw = nsa_w_out.shape[1]
    w = dict(
        norm_mix=norm_mix, norm_ffn=norm_ffn, kv_norm=kv_norm, norm_final=norm_final,
        cmp_pe=cmp_pe, cmp_w1=cmp_w1, cmp_w2=cmp_w2, ffn_conv_w=ffn_conv_w, ffn_conv_b=ffn_conv_b,
        ret_w_in16=ret_w_in[0].astype(BF16), ret_w_out16=ret_w_out[0].astype(BF16), kv_w16=kv_w.astype(BF16),
        nsa_w_in16=nsa_w_in[0].astype(BF16), nsa_w_gate16=nsa_w_in[0][:, qw:].astype(BF16),
        nsa_w_out16=nsa_w_out[0].astype(BF16),
        ffn_w_up16=[ffn_w_up[l].astype(BF16) for l in range(ffn_w_up.shape[0])],
        ffn_w_down16=[ffn_w_down[l].astype(BF16) for l in range(ffn_w_down.shape[0])],
    )
    heads_r = RET_HEADS
    dk = d // heads_r
    ret0 = jnp.zeros((b, heads_r, dk, 2 * dk), F32)
    conv0 = jnp.zeros((state_conv.shape[0], b, 2, ffn_conv_w.shape[-1]), F32)
    y_p, ret_p, conv_p, kv_p, win_p = _trunk(
        x_prompt.reshape(b * t, d), w, nb=b, t_rows=t, t_valid=t, pos0=0, ret_s0=ret0, conv_state=conv0,
        sample_ctx=None)
    tp = 2 * SUBLANES
    xs = jnp.zeros((db, tp, d), F32).at[:, :dt].set(x_sample).reshape(db * tp, d)
    y_s, ret_s, conv_s, kv_s, win_s = _trunk(
        xs, w, nb=db, t_rows=tp, t_valid=dt, pos0=past, ret_s0=state_ret[0], conv_state=state_conv,
        sample_ctx=(cache_kv, cache_win, page_table))
    y_s = y_s.reshape(db, tp, d)[:, :dt]
    return (y_p.reshape(b, t, d), y_s, kv_p, kv_s, win_p, win_s, ret_p, ret_s, conv_p, conv_s)
```

```python
import functools
import math

import jax
import jax.numpy as jnp
from jax import lax
from jax.experimental import pallas as pl
from jax.experimental.pallas import tpu as pltpu

F32 = jnp.float32
BF16 = jnp.bfloat16

RET_HEADS = 16
RET_CHUNK = 512
RET_THETA = 10000.0
NSA_HEADS = 32
NSA_KV_HEADS = 4
CMP_LEN = 32
CMP_STRIDE = 16
SLC_BLOCK = 64
N_SELECT = 16
WINDOW = 512
ROPE_THETA = 500000.0
PAGE_SIZE = 128
EPS = 1e-6
NEG = -1e30
FORCE = 1e9
TINY = 1e-20

LANES = 128
SUBLANES = 8
VMEM_LIMIT = 56 * 1024 * 1024


def _params(sem):
    return pltpu.CompilerParams(dimension_semantics=sem, vmem_limit_bytes=VMEM_LIMIT)


def _tile(dim, pref):
    if dim <= pref:
        return dim
    t = pref
    while dim % t:
        t //= 2
    return t


def _sigmoid(x):
    return 1.0 / (1.0 + jnp.exp(-x))


def _dot(a, b):
    return jnp.dot(a, b, preferred_element_type=F32)


def _dot_nt(a, b):
    return lax.dot_general(a, b, (((1,), (1,)), ((), ())), preferred_element_type=F32)


def _rmsnorm_kernel(x_ref, g_ref, o_ref):
    x = x_ref[...]
    y = x * lax.rsqrt(jnp.mean(x * x, axis=-1, keepdims=True) + EPS)
    o_ref[...] = (y * g_ref[...]).astype(o_ref.dtype)


def rmsnorm(x, g, out_dtype):
    m, d = x.shape
    tm = _tile(m, 256)
    return pl.pallas_call(
        _rmsnorm_kernel,
        grid=(m // tm,),
        in_specs=[pl.BlockSpec((tm, d), lambda i: (i, 0)),
                  pl.BlockSpec((1, d), lambda i: (0, 0))],
        out_specs=pl.BlockSpec((tm, d), lambda i: (i, 0)),
        out_shape=jax.ShapeDtypeStruct((m, d), out_dtype),
        compiler_params=_params(("parallel",)),
        name="rmsnorm",
    )(x, g.reshape(1, d).astype(F32))


def _rope_lanes(x, c, s_up, s_dn, half):
    return x * c + pltpu.roll(x, LANES - half, 1) * s_up + pltpu.roll(x, half, 1) * s_dn


def _mm_kernel(*refs, nk, has_res, rope_every, rope_half, act, n_out):
    x_ref, w_ref = refs[0], refs[1]
    pos = 2
    res_ref = None
    if has_res:
        res_ref = refs[pos]
        pos += 1
    if rope_every:
        c_ref, su_ref, sd_ref = refs[pos:pos + 3]
        pos += 3
    out_refs = refs[pos:pos + n_out]
    acc_ref = refs[pos + n_out] if nk > 1 else None
    j = pl.program_id(1)
    k = pl.program_id(2)

    part = _dot(x_ref[...], w_ref[...])

    def store(val):
        for o in out_refs:
            o[...] = val.astype(o.dtype)

    def finish(acc):
        if has_res:
            acc = acc + res_ref[...]
        if act == "sigmoid":
            acc = _sigmoid(acc)
        if not rope_every:
            store(acc)
            return
        tn = acc.shape[1]

        def roped():
            c, su, sd = c_ref[...], su_ref[...], sd_ref[...]
            pieces = [_rope_lanes(acc[:, a:a + LANES], c, su, sd, rope_half) for a in range(0, tn, LANES)]
            store(jnp.concatenate(pieces, axis=1))

        if rope_every == 1:
            roped()
        else:
            pl.when(j % rope_every == 0)(roped)
            pl.when(j % rope_every != 0)(lambda: store(acc))

    if nk == 1:
        finish(part)
    else:
        @pl.when(k == 0)
        def _():
            acc_ref[...] = part

        @pl.when(k > 0)
        def _():
            acc_ref[...] += part

        @pl.when(k == nk - 1)
        def _():
            finish(acc_ref[...])


def matmul(x, w, n, *, tm, tn, tk, out_dtypes, res=None, rope=None, rope_every=0, rope_half=0, act=None):
    m, kdim = x.shape
    tm, tn, tk = _tile(m, tm), _tile(n, tn), _tile(kdim, tk)
    nk = kdim // tk
    in_specs = [pl.BlockSpec((tm, tk), lambda i, j, k: (i, k)),
                pl.BlockSpec((tk, tn), lambda i, j, k: (k, j))]
    args = [x, w]
    if res is not None:
        in_specs.append(pl.BlockSpec((tm, tn), lambda i, j, k: (i, j)))
        args.append(res)
    if rope is not None:
        period = rope[0].shape[0]
        nper = period // tm
        for t in rope:
            in_specs.append(pl.BlockSpec((tm, LANES), lambda i, j, k: (i % nper, 0)))
            args.append(t)
    out_specs = [pl.BlockSpec((tm, tn), lambda i, j, k: (i, j)) for _ in out_dtypes]
    out_shape = [jax.ShapeDtypeStruct((m, n), dt) for dt in out_dtypes]
    scratch = [pltpu.VMEM((tm, tn), F32)] if nk > 1 else []
    outs = pl.pallas_call(
        functools.partial(_mm_kernel, nk=nk, has_res=res is not None,
                          rope_every=rope_every if rope is not None else 0, rope_half=rope_half,
                          act=act, n_out=len(out_dtypes)),
        grid=(m // tm, n // tn, nk),
        in_specs=in_specs, out_specs=out_specs, out_shape=out_shape,
        scratch_shapes=scratch,
        compiler_params=_params(("parallel", "parallel", "arbitrary")),
        name="matmul",
    )(*args)
    return outs if len(outs) > 1 else outs[0]


def _ffn_up_kernel(*refs, tiles_per_seq, seg, carry_mode):
    if carry_mode:
        x_ref, wv_ref, wg_ref, cw_ref, cb_ref, st_ref, a_ref, tail_ref, carry_ref = refs
    else:
        x_ref, wv_ref, wg_ref, cw_ref, cb_ref, f1_ref, f2_ref, a_ref, tail_ref = refs
    i = pl.program_id(1)
    x = x_ref[...]
    val = _dot(x, wv_ref[...])
    gate = _dot(x, wg_ref[...])
    tm = gate.shape[0]
    row = lax.broadcasted_iota(jnp.int32, (tm, 1), 0)
    g1 = pltpu.roll(gate, 1, 0)
    g2 = pltpu.roll(gate, 2, 0)
    if carry_mode:
        @pl.when(i % tiles_per_seq == 0)
        def _():
            carry_ref[...] = st_ref[0]

        c0 = carry_ref[0:1, :]
        c1 = carry_ref[1:2, :]
        g1 = jnp.where(row == 0, c1, g1)
        g2 = jnp.where(row == 0, c0, jnp.where(row == 1, c1, g2))
        carry_ref[...] = gate[tm - 2:tm, :]
    else:
        t = row % seg
        g1 = jnp.where(t == 0, f1_ref[...], g1)
        g2 = jnp.where(t < 2, f2_ref[...], g2)
    conv = cb_ref[...] + cw_ref[0:1, :] * g2 + cw_ref[1:2, :] * g1 + cw_ref[2:3, :] * gate
    a_ref[...] = (val * (conv * _sigmoid(conv))).astype(a_ref.dtype)
    nt = tail_ref.shape[1]
    tail_ref[0] = gate[tm - nt:tm, :]


def ffn_up(h, w_up, d_ff, conv_w, conv_b, *, tm, state=None, fill=None, seg=None):
    m, kdim = h.shape
    tn = _tile(d_ff, 256)
    nj = d_ff // tn
    tm = _tile(m, tm)
    carry_mode = state is not None
    in_specs = [pl.BlockSpec((tm, kdim), lambda j, i: (i, 0)),
                pl.BlockSpec((kdim, tn), lambda j, i: (0, j)),
                pl.BlockSpec((kdim, tn), lambda j, i: (0, nj + j)),
                pl.BlockSpec((3, tn), lambda j, i: (0, j)),
                pl.BlockSpec((1, tn), lambda j, i: (0, j))]
    args = [h, w_up, w_up, conv_w, conv_b.reshape(1, d_ff)]
    if carry_mode:
        nseq = state.shape[0]
        tps = (m // nseq) // tm
        in_specs.append(pl.BlockSpec((1, 2, tn), lambda j, i: (i // tps, 0, j)))
        args.append(state)
        ngrp, nt = nseq, SUBLANES
        tail_map = lambda j, i: (i // tps, 0, j)
        scratch = [pltpu.VMEM((2, tn), F32)]
    else:
        tps = 1
        for f in fill:
            in_specs.append(pl.BlockSpec((tm, tn), lambda j, i: (i, j)))
            args.append(f)
        ngrp, nt = m // tm, tm
        tail_map = lambda j, i: (i, 0, j)
        scratch = []
    a, tail = pl.pallas_call(
        functools.partial(_ffn_up_kernel, tiles_per_seq=tps, seg=seg, carry_mode=carry_mode),
        grid=(nj, m // tm),
        in_specs=in_specs,
        out_specs=[pl.BlockSpec((tm, tn), lambda j, i: (i, j)),
                   pl.BlockSpec((1, nt, tn), tail_map)],
        out_shape=[jax.ShapeDtypeStruct((m, d_ff), BF16),
                   jax.ShapeDtypeStruct((ngrp, nt, d_ff), F32)],
        scratch_shapes=scratch,
        compiler_params=_params(("parallel", "arbitrary")),
        name="ffn_up",
    )(*args)
    return a, tail


def _ret_kernel(lg_ref, q_ref, k_ref, v_ref, g_ref, cos_ref, sin_ref, s0_ref,
                o_ref, s_out_ref, s_scr, intra_scr, *, c_valid, dk):
    h = pl.program_id(1)
    c = pl.program_id(2)
    nc = pl.num_programs(2)
    half = dk // 2
    lg = lg_ref[h]
    cp = q_ref.shape[0]

    @pl.when(c == 0)
    def _():
        s_scr[...] = s0_ref[0, 0]
        di = lax.broadcasted_iota(jnp.int32, (cp, cp), 0)
        dj = lax.broadcasted_iota(jnp.int32, (cp, cp), 1)
        diff = di - dj
        intra_scr[...] = jnp.where((diff >= 0) & (dj < c_valid),
                                   jnp.exp(lg * jnp.maximum(diff, 0).astype(F32)), 0.0)

    cos = cos_ref[...]
    sin = sin_ref[...]

    def rope(x):
        x1 = x[:, :half]
        x2 = x[:, half:]
        return jnp.concatenate([x1 * cos - x2 * sin, x2 * cos + x1 * sin], axis=1)

    q = rope(q_ref[...].astype(F32))
    k = rope(k_ref[...].astype(F32)) * (dk ** -0.5)
    v = v_ref[...]
    ri = lax.broadcasted_iota(jnp.int32, (cp, 1), 0)
    rif = ri.astype(F32)
    read_decay = jnp.exp(lg * (rif + 1.0))
    write_decay = jnp.where(ri < c_valid, jnp.exp(lg * jnp.maximum(c_valid - 1.0 - rif, 0.0)), 0.0)

    s = s_scr[...]
    att = _dot_nt(q.astype(BF16), k.astype(BF16)) * intra_scr[...]
    o = _dot(att.astype(BF16), v) + _dot((q * read_decay).astype(BF16), s.astype(BF16))
    kw = (k * write_decay).astype(BF16)
    chunk_decay = jnp.exp(lg * jnp.full((1, 1), float(c_valid), F32))
    s_new = s * chunk_decay + lax.dot_general(kw, v, (((0,), (0,)), ((), ())), preferred_element_type=F32)
    s_scr[...] = s_new

    @pl.when(c == nc - 1)
    def _():
        s_out_ref[0, 0] = s_new

    on = o * lax.rsqrt(jnp.mean(o * o, axis=-1, keepdims=True) + EPS)
    gate = g_ref[...].astype(F32)
    o_ref[...] = (on * (gate * _sigmoid(gate))).astype(o_ref.dtype)


def retention(proj, s0, cos, sin, *, nb, heads, dk, dv, chunk, c_valid):
    m = proj.shape[0]
    t = m // nb
    nc = t // chunk
    qb = heads
    vb = 2 * heads * dk // dv
    log_g = jnp.log1p(-jnp.exp2(-5.0 - jnp.arange(heads, dtype=F32)))
    grid_spec = pltpu.PrefetchScalarGridSpec(
        num_scalar_prefetch=1,
        grid=(nb, heads, nc),
        in_specs=[
            pl.BlockSpec((chunk, dk), lambda b, h, c, lg: (b * nc + c, h)),
            pl.BlockSpec((chunk, dk), lambda b, h, c, lg: (b * nc + c, qb + h)),
            pl.BlockSpec((chunk, dv), lambda b, h, c, lg: (b * nc + c, vb + h)),
            pl.BlockSpec((chunk, dv), lambda b, h, c, lg: (b * nc + c, vb + heads + h)),
            pl.BlockSpec((chunk, dk // 2), lambda b, h, c, lg: (c, 0)),
            pl.BlockSpec((chunk, dk // 2), lambda b, h, c, lg: (c, 0)),
            pl.BlockSpec((1, 1, dk, dv), lambda b, h, c, lg: (b, h, 0, 0)),
        ],
        out_specs=[
            pl.BlockSpec((chunk, dv), lambda b, h, c, lg: (b * nc + c, h)),
            pl.BlockSpec((1, 1, dk, dv), lambda b, h, c, lg: (b, h, 0, 0)),
        ],
        scratch_shapes=[pltpu.VMEM((dk, dv), F32), pltpu.VMEM((chunk, chunk), F32)],
    )
    o, s = pl.pallas_call(
        functools.partial(_ret_kernel, c_valid=c_valid, dk=dk),
        grid_spec=grid_spec,
        out_shape=[jax.ShapeDtypeStruct((m, heads * dv), BF16),
                   jax.ShapeDtypeStruct((nb, heads, dk, dv), F32)],
        compiler_params=_params(("parallel", "parallel", "arbitrary")),
        name="retention",
    )(log_g, proj, proj, proj, proj, cos, sin, s0)
    return o, s


def _cmp_project(planes, w_ref, p_ref, kvh, hd):
    n_seg = planes.shape[1] // CMP_STRIDE
    for br in range(2):
        acc = jnp.zeros((kvh * n_seg, 2 * hd), F32)
        for sp in range(CMP_STRIDE // 2):
            pieces = []
            for g in range(kvh):
                plane = planes.at[br * kvh + g]
                a = plane[pl.ds(2 * sp, n_seg, stride=CMP_STRIDE), :]
                b = plane[pl.ds(2 * sp + 1, n_seg, stride=CMP_STRIDE), :]
                pieces.append(jnp.concatenate([a, b], axis=1))
            lhs = jnp.concatenate(pieces, axis=0).astype(BF16)
            acc = acc + _dot(lhs, w_ref[br, sp])
        for g in range(kvh):
            p_ref[0, br * kvh + g] = acc[g * n_seg:(g + 1) * n_seg, :]


def _cmp_proj_rows_kernel(x_ref, w_ref, p_ref, r_ref, *, kvh, hd):
    for hh in range(2 * kvh):
        r_ref[hh] = x_ref[:, hh * hd:(hh + 1) * hd]
    _cmp_project(r_ref, w_ref, p_ref, kvh, hd)


def _cmp_proj_pages_kernel(pt_ref, cache_ref, w_ref, p_ref, r_ref, sem, *, kvh, hd, pages_per_step, n_pages):
    b = pl.program_id(0)
    c = pl.program_id(1)
    nchunk = pl.num_programs(1)
    step = b * nchunk + c
    nsteps = pl.num_programs(0) * nchunk
    page = cache_ref.shape[1]

    def copies(st, slot):
        bb = st // nchunk
        cc = st % nchunk
        out = []
        for u in range(pages_per_step):
            phys = pt_ref[bb * n_pages + cc * pages_per_step + u]
            for hh in range(2 * kvh):
                out.append(pltpu.make_async_copy(
                    cache_ref.at[phys, :, hh // kvh, hh % kvh, :],
                    r_ref.at[slot, hh, pl.ds(u * page, page), :],
                    sem.at[slot]))
        return out

    slot = step % 2

    @pl.when(step == 0)
    def _():
        for cp in copies(step, slot):
            cp.start()

    @pl.when(step + 1 < nsteps)
    def _():
        for cp in copies(step + 1, 1 - slot):
            cp.start()

    for cp in copies(step, slot):
        cp.wait()
    _cmp_project(r_ref.at[slot], w_ref, p_ref, kvh, hd)


def _cmp_w1_pairs(cmp_w1, hd):
    r = CMP_LEN // CMP_STRIDE
    e = cmp_w1.shape[-1]
    w = cmp_w1.reshape(2, r, CMP_STRIDE // 2, 2, hd, e)
    w = w.transpose(0, 2, 3, 4, 1, 5)
    return w.reshape(2, CMP_STRIDE // 2, 2 * hd, r * e).astype(BF16)


def cmp_proj_rows(rows2d, nb, t, w_pairs, kvh, hd):
    n_seg = t // CMP_STRIDE
    width = 2 * kvh * hd
    return pl.pallas_call(
        functools.partial(_cmp_proj_rows_kernel, kvh=kvh, hd=hd),
        grid=(nb,),
        in_specs=[pl.BlockSpec((t, width), lambda b: (b, 0)),
                  pl.BlockSpec(w_pairs.shape, lambda b: (0, 0, 0, 0))],
        out_specs=pl.BlockSpec((1, 2 * kvh, n_seg, 2 * hd), lambda b: (b, 0, 0, 0)),
        out_shape=jax.ShapeDtypeStruct((nb, 2 * kvh, n_seg, 2 * hd), F32),
        scratch_shapes=[pltpu.VMEM((2 * kvh, t, hd), F32)],
        compiler_params=_params(("parallel",)),
        name="cmp_proj_prompt",
    )(rows2d, w_pairs)


def cmp_proj_pages(cache, page_table, w_pairs, kvh, hd, pages_per_step):
    nb, n_pages = page_table.shape
    page = cache.shape[1]
    n_seg = pages_per_step * page // CMP_STRIDE
    nchunk = n_pages // pages_per_step
    grid_spec = pltpu.PrefetchScalarGridSpec(
        num_scalar_prefetch=1,
        grid=(nb, nchunk),
        in_specs=[pl.BlockSpec(memory_space=pl.ANY),
                  pl.BlockSpec(w_pairs.shape, lambda b, c, pt: (0, 0, 0, 0))],
        out_specs=pl.BlockSpec((1, 2 * kvh, n_seg, 2 * hd), lambda b, c, pt: (b, 0, c, 0)),
        scratch_shapes=[pltpu.VMEM((2, 2 * kvh, pages_per_step * page, hd), F32),
                        pltpu.SemaphoreType.DMA((2,))],
    )
    return pl.pallas_call(
        functools.partial(_cmp_proj_pages_kernel, kvh=kvh, hd=hd, pages_per_step=pages_per_step,
                          n_pages=n_pages),
        grid_spec=grid_spec,
        out_shape=jax.ShapeDtypeStruct((nb, 2 * kvh, nchunk * n_seg, 2 * hd), F32),
        compiler_params=_params(("arbitrary", "arbitrary")),
        name="cmp_proj_sample",
    )(page_table.reshape(-1), cache, w_pairs)


def _cmp_pe_kernel(pe_ref, w_ref, o_ref):
    o_ref[0] = _dot(pe_ref[0], w_ref[0])


def cmp_pe_term(cmp_pe, cmp_w1):
    _, n, hd = cmp_pe.shape
    e = cmp_w1.shape[-1]
    pe_flat = jnp.zeros((2, SUBLANES, n * hd), BF16).at[:, 0].set(cmp_pe.reshape(2, n * hd).astype(BF16))
    w_flat = cmp_w1.reshape(2, n * hd, e).astype(BF16)
    return pl.pallas_call(
        _cmp_pe_kernel,
        grid=(2,),
        in_specs=[pl.BlockSpec((1, SUBLANES, n * hd), lambda b: (b, 0, 0)),
                  pl.BlockSpec((1, n * hd, e), lambda b: (b, 0, 0))],
        out_specs=pl.BlockSpec((1, SUBLANES, e), lambda b: (b, 0, 0)),
        out_shape=jax.ShapeDtypeStruct((2, SUBLANES, e), F32),
        compiler_params=_params(("parallel",)),
        name="cmp_pe",
    )(pe_flat, w_flat)


def _cmp_finish_kernel(p_ref, pe_ref, w2_ref, o_ref):
    p = p_ref[0, 0]
    n_seg = p.shape[0]
    e = p.shape[1] // 2
    hid = p[:, :e] + pltpu.roll(p[:, e:], n_seg - 1, 0) + pe_ref[0, 0:1, :]
    c = math.sqrt(2.0 / math.pi)
    act = 0.5 * hid * (1.0 + jnp.tanh(c * (hid + 0.044715 * (hid * hid * hid))))
    o_ref[0, 0] = _dot(act.astype(BF16), w2_ref[0]).astype(o_ref.dtype)


def cmp_finish(p, pe_term, cmp_w2, kvh):
    nb, nu, n_seg, e2 = p.shape
    e = e2 // 2
    hd = cmp_w2.shape[-1]
    return pl.pallas_call(
        _cmp_finish_kernel,
        grid=(nb, nu),
        in_specs=[pl.BlockSpec((1, 1, n_seg, e2), lambda b, u: (b, u, 0, 0)),
                  pl.BlockSpec((1, SUBLANES, e), lambda b, u: (u // kvh, 0, 0)),
                  pl.BlockSpec((1, e, hd), lambda b, u: (u // kvh, 0, 0))],
        out_specs=pl.BlockSpec((1, 1, n_seg, hd), lambda b, u: (b, u, 0, 0)),
        out_shape=jax.ShapeDtypeStruct((nb, nu, n_seg, hd), BF16),
        compiler_params=_params(("parallel", "parallel")),
        name="cmp_finish",
    )(p, pe_term, cmp_w2.astype(BF16))


def _cmp_attn_kernel(q_ref, kc_ref, vc_ref, agg_ref, o_ref, imp_ref, *, group, hd, n_c, pos0):
    qi = pl.program_id(2)
    tq = q_ref.shape[0]
    kc = kc_ref[0, 0]
    vc = vc_ref[0, 0]
    agg = agg_ref[...]
    n_cp = kc.shape[0]
    qs = jnp.concatenate([q_ref[:, h * hd:(h + 1) * hd] for h in range(group)], axis=0)
    qpos1 = pos0 + qi * tq + lax.broadcasted_iota(jnp.int32, (tq, 1), 0)
    qpos = jnp.concatenate([qpos1] * group, axis=0)
    ci = lax.broadcasted_iota(jnp.int32, (1, n_cp), 1)
    cm = ((ci * CMP_STRIDE + (CMP_LEN - 1)) <= qpos) & (ci < n_c)
    s = jnp.where(cm, _dot_nt(qs, kc) * (hd ** -0.5), NEG)
    e = jnp.where(cm, jnp.exp(s - jnp.max(s, axis=-1, keepdims=True)), 0.0)
    p = e / jnp.maximum(jnp.sum(e, axis=-1, keepdims=True), TINY)
    pb = p.astype(BF16)
    o = _dot(pb, vc)
    psum = jnp.zeros((tq, n_cp), F32)
    for h in range(group):
        o_ref[:, h * hd:(h + 1) * hd] = o[h * tq:(h + 1) * tq, :].astype(o_ref.dtype)
        psum = psum + pb[h * tq:(h + 1) * tq, :].astype(F32)
    hi = psum.astype(BF16)
    lo = (psum - hi.astype(F32)).astype(BF16)
    imp_ref[0, 0] = _dot_nt(agg, hi) + _dot_nt(agg, lo)


def _slc_aggregation_t(n_cp, n_slc, rows):
    rs = SLC_BLOCK // CMP_STRIDE
    rc = CMP_LEN // CMP_STRIDE
    i = jnp.arange(n_cp)[None, :]
    j = jnp.arange(rows)[:, None]
    w = sum((i == j * rs + m - n).astype(F32) for m in range(rs) for n in range(rc))
    return jnp.where(j < n_slc, w, 0.0).astype(BF16)


def cmp_attn(q, kvc, *, nb, kvh, group, hd, n_c, n_slc, pos0, tq):
    m = q.shape[0]
    t = m // nb
    tq = _tile(t, tq)
    nq = t // tq
    n_cp = kvc.shape[2]
    rows = -(-n_slc // SUBLANES) * SUBLANES
    agg_t = _slc_aggregation_t(n_cp, n_slc, rows)
    gw = group * hd
    return pl.pallas_call(
        functools.partial(_cmp_attn_kernel, group=group, hd=hd, n_c=n_c, pos0=pos0),
        grid=(nb, kvh, nq),
        in_specs=[pl.BlockSpec((tq, gw), lambda b, g, i: (b * nq + i, g)),
                  pl.BlockSpec((1, 1, n_cp, hd), lambda b, g, i: (b, g, 0, 0)),
                  pl.BlockSpec((1, 1, n_cp, hd), lambda b, g, i: (b, kvh + g, 0, 0)),
                  pl.BlockSpec((rows, n_cp), lambda b, g, i: (0, 0))],
        out_specs=[pl.BlockSpec((tq, gw), lambda b, g, i: (b * nq + i, g)),
                   pl.BlockSpec((1, 1, rows, tq), lambda b, g, i: (b, g, 0, i))],
        out_shape=[jax.ShapeDtypeStruct((m, kvh * gw), BF16),
                   jax.ShapeDtypeStruct((nb, kvh, rows, t), F32)],
        compiler_params=_params(("parallel", "parallel", "parallel")),
        name="cmp_attn",
    )(q, kvc, kvc, agg_t)


def _select_kernel(imp_ref, *rest, n_slc, pos0, t_len, emit_idx):
    if emit_idx:
        idx_ref, ok_ref, sc_ref = rest
    else:
        bias_ref, sc_ref = rest
    rows, width = imp_ref.shape
    lane = pl.program_id(0) * width + lax.broadcasted_iota(jnp.int32, (1, width), 1)
    qpos = pos0 + jnp.bitwise_and(lane, t_len - 1)
    blk = lax.broadcasted_iota(jnp.int32, (rows, width), 0)
    cur = jnp.right_shift(qpos, SLC_BLOCK.bit_length() - 1)
    visible = blk * SLC_BLOCK <= qpos
    forced = (blk == 0) | (blk == cur) | (blk == cur - 1)
    score = jnp.where(visible, jnp.where(forced, FORCE, imp_ref[...]), NEG)
    sc_ref[...] = score

    def count(i, cnt):
        si = sc_ref[pl.ds(i, 1), :]
        beats = (si > score) | ((si == score) & (i < blk))
        return cnt + jnp.where(beats, 1, 0)

    rank = lax.fori_loop(0, n_slc, count, jnp.zeros((rows, width), jnp.int32))
    in_range = blk < n_slc
    if emit_idx:
        blk_f = blk.astype(F32)
        for r in range(idx_ref.shape[0]):
            hit = (rank == r) & in_range
            idx_r = jnp.sum(jnp.where(hit, blk_f, 0.0), axis=0, keepdims=True)
            idx_ref[r:r + 1, :] = idx_r.astype(jnp.int32)
            ok_ref[r:r + 1, :] = jnp.sum(jnp.where(hit & visible, 1.0, 0.0), axis=0, keepdims=True)
    else:
        chosen = (rank < N_SELECT) & visible & in_range
        bias_ref[...] = jnp.where(chosen, 0.0, NEG).astype(bias_ref.dtype)


def select_blocks(imp, *, n_slc, pos0, emit_idx):
    nb, kvh, rows, t = imp.shape
    assert t & (t - 1) == 0
    lanes = nb * kvh * t
    imp2 = imp.transpose(2, 0, 1, 3).reshape(rows, lanes)
    width = _tile(lanes, 2048)
    spec = pl.BlockSpec((rows, width), lambda i: (0, i))
    if emit_idx:
        nsel = min(N_SELECT, n_slc)
        o_spec = pl.BlockSpec((nsel, width), lambda i: (0, i))
        out_specs = [o_spec, o_spec]
        out_shape = [jax.ShapeDtypeStruct((nsel, lanes), jnp.int32), jax.ShapeDtypeStruct((nsel, lanes), F32)]
    else:
        out_specs = [spec]
        out_shape = [jax.ShapeDtypeStruct((rows, lanes), BF16)]
    outs = pl.pallas_call(
        functools.partial(_select_kernel, n_slc=n_slc, pos0=pos0, t_len=t, emit_idx=emit_idx),
        grid=(lanes // width,),
        in_specs=[spec], out_specs=out_specs, out_shape=out_shape,
        scratch_shapes=[pltpu.VMEM((rows, width), F32)],
        compiler_params=_params(("parallel",)),
        name="select_blocks",
    )(imp2)
    if emit_idx:
        return tuple(o.reshape(-1, nb, kvh, t).transpose(1, 2, 0, 3) for o in outs)
    return outs[0].reshape(rows, nb, kvh, t).transpose(1, 2, 3, 0)


V_ROWS_PAD = 2 * SUBLANES


def _prompt_attn_kernel(*refs, mode, nq, hd):
    if mode == "slc":
        q_ref, bias_ref, k_ref, vt_ref, o_ref = refs
    else:
        q_ref, k_ref, vt_ref, o_ref = refs
    qi = pl.program_id(2)
    tq = q_ref.shape[0]
    n_heads = q_ref.shape[1] // hd
    kl = lax.broadcasted_iota(jnp.int32, (tq, tq), 0)
    ql = lax.broadcasted_iota(jnp.int32, (tq, tq), 1)

    def run_head(hp, k0, nkeys, band):
        q = (q_ref[:, hp * hd:(hp + 1) * hd].astype(F32) * (hd ** -0.5 * math.log2(math.e))).astype(BF16)
        if mode == "slc":
            q = jnp.concatenate([q, bias_ref[0, 0]], axis=1)
        st = _dot_nt(k_ref[pl.ds(k0, nkeys), :], q)
        parts = []
        if nkeys > tq:
            top = st[:nkeys - tq, :]
            if band:
                top = jnp.where(kl > ql, top, NEG)
            parts.append(top)
        parts.append(jnp.where(kl <= ql, st[nkeys - tq:, :], NEG))
        st = jnp.concatenate(parts, axis=0) if len(parts) > 1 else parts[0]
        p = jnp.exp2(st - jnp.max(st, axis=0, keepdims=True)).astype(BF16)
        ot = _dot(vt_ref[0, 0, :, pl.ds(k0, nkeys)], p)
        o_ref[:, hp * hd:(hp + 1) * hd] = (ot[:hd, :] / ot[hd:hd + 1, :]).T.astype(o_ref.dtype)

    def run(k0, nkeys, band):
        for hp in range(n_heads):
            run_head(hp, k0, nkeys, band)

    if mode == "slc":
        for v in range(nq):
            pl.when(qi == v)(functools.partial(run, 0, (v + 1) * tq, False))
    else:
        pl.when(qi == 0)(functools.partial(run, 0, tq, False))
        if nq > 1:
            pl.when(qi > 0)(lambda: run(pl.multiple_of((qi - 1) * tq, tq), 2 * tq, True))


def prompt_attn(q, k_rows, vt, bias, *, mode, nb, kvh, group, hd, tq, heads_per_step=2):
    m = q.shape[0]
    t = m // nb
    tq = _tile(t, tq)
    nq = t // tq
    kw = k_rows.shape[1] // kvh
    assert mode == "slc" or tq == WINDOW or nq == 1
    hw = heads_per_step * hd
    hsteps = group // heads_per_step
    in_specs = [pl.BlockSpec((tq, hw), lambda b, g, i, h: (b * nq + i, g * hsteps + h))]
    args = [q]
    if mode == "slc":
        in_specs.append(pl.BlockSpec((1, 1, tq, hd), lambda b, g, i, h: (b, g, i, 0)))
        args.append(bias)
    in_specs += [pl.BlockSpec((t, kw), lambda b, g, i, h: (b, g)),
                 pl.BlockSpec((1, 1, hd + V_ROWS_PAD, t), lambda b, g, i, h: (b, g, 0, 0))]
    args += [k_rows, vt]
    return pl.pallas_call(
        functools.partial(_prompt_attn_kernel, mode=mode, nq=nq, hd=hd),
        grid=(nb, kvh, nq, hsteps),
        in_specs=in_specs,
        out_specs=pl.BlockSpec((tq, hw), lambda b, g, i, h: (b * nq + i, g * hsteps + h)),
        out_shape=jax.ShapeDtypeStruct((m, kvh * group * hd), BF16),
        compiler_params=_params(("parallel", "parallel", "parallel", "parallel")),
        name="prompt_attn_" + mode,
    )(*args)


def _sample_attn_kernel(idx_ref, pt_ref, q_ref, kpos_ref, okx_ref, kn_ref, vn_ref, wk_ref, wv_ref,
                        wkn_ref, wvn_ref, cache_ref, oslc_ref, owin_ref, kbuf, vbuf, sem,
                        *, kvh, group, hd, t_valid, past, nsel, ks_sec, vs_sec):
    b = pl.program_id(0)
    g = pl.program_id(1)
    tp = q_ref.shape[2] // group
    n_blocks = past // SLC_BLOCK
    per_page = PAGE_SIZE // SLC_BLOCK
    n_fetch = t_valid * nsel

    def copies(f):
        t = f // nsel
        r = f % nsel
        blk = idx_ref[((b * kvh + g) * nsel + r) * tp + t]
        blk = jnp.minimum(blk, n_blocks - 1)
        page = pt_ref[b * (past // PAGE_SIZE) + blk // per_page]
        row0 = (blk % per_page) * SLC_BLOCK
        dst = pl.ds(f * SLC_BLOCK, SLC_BLOCK)
        ck = pltpu.make_async_copy(cache_ref.at[page, pl.ds(row0, SLC_BLOCK), ks_sec, g, :],
                                   kbuf.at[dst, :], sem.at[0])
        cv = pltpu.make_async_copy(cache_ref.at[page, pl.ds(row0, SLC_BLOCK), vs_sec, g, :],
                                   vbuf.at[dst, :], sem.at[1])
        return ck, cv

    def start(f, c):
        ck, cv = copies(f)
        ck.start()
        cv.start()
        return c

    def wait(f, c):
        ck, cv = copies(f)
        ck.wait()
        cv.wait()
        return c

    lax.fori_loop(0, n_fetch, start, 0)

    scale = hd ** -0.5
    qa = q_ref[0, 0].astype(BF16)
    trow = jnp.right_shift(lax.broadcasted_iota(jnp.int32, (tp * group, 1), 0), group.bit_length() - 1)
    wk = wk_ref[0].astype(BF16)
    wrows = wk.shape[0]
    s_c = _dot_nt(qa, wk) * scale
    s_n = _dot_nt(qa, wkn_ref[...].astype(BF16)) * scale
    d_c = (past + trow) - (past - wrows + lax.broadcasted_iota(jnp.int32, (1, wrows), 1))
    jn = lax.broadcasted_iota(jnp.int32, (1, tp), 1)
    d_n = trow - jn
    m_c = (d_c >= 0) & (d_c < WINDOW)
    m_n = (d_n >= 0) & (d_n < WINDOW) & (jn < t_valid)
    s_c = jnp.where(m_c, s_c, NEG)
    s_n = jnp.where(m_n, s_n, NEG)
    mx = jnp.maximum(jnp.max(s_c, axis=-1, keepdims=True), jnp.max(s_n, axis=-1, keepdims=True))
    e_c = jnp.where(m_c, jnp.exp(s_c - mx), 0.0)
    e_n = jnp.where(m_n, jnp.exp(s_n - mx), 0.0)
    den = jnp.maximum(jnp.sum(e_c, axis=-1, keepdims=True) + jnp.sum(e_n, axis=-1, keepdims=True), TINY)
    o_w = _dot((e_c / den).astype(BF16), wv_ref[0].astype(BF16)) + \
        _dot((e_n / den).astype(BF16), wvn_ref[...].astype(BF16))
    owin_ref[0, 0] = o_w.astype(owin_ref.dtype)

    lax.fori_loop(0, n_fetch, wait, 0)

    kn = kn_ref[...].astype(BF16)
    vn = vn_ref[...].astype(BF16)
    span = nsel * SLC_BLOCK
    oslc_ref[...] = jnp.zeros(oslc_ref.shape, oslc_ref.dtype)
    for t in range(t_valid):
        qt = q_ref[0, 0, t * group:(t + 1) * group, :].astype(BF16)
        kt = kbuf[t * span:(t + 1) * span, :].astype(BF16)
        vt = vbuf[t * span:(t + 1) * span, :].astype(BF16)
        kp = kpos_ref[0, 0, t:t + 1, :]
        okv = okx_ref[0, 0, t:t + 1, :] > 0.5
        qpos = past + t
        m_g = (kp <= qpos) & okv & (kp < past)
        s_g = jnp.where(m_g, _dot_nt(qt, kt) * scale, NEG)
        selrow = jnp.zeros((1, tp), F32)
        for j in range(t_valid):
            hit = jnp.max(jnp.where((kp == past + j) & okv, 1.0, 0.0), axis=-1, keepdims=True)
            selrow = selrow + jnp.where(jn == j, hit, 0.0)
        m_w = (selrow > 0.5) & (jn <= t)
        s_w = jnp.where(m_w, _dot_nt(qt, kn) * scale, NEG)
        mx = jnp.maximum(jnp.max(s_g, axis=-1, keepdims=True), jnp.max(s_w, axis=-1, keepdims=True))
        e_g = jnp.where(m_g, jnp.exp(s_g - mx), 0.0)
        e_w = jnp.where(m_w, jnp.exp(s_w - mx), 0.0)
        den = jnp.sum(e_g, axis=-1, keepdims=True) + jnp.sum(e_w, axis=-1, keepdims=True)
        o_t = _dot((e_g / den).astype(BF16), vt) + _dot((e_w / den).astype(BF16), vn)
        oslc_ref[0, 0, t * group:(t + 1) * group, :] = o_t.astype(oslc_ref.dtype)


def sample_attn(q4, idx, okf, kv_new, cache_kv4, cache_win3, page_table, *, kvh, group, hd, t_valid, past):
    nb = q4.shape[0]
    tp = q4.shape[2] // group
    nsel = idx.shape[2]
    span = nsel * SLC_BLOCK
    wrows = cache_win3.shape[1]
    idx_t = idx.transpose(0, 1, 3, 2)
    kpos = (idx_t[..., None] * SLC_BLOCK + jnp.arange(SLC_BLOCK, dtype=jnp.int32)).reshape(nb, kvh, tp, span)
    okx = jnp.broadcast_to(okf.transpose(0, 1, 3, 2)[..., None], (nb, kvh, tp, nsel, SLC_BLOCK)).reshape(
        nb, kvh, tp, span)
    grid_spec = pltpu.PrefetchScalarGridSpec(
        num_scalar_prefetch=2,
        grid=(nb, kvh),
        in_specs=[
            pl.BlockSpec((1, 1, tp * group, hd), lambda b, g, *_: (b, g, 0, 0)),
            pl.BlockSpec((1, 1, tp, span), lambda b, g, *_: (b, g, 0, 0)),
            pl.BlockSpec((1, 1, tp, span), lambda b, g, *_: (b, g, 0, 0)),
            pl.BlockSpec((tp, hd), lambda b, g, *_: (b, 2 * kvh + g)),
            pl.BlockSpec((tp, hd), lambda b, g, *_: (b, 3 * kvh + g)),
            pl.BlockSpec((1, wrows, hd), lambda b, g, *_: (b, 0, g)),
            pl.BlockSpec((1, wrows, hd), lambda b, g, *_: (b, 0, kvh + g)),
            pl.BlockSpec((tp, hd), lambda b, g, *_: (b, 4 * kvh + g)),
            pl.BlockSpec((tp, hd), lambda b, g, *_: (b, 5 * kvh + g)),
            pl.BlockSpec(memory_space=pl.ANY),
        ],
        out_specs=[pl.BlockSpec((1, 1, tp * group, hd), lambda b, g, *_: (b, g, 0, 0)),
                   pl.BlockSpec((1, 1, tp * group, hd), lambda b, g, *_: (b, g, 0, 0))],
        scratch_shapes=[pltpu.VMEM((t_valid * span, hd), F32),
                        pltpu.VMEM((t_valid * span, hd), F32),
                        pltpu.SemaphoreType.DMA((2,))],
    )
    return pl.pallas_call(
        functools.partial(_sample_attn_kernel, kvh=kvh, group=group, hd=hd, t_valid=t_valid, past=past,
                          nsel=nsel, ks_sec=2, vs_sec=3),
        grid_spec=grid_spec,
        out_shape=[jax.ShapeDtypeStruct(q4.shape, F32), jax.ShapeDtypeStruct(q4.shape, F32)],
        compiler_params=_params(("arbitrary", "arbitrary")),
        name="sample_attn",
    )(idx.reshape(-1), page_table.reshape(-1), q4, kpos, okx, kv_new, kv_new, cache_win3, cache_win3,
      kv_new, kv_new, cache_kv4)


def _combine_kernel(oc_ref, os_ref, ow_ref, g_ref, o_ref, *, heads, hd):
    gates = g_ref[...]
    for h in range(heads):
        sl = slice(h * hd, (h + 1) * hd)
        o = (gates[:, h:h + 1] * oc_ref[:, sl].astype(F32)
             + gates[:, heads + h:heads + h + 1] * os_ref[:, sl].astype(F32)
             + gates[:, 2 * heads + h:2 * heads + h + 1] * ow_ref[:, sl].astype(F32))
        o_ref[:, sl] = o.astype(o_ref.dtype)


def combine(o_cmp, o_slc, o_win, gates, heads, hd):
    m, d = o_cmp.shape
    tm = _tile(m, 256)
    spec = pl.BlockSpec((tm, d), lambda i: (i, 0))
    return pl.pallas_call(
        functools.partial(_combine_kernel, heads=heads, hd=hd),
        grid=(m // tm,),
        in_specs=[spec, spec, spec, pl.BlockSpec((tm, 3 * heads), lambda i: (i, 0))],
        out_specs=spec,
        out_shape=jax.ShapeDtypeStruct((m, d), BF16),
        compiler_params=_params(("parallel",)),
        name="combine",
    )(o_cmp, o_slc, o_win, gates)


def _rope_tables_half(pos, half, theta):
    inv = jnp.power(theta, -jnp.arange(half, dtype=F32) / half)
    ang = pos.astype(F32)[:, None] * inv[None, :]
    return jnp.cos(ang), jnp.sin(ang)


def _nsa_rope_tables(pos, hd):
    half = hd // 8
    cos, sin = _rope_tables_half(pos, half, ROPE_THETA)
    n = pos.shape[0]
    ones = jnp.ones((n, hd - 2 * half), F32)
    zeros = jnp.zeros((n, hd - 2 * half), F32)
    zh = jnp.zeros((n, half), F32)
    c = jnp.concatenate([cos, cos, ones], axis=1)
    s_up = jnp.concatenate([-sin, zh, zeros], axis=1)
    s_dn = jnp.concatenate([zh, sin, zeros], axis=1)
    return c, s_up, s_dn


def _trunk(x2d, w, *, nb, t_rows, t_valid, pos0, ret_s0, conv_state, sample_ctx):
    m, d = x2d.shape
    is_sample = sample_ctx is not None
    tm = 1024 if not is_sample else m
    d_ff = w["ffn_conv_w"].shape[-1]
    heads_r = RET_HEADS
    dk = d // heads_r
    dv = 2 * dk
    hd = d // NSA_HEADS
    kvh = NSA_KV_HEADS
    group = NSA_HEADS // kvh
    qw = NSA_HEADS * hd
    pos = pos0 + jnp.arange(t_rows)

    def ffn(x, layer):
        h = rmsnorm(x, w["norm_ffn"][layer], BF16)
        if is_sample:
            st = conv_state[layer]
            f1 = jnp.zeros((nb, t_rows, d_ff), F32).at[:, 0].set(st[:, 1])
            f2 = jnp.zeros((nb, t_rows, d_ff), F32).at[:, 0].set(st[:, 0]).at[:, 1].set(st[:, 1])
            a, tail = ffn_up(h, w["ffn_w_up16"][layer], d_ff, w["ffn_conv_w"][layer], w["ffn_conv_b"][layer],
                             tm=tm, fill=(f1.reshape(m, d_ff), f2.reshape(m, d_ff)), seg=t_rows)
            cs = tail.reshape(nb, t_rows, d_ff)[:, t_valid - 2:t_valid]
        else:
            a, tail = ffn_up(h, w["ffn_w_up16"][layer], d_ff, w["ffn_conv_w"][layer], w["ffn_conv_b"][layer],
                             tm=tm, state=conv_state[layer])
            cs = tail[:, SUBLANES - 2:]
        y = matmul(a, w["ffn_w_down16"][layer], d, tm=tm, tn=512, tk=d_ff // 2, out_dtypes=[F32], res=x)
        return y, cs

    h = rmsnorm(x2d, w["norm_mix"][0], BF16)
    proj = matmul(h, w["ret_w_in16"], w["ret_w_in16"].shape[-1], tm=tm, tn=1024, tk=d, out_dtypes=[BF16])
    chunk = math.gcd(t_valid, RET_CHUNK)
    cpad = t_rows if is_sample else chunk
    cos_r, sin_r = _rope_tables_half(pos, dk // 2, RET_THETA)
    o_ret, s_ret = retention(proj, ret_s0, cos_r, sin_r, nb=nb, heads=heads_r, dk=dk, dv=dv,
                             chunk=cpad, c_valid=chunk)
    x1 = matmul(o_ret, w["ret_w_out16"], d, tm=tm, tn=1024, tk=2048, out_dtypes=[F32], res=x2d)
    x2, conv0 = ffn(x1, 0)

    hk = rmsnorm(x2, w["kv_norm"], BF16)
    rope_t = _nsa_rope_tables(pos, hd)
    if is_sample:
        rope_t = tuple(jnp.tile(tb, (nb, 1)) for tb in rope_t)
    kv32, kv16 = matmul(hk, w["kv_w16"], 6 * kvh * hd, tm=tm, tn=kvh * hd, tk=d, out_dtypes=[F32, BF16],
                        rope=rope_t, rope_every=2, rope_half=hd // 8)
    kv_rows = kv32[:, :4 * kvh * hd].reshape(nb, t_rows, 4, kvh, hd)[:, :t_valid]
    win_new = kv32[:, 4 * kvh * hd:].reshape(nb, t_rows, 2, kvh, hd)[:, :t_valid]

    h = rmsnorm(x2, w["norm_mix"][1], BF16)
    q = matmul(h, w["nsa_w_in16"], qw, tm=tm, tn=1024, tk=d, out_dtypes=[BF16], rope=rope_t, rope_every=1,
               rope_half=hd // 8)
    gates = matmul(h, w["nsa_w_gate16"], 3 * NSA_HEADS, tm=tm, tn=3 * NSA_HEADS, tk=d, out_dtypes=[F32],
                   act="sigmoid")
    w_pairs = _cmp_w1_pairs(w["cmp_w1"], hd)
    pe_term = cmp_pe_term(w["cmp_pe"], w["cmp_w1"])
    if not is_sample:
        p = cmp_proj_rows(kv32, nb, t_rows, w_pairs, kvh, hd)
        kvc = cmp_finish(p, pe_term, w["cmp_w2"], kvh)
        n_seg = t_rows // CMP_STRIDE
        n_c = n_seg - CMP_LEN // CMP_STRIDE + 1
        n_slc = -(-t_rows // SLC_BLOCK)
        o_cmp, imp = cmp_attn(q, kvc, nb=nb, kvh=kvh, group=group, hd=hd, n_c=n_c, n_slc=n_slc,
                              pos0=pos0, tq=512)
        bias = select_blocks(imp, n_slc=n_slc, pos0=pos0, emit_idx=False)
        bias = jnp.pad(bias, ((0, 0), (0, 0), (0, 0), (0, hd - bias.shape[-1])))
        kv5 = kv16.reshape(nb, t_rows, 6, kvh, hd)
        onehot = (jnp.arange(t_rows)[:, None] // SLC_BLOCK == jnp.arange(hd)[None, :]).astype(BF16)
        k_slc = jnp.concatenate([kv5[:, :, 2], jnp.broadcast_to(onehot[None, :, None, :], (nb, t_rows, kvh, hd))],
                                axis=-1).reshape(m, kvh * 2 * hd)
        k_win = kv5[:, :, 4].reshape(m, kvh * hd)
        ones_rows = jnp.zeros((nb, kvh, V_ROWS_PAD, t_rows), BF16).at[:, :, 0].set(1.0)

        def v_t(sec):
            return jnp.concatenate([kv5[:, :, sec].transpose(0, 2, 3, 1), ones_rows], axis=2)

        o_slc = prompt_attn(q, k_slc, v_t(3), bias, mode="slc", nb=nb, kvh=kvh, group=group, hd=hd, tq=WINDOW)
        o_win = prompt_attn(q, k_win, v_t(5), None, mode="win", nb=nb, kvh=kvh, group=group, hd=hd, tq=WINDOW)
        win = win_new[:, t_valid - min(WINDOW, t_valid):]
    else:
        cache_kv, cache_win, page_table = sample_ctx
        past = pos0
        p = cmp_proj_pages(cache_kv, page_table, w_pairs, kvh, hd, pages_per_step=min(16, page_table.shape[1]))
        kvc = cmp_finish(p, pe_term, w["cmp_w2"], kvh)
        n_seg = (past + t_valid) // CMP_STRIDE
        n_c = n_seg - CMP_LEN // CMP_STRIDE + 1
        n_slc = -(-(past + t_valid) // SLC_BLOCK)
        o_cmp, imp = cmp_attn(q, kvc, nb=nb, kvh=kvh, group=group, hd=hd, n_c=n_c, n_slc=n_slc,
                              pos0=pos0, tq=t_rows)
        idx, okf = select_blocks(imp, n_slc=n_slc, pos0=pos0, emit_idx=True)
        q4 = q.astype(F32).reshape(nb, t_rows, kvh, group, hd).transpose(0, 2, 1, 3, 4).reshape(
            nb, kvh, t_rows * group, hd)
        wrows = cache_win.shape[1]
        o_slc4, o_win4 = sample_attn(q4, idx, okf, kv32, cache_kv,
                                     cache_win.reshape(nb, wrows, 2 * kvh * hd), page_table,
                                     kvh=kvh, group=group, hd=hd, t_valid=t_valid, past=past)

        def rows(o4):
            return o4.reshape(nb, kvh, t_rows, group, hd).transpose(0, 2, 1, 3, 4).reshape(m, qw).astype(BF16)

        o_slc, o_win = rows(o_slc4), rows(o_win4)
        win = jnp.concatenate([cache_win, win_new], axis=1)[:, t_valid:]
    o = combine(o_cmp, o_slc, o_win, gates, NSA_HEADS, hd)
    x3 = matmul(o, w["nsa_w_out16"], d, tm=tm, tn=512, tk=d, out_dtypes=[F32], res=x2)
    x4, conv1 = ffn(x3, 1)
    y = rmsnorm(x4, w["norm_final"], F32)
    return y, s_ret[None], jnp.stack([conv0, conv1]), kv_rows, win


def kernel(x_prompt, x_sample, cache_kv, cache_win, state_ret, state_conv, page_table, norm_mix, norm_ffn,
           ret_w_in, ret_w_out, kv_norm, kv_w, cmp_pe, cmp_w1, cmp_w2, nsa_w_in, nsa_w_out, ffn_w_up, ffn_conv_w,
           ffn_conv_b, ffn_w_down, norm_final):
    b, t, d = x_prompt.shape
    db, dt, _ = x_sample.shape
    past = page_table.shape[1] * cache_kv.shape[1]
    qw = nsa_w_out.shape[1]
    w = dict(
        norm_mix=norm_mix, norm_ffn=norm_ffn, kv_norm=kv_norm, norm_final=norm_final,
        cmp_pe=cmp_pe, cmp_w1=cmp_w1, cmp_w2=cmp_w2, ffn_conv_w=ffn_conv_w, ffn_conv_b=ffn_conv_b,
        ret_w_in16=ret_w_in[0].astype(BF16), ret_w_out16=ret_w_out[0].astype(BF16), kv_w16=kv_w.astype(BF16),
        nsa_w_in16=nsa_w_in[0].astype(BF16), nsa_w_gate16=nsa_w_in[0][:, qw:].astype(BF16),
        nsa_w_out16=nsa_w_out[0].astype(BF16),
        ffn_w_up16=[ffn_w_up[l].astype(BF16) for l in range(ffn_w_up.shape[0])],
        ffn_w_down16=[ffn_w_down[l].astype(BF16) for l in range(ffn_w_down.shape[0])],
    )
    heads_r = RET_HEADS
    dk = d // heads_r
    ret0 = jnp.zeros((b, heads_r, dk, 2 * dk), F32)
    conv0 = jnp.zeros((state_conv.shape[0], b, 2, ffn_conv_w.shape[-1]), F32)
    y_p, ret_p, conv_p, kv_p, win_p = _trunk(
        x_prompt.reshape(b * t, d), w, nb=b, t_rows=t, t_valid=t, pos0=0, ret_s0=ret0, conv_state=conv0,
        sample_ctx=None)
    tp = 2 * SUBLANES
    xs = jnp.zeros((db, tp, d), F32).at[:, :dt].set(x_sample).reshape(db * tp, d)
    y_s, ret_s, conv_s, kv_s, win_s = _trunk(
        xs, w, nb=db, t_rows=tp, t_valid=dt, pos0=past, ret_s0=state_ret[0], conv_state=state_conv,
        sample_ctx=(cache_kv, cache_win, page_table))
    y_s = y_s.reshape(db, tp, d)[:, :dt]
    return (y_p.reshape(b, t, d), y_s, kv_p, kv_s, win_p, win_s, ret_p, ret_s, conv_p, conv_s)
```

```python
import functools
import math

import jax
import jax.numpy as jnp
from jax import lax
from jax.experimental import pallas as pl
from jax.experimental.pallas import tpu as pltpu

F32 = jnp.float32
BF16 = jnp.bfloat16

RET_HEADS = 16
RET_CHUNK = 512
RET_THETA = 10000.0
NSA_HEADS = 32
NSA_KV_HEADS = 4
CMP_LEN = 32
CMP_STRIDE = 16
SLC_BLOCK = 64
N_SELECT = 16
WINDOW = 512
ROPE_THETA = 500000.0
PAGE_SIZE = 128
EPS = 1e-6
NEG = -1e30
FORCE = 1e9
TINY = 1e-20

LANES = 128
SUBLANES = 8
VMEM_LIMIT = 56 * 1024 * 1024


def _params(sem):
    return pltpu.CompilerParams(dimension_semantics=sem, vmem_limit_bytes=VMEM_LIMIT)


def _tile(dim, pref):
    if dim <= pref:
        return dim
    t = pref
    while dim % t:
        t //= 2
    return t


def _sigmoid(x):
    return 1.0 / (1.0 + jnp.exp(-x))


def _dot(a, b):
    return jnp.dot(a, b, preferred_element_type=F32)


def _dot_nt(a, b):
    return lax.dot_general(a, b, (((1,), (1,)), ((), ())), preferred_element_type=F32)


def _rmsnorm_kernel(x_ref, g_ref, o_ref):
    x = x_ref[...]
    y = x * lax.rsqrt(jnp.mean(x * x, axis=-1, keepdims=True) + EPS)
    o_ref[...] = (y * g_ref[...]).astype(o_ref.dtype)


def rmsnorm(x, g, out_dtype):
    m, d = x.shape
    tm = _tile(m, 256)
    return pl.pallas_call(
        _rmsnorm_kernel,
        grid=(m // tm,),
        in_specs=[pl.BlockSpec((tm, d), lambda i: (i, 0)),
                  pl.BlockSpec((1, d), lambda i: (0, 0))],
        out_specs=pl.BlockSpec((tm, d), lambda i: (i, 0)),
        out_shape=jax.ShapeDtypeStruct((m, d), out_dtype),
        compiler_params=_params(("parallel",)),
        name="rmsnorm",
    )(x, g.reshape(1, d).astype(F32))


def _rope_lanes(x, c, s_up, s_dn, half):
    return x * c + pltpu.roll(x, LANES - half, 1) * s_up + pltpu.roll(x, half, 1) * s_dn


def _mm_epilogue(acc, j, res_ref, rope_refs, out_refs, *, rope_every, rope_half, act):
    def store(val):
        for o in out_refs:
            o[...] = val.astype(o.dtype)

    if res_ref is not None:
        acc = acc + res_ref[...]
    if act == "sigmoid":
        acc = _sigmoid(acc)
    if not rope_every:
        store(acc)
        return
    tn = acc.shape[1]

    def roped():
        c, su, sd = (r[...] for r in rope_refs)
        pieces = [_rope_lanes(acc[:, a:a + LANES], c, su, sd, rope_half) for a in range(0, tn, LANES)]
        store(jnp.concatenate(pieces, axis=1))

    if rope_every == 1:
        roped()
    else:
        pl.when(j % rope_every == 0)(roped)
        pl.when(j % rope_every != 0)(lambda: store(acc))


def _mm_kernel(*refs, nk, has_res, rope_every, rope_half, act, n_out):
    x_ref, w_ref = refs[0], refs[1]
    pos = 2
    res_ref = None
    if has_res:
        res_ref = refs[pos]
        pos += 1
    rope_refs = None
    if rope_every:
        rope_refs = refs[pos:pos + 3]
        pos += 3
    out_refs = refs[pos:pos + n_out]
    acc_ref = refs[pos + n_out] if nk > 1 else None
    j = pl.program_id(1)
    k = pl.program_id(2)

    part = _dot(x_ref[...], w_ref[...])
    finish = functools.partial(_mm_epilogue, j=j, res_ref=res_ref, rope_refs=rope_refs, out_refs=out_refs,
                               rope_every=rope_every, rope_half=rope_half, act=act)

    if nk == 1:
        finish(part)
    else:
        @pl.when(k == 0)
        def _():
            acc_ref[...] = part

        @pl.when(k > 0)
        def _():
            acc_ref[...] += part

        @pl.when(k == nk - 1)
        def _():
            finish(acc_ref[...])


def matmul(x, w, n, *, tm, tn, tk, out_dtypes, res=None, rope=None, rope_every=0, rope_half=0, act=None):
    m, kdim = x.shape
    tm, tn, tk = _tile(m, tm), _tile(n, tn), _tile(kdim, tk)
    nk = kdim // tk
    in_specs = [pl.BlockSpec((tm, tk), lambda i, j, k: (i, k)),
                pl.BlockSpec((tk, tn), lambda i, j, k: (k, j))]
    args = [x, w]
    if res is not None:
        in_specs.append(pl.BlockSpec((tm, tn), lambda i, j, k: (i, j)))
        args.append(res)
    if rope is not None:
        period = rope[0].shape[0]
        nper = period // tm
        for t in rope:
            in_specs.append(pl.BlockSpec((tm, LANES), lambda i, j, k: (i % nper, 0)))
            args.append(t)
    out_specs = [pl.BlockSpec((tm, tn), lambda i, j, k: (i, j)) for _ in out_dtypes]
    out_shape = [jax.ShapeDtypeStruct((m, n), dt) for dt in out_dtypes]
    scratch = [pltpu.VMEM((tm, tn), F32)] if nk > 1 else []
    outs = pl.pallas_call(
        functools.partial(_mm_kernel, nk=nk, has_res=res is not None,
                          rope_every=rope_every if rope is not None else 0, rope_half=rope_half,
                          act=act, n_out=len(out_dtypes)),
        grid=(m // tm, n // tn, nk),
        in_specs=in_specs, out_specs=out_specs, out_shape=out_shape,
        scratch_shapes=scratch,
        compiler_params=_params(("parallel", "parallel", "arbitrary")),
        name="matmul",
    )(*args)
    return outs if len(outs) > 1 else outs[0]


def _mm_ws_kernel(*refs, has_res, rope_every, rope_half, n_out):
    x_ref, xs_ref, w_ref = refs[:3]
    pos = 3
    res_ref = ress_ref = rope_refs = ropes_refs = None
    if has_res:
        res_ref, ress_ref = refs[pos:pos + 2]
        pos += 2
    if rope_every:
        rope_refs, ropes_refs = refs[pos:pos + 3], refs[pos + 3:pos + 6]
        pos += 6
    out_refs = refs[pos:pos + n_out]
    outs_refs = refs[pos + n_out:pos + 2 * n_out]
    w16 = refs[pos + 2 * n_out]
    j = pl.program_id(0)
    i = pl.program_id(1)

    @pl.when(i == 0)
    def _():
        w16[...] = w_ref[...].astype(BF16)

    kw = dict(rope_every=rope_every, rope_half=rope_half, act=None)
    _mm_epilogue(_dot(x_ref[...], w16[...]), j, res_ref, rope_refs, out_refs, **kw)

    @pl.when(i == pl.num_programs(1) - 1)
    def _():
        _mm_epilogue(_dot(xs_ref[...], w16[...]), j, ress_ref, ropes_refs, outs_refs, **kw)


def matmul_ws(x, xs, w, n, *, tm, tn, out_dtypes, res=None, rope=None, rope_every=0, rope_half=0):
    m, kdim = x.shape
    ms = xs.shape[0]
    tm, tn = _tile(m, tm), _tile(n, tn)
    in_specs = [pl.BlockSpec((tm, kdim), lambda j, i: (i, 0)),
                pl.BlockSpec((ms, kdim), lambda j, i: (0, 0)),
                pl.BlockSpec((kdim, tn), lambda j, i: (0, j))]
    args = [x, xs, w]
    if res is not None:
        in_specs += [pl.BlockSpec((tm, tn), lambda j, i: (i, j)), pl.BlockSpec((ms, tn), lambda j, i: (0, j))]
        args += list(res)
    if rope is not None:
        nper = rope[0][0].shape[0] // tm
        in_specs += [pl.BlockSpec((tm, LANES), lambda j, i: (i % nper, 0))] * 3
        in_specs += [pl.BlockSpec((ms, LANES), lambda j, i: (0, 0))] * 3
        args += list(rope[0]) + list(rope[1])
    out_specs = ([pl.BlockSpec((tm, tn), lambda j, i: (i, j)) for _ in out_dtypes]
                 + [pl.BlockSpec((ms, tn), lambda j, i: (0, j)) for _ in out_dtypes])
    out_shape = ([jax.ShapeDtypeStruct((m, n), dt) for dt in out_dtypes]
                 + [jax.ShapeDtypeStruct((ms, n), dt) for dt in out_dtypes])
    outs = pl.pallas_call(
        functools.partial(_mm_ws_kernel, has_res=res is not None,
                          rope_every=rope_every if rope is not None else 0, rope_half=rope_half,
                          n_out=len(out_dtypes)),
        grid=(n // tn, m // tm),
        in_specs=in_specs, out_specs=out_specs, out_shape=out_shape,
        scratch_shapes=[pltpu.VMEM((kdim, tn), BF16)],
        compiler_params=_params(("parallel", "arbitrary")),
        name="matmul_ws",
    )(*args)
    k = len(out_dtypes)
    return outs[:k], outs[k:]


def _ffn_up_kernel(x_ref, xs_ref, wv_ref, wg_ref, cw_ref, cb_ref, st_ref, f1_ref, f2_ref,
                   a_ref, tail_ref, as_ref, tails_ref, wv16, wg16, carry_ref, *, tiles_per_seq, seg, n_sub):
    i = pl.program_id(1)

    @pl.when(i == 0)
    def _():
        wv16[...] = wv_ref[...].astype(BF16)
        wg16[...] = wg_ref[...].astype(BF16)

    cb = cb_ref[...]
    cw0, cw1, cw2 = cw_ref[0:1, :], cw_ref[1:2, :], cw_ref[2:3, :]

    def gated(val, gate, g1, g2):
        conv = cb + cw0 * g2 + cw1 * g1 + cw2 * gate
        return (val * (conv * _sigmoid(conv))).astype(BF16)

    @pl.when(i % tiles_per_seq == 0)
    def _():
        carry_ref[...] = st_ref[0]

    c0 = carry_ref[0:1, :]
    c1 = carry_ref[1:2, :]
    ts = x_ref.shape[0] // n_sub
    row = lax.broadcasted_iota(jnp.int32, (ts, 1), 0)
    for sb in range(n_sub):
        x = x_ref[sb * ts:(sb + 1) * ts, :]
        val = _dot(x, wv16[...])
        gate = _dot(x, wg16[...])
        g1 = jnp.where(row == 0, c1, pltpu.roll(gate, 1, 0))
        g2 = jnp.where(row == 0, c0, jnp.where(row == 1, c1, pltpu.roll(gate, 2, 0)))
        a_ref[sb * ts:(sb + 1) * ts, :] = gated(val, gate, g1, g2)
        c0 = gate[ts - 2:ts - 1, :]
        c1 = gate[ts - 1:ts, :]
    carry_ref[0:1, :] = c0
    carry_ref[1:2, :] = c1
    nt = tail_ref.shape[1]
    tail_ref[0] = gate[ts - nt:ts, :]

    @pl.when(i == pl.num_programs(1) - 1)
    def _():
        xs = xs_ref[...]
        val = _dot(xs, wv16[...])
        gate = _dot(xs, wg16[...])
        t = jnp.bitwise_and(lax.broadcasted_iota(jnp.int32, (xs.shape[0], 1), 0), seg - 1)
        g1 = jnp.where(t == 0, f1_ref[...], pltpu.roll(gate, 1, 0))
        g2 = jnp.where(t < 2, f2_ref[...], pltpu.roll(gate, 2, 0))
        as_ref[...] = gated(val, gate, g1, g2)
        tails_ref[...] = gate


def ffn_up(h, hs, w_up, layer, d_ff, conv_w, conv_b, *, tm, state, fill, seg):
    m, kdim = h.shape
    ms = hs.shape[0]
    assert seg & (seg - 1) == 0
    tn = _tile(d_ff, 256)
    nj = d_ff // tn
    tm = _tile(m, tm)
    nseq = state.shape[0]
    tps = (m // nseq) // tm
    a, tail, a_s, tail_s = pl.pallas_call(
        functools.partial(_ffn_up_kernel, tiles_per_seq=tps, seg=seg, n_sub=2 if tm >= 512 else 1),
        grid=(nj, m // tm),
        in_specs=[pl.BlockSpec((tm, kdim), lambda j, i: (i, 0)),
                  pl.BlockSpec((ms, kdim), lambda j, i: (0, 0)),
                  pl.BlockSpec((None, kdim, tn), lambda j, i: (layer, 0, j)),
                  pl.BlockSpec((None, kdim, tn), lambda j, i: (layer, 0, nj + j)),
                  pl.BlockSpec((3, tn), lambda j, i: (0, j)),
                  pl.BlockSpec((1, tn), lambda j, i: (0, j)),
                  pl.BlockSpec((1, 2, tn), lambda j, i: (i // tps, 0, j)),
                  pl.BlockSpec((ms, tn), lambda j, i: (0, j)),
                  pl.BlockSpec((ms, tn), lambda j, i: (0, j))],
        out_specs=[pl.BlockSpec((tm, tn), lambda j, i: (i, j)),
                   pl.BlockSpec((1, SUBLANES, tn), lambda j, i: (i // tps, 0, j)),
                   pl.BlockSpec((ms, tn), lambda j, i: (0, j)),
                   pl.BlockSpec((ms, tn), lambda j, i: (0, j))],
        out_shape=[jax.ShapeDtypeStruct((m, d_ff), BF16),
                   jax.ShapeDtypeStruct((nseq, SUBLANES, d_ff), F32),
                   jax.ShapeDtypeStruct((ms, d_ff), BF16),
                   jax.ShapeDtypeStruct((ms, d_ff), F32)],
        scratch_shapes=[pltpu.VMEM((kdim, tn), BF16), pltpu.VMEM((kdim, tn), BF16), pltpu.VMEM((2, tn), F32)],
        compiler_params=_params(("parallel", "arbitrary")),
        name="ffn_up",
    )(h, hs, w_up, w_up, conv_w, conv_b.reshape(1, d_ff), state, fill[0], fill[1])
    return (a, tail), (a_s, tail_s)


def _ret_kernel(lg_ref, q_ref, k_ref, v_ref, g_ref, cos_ref, sin_ref, s0_ref,
                o_ref, s_out_ref, s_scr, intra_scr, *, c_valid, dk):
    h = pl.program_id(1)
    c = pl.program_id(2)
    nc = pl.num_programs(2)
    half = dk // 2
    lg = lg_ref[h]
    cp = q_ref.shape[0]

    @pl.when(c == 0)
    def _():
        s_scr[...] = s0_ref[0, 0]
        di = lax.broadcasted_iota(jnp.int32, (cp, cp), 0)
        dj = lax.broadcasted_iota(jnp.int32, (cp, cp), 1)
        diff = di - dj
        intra_scr[...] = jnp.where((diff >= 0) & (dj < c_valid),
                                   jnp.exp(lg * jnp.maximum(diff, 0).astype(F32)), 0.0)

    cos = cos_ref[...]
    sin = sin_ref[...]

    def rope(x):
        x1 = x[:, :half]
        x2 = x[:, half:]
        return jnp.concatenate([x1 * cos - x2 * sin, x2 * cos + x1 * sin], axis=1)

    q = rope(q_ref[...].astype(F32))
    k = rope(k_ref[...].astype(F32)) * (dk ** -0.5)
    v = v_ref[...]
    ri = lax.broadcasted_iota(jnp.int32, (cp, 1), 0)
    rif = ri.astype(F32)
    read_decay = jnp.exp(lg * (rif + 1.0))
    write_decay = jnp.where(ri < c_valid, jnp.exp(lg * jnp.maximum(c_valid - 1.0 - rif, 0.0)), 0.0)

    s = s_scr[...]
    att = _dot_nt(q.astype(BF16), k.astype(BF16)) * intra_scr[...]
    o = _dot(att.astype(BF16), v) + _dot((q * read_decay).astype(BF16), s.astype(BF16))
    kw = (k * write_decay).astype(BF16)
    chunk_decay = jnp.exp(lg * jnp.full((1, 1), float(c_valid), F32))
    s_new = s * chunk_decay + lax.dot_general(kw, v, (((0,), (0,)), ((), ())), preferred_element_type=F32)
    s_scr[...] = s_new

    @pl.when(c == nc - 1)
    def _():
        s_out_ref[0, 0] = s_new

    on = o * lax.rsqrt(jnp.mean(o * o, axis=-1, keepdims=True) + EPS)
    gate = g_ref[...].astype(F32)
    o_ref[...] = (on * (gate * _sigmoid(gate))).astype(o_ref.dtype)


def retention(proj, s0, cos, sin, *, nb, heads, dk, dv, chunk, c_valid):
    m = proj.shape[0]
    t = m // nb
    nc = t // chunk
    qb = heads
    vb = 2 * heads * dk // dv
    log_g = jnp.log1p(-jnp.exp2(-5.0 - jnp.arange(heads, dtype=F32)))
    grid_spec = pltpu.PrefetchScalarGridSpec(
        num_scalar_prefetch=1,
        grid=(nb, heads, nc),
        in_specs=[
            pl.BlockSpec((chunk, dk), lambda b, h, c, lg: (b * nc + c, h)),
            pl.BlockSpec((chunk, dk), lambda b, h, c, lg: (b * nc + c, qb + h)),
            pl.BlockSpec((chunk, dv), lambda b, h, c, lg: (b * nc + c, vb + h)),
            pl.BlockSpec((chunk, dv), lambda b, h, c, lg: (b * nc + c, vb + heads + h)),
            pl.BlockSpec((chunk, dk // 2), lambda b, h, c, lg: (c, 0)),
            pl.BlockSpec((chunk, dk // 2), lambda b, h, c, lg: (c, 0)),
            pl.BlockSpec((1, 1, dk, dv), lambda b, h, c, lg: (b, h, 0, 0)),
        ],
        out_specs=[
            pl.BlockSpec((chunk, dv), lambda b, h, c, lg: (b * nc + c, h)),
            pl.BlockSpec((1, 1, dk, dv), lambda b, h, c, lg: (b, h, 0, 0)),
        ],
        scratch_shapes=[pltpu.VMEM((dk, dv), F32), pltpu.VMEM((chunk, chunk), F32)],
    )
    o, s = pl.pallas_call(
        functools.partial(_ret_kernel, c_valid=c_valid, dk=dk),
        grid_spec=grid_spec,
        out_shape=[jax.ShapeDtypeStruct((m, heads * dv), BF16),
                   jax.ShapeDtypeStruct((nb, heads, dk, dv), F32)],
        compiler_params=_params(("parallel", "parallel", "arbitrary")),
        name="retention",
    )(log_g, proj, proj, proj, proj, cos, sin, s0)
    return o, s


def _cmp_project(planes, w_ref, p_ref, kvh, hd):
    n_seg = planes.shape[1] // CMP_STRIDE
    for br in range(2):
        acc = jnp.zeros((kvh * n_seg, 2 * hd), F32)
        for sp in range(CMP_STRIDE // 2):
            pieces = []
            for g in range(kvh):
                plane = planes.at[br * kvh + g]
                a = plane[pl.ds(2 * sp, n_seg, stride=CMP_STRIDE), :]
                b = plane[pl.ds(2 * sp + 1, n_seg, stride=CMP_STRIDE), :]
                pieces.append(jnp.concatenate([a, b], axis=1))
            lhs = jnp.concatenate(pieces, axis=0).astype(BF16)
            acc = acc + _dot(lhs, w_ref[br, sp])
        for g in range(kvh):
            p_ref[0, br * kvh + g] = acc[g * n_seg:(g + 1) * n_seg, :]


def _cmp_proj_rows_kernel(x_ref, w_ref, p_ref, r_ref, *, kvh, hd):
    for hh in range(2 * kvh):
        r_ref[hh] = x_ref[:, hh * hd:(hh + 1) * hd]
    _cmp_project(r_ref, w_ref, p_ref, kvh, hd)


def _cmp_proj_pages_kernel(pt_ref, cache_ref, w_ref, p_ref, r_ref, sem, *, kvh, hd, pages_per_step, n_pages):
    b = pl.program_id(0)
    c = pl.program_id(1)
    nchunk = pl.num_programs(1)
    step = b * nchunk + c
    nsteps = pl.num_programs(0) * nchunk
    page = cache_ref.shape[1]

    def copies(st, slot):
        bb = st // nchunk
        cc = st % nchunk
        out = []
        for u in range(pages_per_step):
            phys = pt_ref[bb * n_pages + cc * pages_per_step + u]
            for hh in range(2 * kvh):
                out.append(pltpu.make_async_copy(
                    cache_ref.at[phys, :, hh // kvh, hh % kvh, :],
                    r_ref.at[slot, hh, pl.ds(u * page, page), :],
                    sem.at[slot]))
        return out

    slot = step % 2

    @pl.when(step == 0)
    def _():
        for cp in copies(step, slot):
            cp.start()

    @pl.when(step + 1 < nsteps)
    def _():
        for cp in copies(step + 1, 1 - slot):
            cp.start()

    for cp in copies(step, slot):
        cp.wait()
    _cmp_project(r_ref.at[slot], w_ref, p_ref, kvh, hd)


def _cmp_w1_pairs(cmp_w1, hd):
    r = CMP_LEN // CMP_STRIDE
    e = cmp_w1.shape[-1]
    w = cmp_w1.reshape(2, r, CMP_STRIDE // 2, 2, hd, e)
    w = w.transpose(0, 2, 3, 4, 1, 5)
    return w.reshape(2, CMP_STRIDE // 2, 2 * hd, r * e).astype(BF16)


def cmp_proj_rows(rows2d, nb, t, w_pairs, kvh, hd):
    n_seg = t // CMP_STRIDE
    width = 2 * kvh * hd
    return pl.pallas_call(
        functools.partial(_cmp_proj_rows_kernel, kvh=kvh, hd=hd),
        grid=(nb,),
        in_specs=[pl.BlockSpec((t, width), lambda b: (b, 0)),
                  pl.BlockSpec(w_pairs.shape, lambda b: (0, 0, 0, 0))],
        out_specs=pl.BlockSpec((1, 2 * kvh, n_seg, 2 * hd), lambda b: (b, 0, 0, 0)),
        out_shape=jax.ShapeDtypeStruct((nb, 2 * kvh, n_seg, 2 * hd), F32),
        scratch_shapes=[pltpu.VMEM((2 * kvh, t, hd), F32)],
        compiler_params=_params(("parallel",)),
        name="cmp_proj_prompt",
    )(rows2d, w_pairs)


def cmp_proj_pages(cache, page_table, w_pairs, kvh, hd, pages_per_step):
    nb, n_pages = page_table.shape
    page = cache.shape[1]
    n_seg = pages_per_step * page // CMP_STRIDE
    nchunk = n_pages // pages_per_step
    grid_spec = pltpu.PrefetchScalarGridSpec(
        num_scalar_prefetch=1,
        grid=(nb, nchunk),
        in_specs=[pl.BlockSpec(memory_space=pl.ANY),
                  pl.BlockSpec(w_pairs.shape, lambda b, c, pt: (0, 0, 0, 0))],
        out_specs=pl.BlockSpec((1, 2 * kvh, n_seg, 2 * hd), lambda b, c, pt: (b, 0, c, 0)),
        scratch_shapes=[pltpu.VMEM((2, 2 * kvh, pages_per_step * page, hd), F32),
                        pltpu.SemaphoreType.DMA((2,))],
    )
    return pl.pallas_call(
        functools.partial(_cmp_proj_pages_kernel, kvh=kvh, hd=hd, pages_per_step=pages_per_step,
                          n_pages=n_pages),
        grid_spec=grid_spec,
        out_shape=jax.ShapeDtypeStruct((nb, 2 * kvh, nchunk * n_seg, 2 * hd), F32),
        compiler_params=_params(("arbitrary", "arbitrary")),
        name="cmp_proj_sample",
    )(page_table.reshape(-1), cache, w_pairs)


def _cmp_pe_kernel(pe_ref, w_ref, o_ref):
    o_ref[0] = _dot(pe_ref[0], w_ref[0])


def cmp_pe_term(cmp_pe, cmp_w1):
    _, n, hd = cmp_pe.shape
    e = cmp_w1.shape[-1]
    pe_flat = jnp.zeros((2, SUBLANES, n * hd), BF16).at[:, 0].set(cmp_pe.reshape(2, n * hd).astype(BF16))
    w_flat = cmp_w1.reshape(2, n * hd, e).astype(BF16)
    return pl.pallas_call(
        _cmp_pe_kernel,
        grid=(2,),
        in_specs=[pl.BlockSpec((1, SUBLANES, n * hd), lambda b: (b, 0, 0)),
                  pl.BlockSpec((1, n * hd, e), lambda b: (b, 0, 0))],
        out_specs=pl.BlockSpec((1, SUBLANES, e), lambda b: (b, 0, 0)),
        out_shape=jax.ShapeDtypeStruct((2, SUBLANES, e), F32),
        compiler_params=_params(("parallel",)),
        name="cmp_pe",
    )(pe_flat, w_flat)


def _cmp_finish_kernel(p_ref, pe_ref, w2_ref, o_ref):
    p = p_ref[0, 0]
    n_seg = p.shape[0]
    e = p.shape[1] // 2
    hid = p[:, :e] + pltpu.roll(p[:, e:], n_seg - 1, 0) + pe_ref[0, 0:1, :]
    c = math.sqrt(2.0 / math.pi)
    act = 0.5 * hid * (1.0 + jnp.tanh(c * (hid + 0.044715 * (hid * hid * hid))))
    o_ref[0, 0] = _dot(act.astype(BF16), w2_ref[0]).astype(o_ref.dtype)


def cmp_finish(p, pe_term, cmp_w2, kvh):
    nb, nu, n_seg, e2 = p.shape
    e = e2 // 2
    hd = cmp_w2.shape[-1]
    return pl.pallas_call(
        _cmp_finish_kernel,
        grid=(nb, nu),
        in_specs=[pl.BlockSpec((1, 1, n_seg, e2), lambda b, u: (b, u, 0, 0)),
                  pl.BlockSpec((1, SUBLANES, e), lambda b, u: (u // kvh, 0, 0)),
                  pl.BlockSpec((1, e, hd), lambda b, u: (u // kvh, 0, 0))],
        out_specs=pl.BlockSpec((1, 1, n_seg, hd), lambda b, u: (b, u, 0, 0)),
        out_shape=jax.ShapeDtypeStruct((nb, nu, n_seg, hd), BF16),
        compiler_params=_params(("parallel", "parallel")),
        name="cmp_finish",
    )(p, pe_term, cmp_w2.astype(BF16))


def _cmp_attn_kernel(q_ref, kc_ref, vc_ref, agg_ref, o_ref, imp_ref, *, group, hd, n_c, pos0):
    qi = pl.program_id(2)
    tq = q_ref.shape[0]
    kc = kc_ref[0, 0]
    vc = vc_ref[0, 0]
    agg = agg_ref[...]
    n_cp = kc.shape[0]
    qs = jnp.concatenate([q_ref[:, h * hd:(h + 1) * hd] for h in range(group)], axis=0)
    qpos1 = pos0 + qi * tq + lax.broadcasted_iota(jnp.int32, (tq, 1), 0)
    qpos = jnp.concatenate([qpos1] * group, axis=0)
    ci = lax.broadcasted_iota(jnp.int32, (1, n_cp), 1)
    cm = ((ci * CMP_STRIDE + (CMP_LEN - 1)) <= qpos) & (ci < n_c)
    s = jnp.where(cm, _dot_nt(qs, kc) * (hd ** -0.5), NEG)
    e = jnp.where(cm, jnp.exp(s - jnp.max(s, axis=-1, keepdims=True)), 0.0)
    p = e / jnp.maximum(jnp.sum(e, axis=-1, keepdims=True), TINY)
    pb = p.astype(BF16)
    o = _dot(pb, vc)
    psum = jnp.zeros((tq, n_cp), F32)
    for h in range(group):
        o_ref[:, h * hd:(h + 1) * hd] = o[h * tq:(h + 1) * tq, :].astype(o_ref.dtype)
        psum = psum + pb[h * tq:(h + 1) * tq, :].astype(F32)
    hi = psum.astype(BF16)
    lo = (psum - hi.astype(F32)).astype(BF16)
    imp_ref[0, 0] = _dot_nt(agg, hi) + _dot_nt(agg, lo)


def _slc_aggregation_t(n_cp, n_slc, rows):
    rs = SLC_BLOCK // CMP_STRIDE
    rc = CMP_LEN // CMP_STRIDE
    i = jnp.arange(n_cp)[None, :]
    j = jnp.arange(rows)[:, None]
    w = sum((i == j * rs + m - n).astype(F32) for m in range(rs) for n in range(rc))
    return jnp.where(j < n_slc, w, 0.0).astype(BF16)


def cmp_attn(q, kvc, *, nb, kvh, group, hd, n_c, n_slc, pos0, tq):
    m = q.shape[0]
    t = m // nb
    tq = _tile(t, tq)
    nq = t // tq
    n_cp = kvc.shape[2]
    rows = -(-n_slc // SUBLANES) * SUBLANES
    agg_t = _slc_aggregation_t(n_cp, n_slc, rows)
    gw = group * hd
    return pl.pallas_call(
        functools.partial(_cmp_attn_kernel, group=group, hd=hd, n_c=n_c, pos0=pos0),
        grid=(nb, kvh, nq),
        in_specs=[pl.BlockSpec((tq, gw), lambda b, g, i: (b * nq + i, g)),
                  pl.BlockSpec((1, 1, n_cp, hd), lambda b, g, i: (b, g, 0, 0)),
                  pl.BlockSpec((1, 1, n_cp, hd), lambda b, g, i: (b, kvh + g, 0, 0)),
                  pl.BlockSpec((rows, n_cp), lambda b, g, i: (0, 0))],
        out_specs=[pl.BlockSpec((tq, gw), lambda b, g, i: (b * nq + i, g)),
                   pl.BlockSpec((1, 1, rows, tq), lambda b, g, i: (b, g, 0, i))],
        out_shape=[jax.ShapeDtypeStruct((m, kvh * gw), BF16),
                   jax.ShapeDtypeStruct((nb, kvh, rows, t), F32)],
        compiler_params=_params(("parallel", "parallel", "parallel")),
        name="cmp_attn",
    )(q, kvc, kvc, agg_t)


def _select_kernel(imp_ref, *rest, n_slc, pos0, t_len, emit_idx):
    if emit_idx:
        idx_ref, ok_ref, sc_ref = rest
    else:
        bias_ref, sc_ref = rest
    rows, width = imp_ref.shape
    lane = pl.program_id(0) * width + lax.broadcasted_iota(jnp.int32, (1, width), 1)
    qpos = pos0 + jnp.bitwise_and(lane, t_len - 1)
    blk = lax.broadcasted_iota(jnp.int32, (rows, width), 0)
    cur = jnp.right_shift(qpos, SLC_BLOCK.bit_length() - 1)
    visible = blk * SLC_BLOCK <= qpos
    forced = (blk == 0) | (blk == cur) | (blk == cur - 1)
    score = jnp.where(visible, jnp.where(forced, FORCE, imp_ref[...]), NEG)
    sc_ref[...] = score

    def count(i, cnt):
        si = sc_ref[pl.ds(i, 1), :]
        beats = (si > score) | ((si == score) & (i < blk))
        return cnt + jnp.where(beats, 1, 0)

    rank = lax.fori_loop(0, n_slc, count, jnp.zeros((rows, width), jnp.int32))
    in_range = blk < n_slc
    if emit_idx:
        blk_f = blk.astype(F32)
        for r in range(idx_ref.shape[0]):
            hit = (rank == r) & in_range
            idx_r = jnp.sum(jnp.where(hit, blk_f, 0.0), axis=0, keepdims=True)
            idx_ref[r:r + 1, :] = idx_r.astype(jnp.int32)
            ok_ref[r:r + 1, :] = jnp.sum(jnp.where(hit & visible, 1.0, 0.0), axis=0, keepdims=True)
    else:
        chosen = (rank < N_SELECT) & visible & in_range
        bias_ref[...] = jnp.where(chosen, 0.0, NEG).astype(bias_ref.dtype)


def select_blocks(imp, *, n_slc, pos0, emit_idx):
    nb, kvh, rows, t = imp.shape
    assert t & (t - 1) == 0
    lanes = nb * kvh * t
    imp2 = imp.transpose(2, 0, 1, 3).reshape(rows, lanes)
    width = _tile(lanes, 2048)
    spec = pl.BlockSpec((rows, width), lambda i: (0, i))
    if emit_idx:
        nsel = min(N_SELECT, n_slc)
        o_spec = pl.BlockSpec((nsel, width), lambda i: (0, i))
        out_specs = [o_spec, o_spec]
        out_shape = [jax.ShapeDtypeStruct((nsel, lanes), jnp.int32), jax.ShapeDtypeStruct((nsel, lanes), F32)]
    else:
        out_specs = [spec]
        out_shape = [jax.ShapeDtypeStruct((rows, lanes), BF16)]
    outs = pl.pallas_call(
        functools.partial(_select_kernel, n_slc=n_slc, pos0=pos0, t_len=t, emit_idx=emit_idx),
        grid=(lanes // width,),
        in_specs=[spec], out_specs=out_specs, out_shape=out_shape,
        scratch_shapes=[pltpu.VMEM((rows, width), F32)],
        compiler_params=_params(("parallel",)),
        name="select_blocks",
    )(imp2)
    if emit_idx:
        return tuple(o.reshape(-1, nb, kvh, t).transpose(1, 2, 0, 3) for o in outs)
    return outs[0].reshape(rows, nb, kvh, t).transpose(1, 2, 3, 0)


V_ROWS_PAD = 2 * SUBLANES


def _prompt_attn_kernel(*refs, mode, nq, hd):
    if mode == "slc":
        q_ref, bias_ref, k_ref, vt_ref, o_ref = refs
    else:
        q_ref, k_ref, vt_ref, o_ref = refs
    qi = pl.program_id(2)
    tq = q_ref.shape[0]
    n_heads = q_ref.shape[1] // hd
    kl = lax.broadcasted_iota(jnp.int32, (tq, tq), 0)
    ql = lax.broadcasted_iota(jnp.int32, (tq, tq), 1)

    def run_head(hp, k0, nkeys, band):
        q = (q_ref[:, hp * hd:(hp + 1) * hd].astype(F32) * (hd ** -0.5 * math.log2(math.e))).astype(BF16)
        if mode == "slc":
            q = jnp.concatenate([q, bias_ref[0, 0]], axis=1)
        st = _dot_nt(k_ref[pl.ds(k0, nkeys), :], q)
        parts = []
        if nkeys > tq:
            top = st[:nkeys - tq, :]
            if band:
                top = jnp.where(kl > ql, top, NEG)
            parts.append(top)
        parts.append(jnp.where(kl <= ql, st[nkeys - tq:, :], NEG))
        st = jnp.concatenate(parts, axis=0) if len(parts) > 1 else parts[0]
        p = jnp.exp2(st - jnp.max(st, axis=0, keepdims=True)).astype(BF16)
        ot = _dot(vt_ref[0, 0, :, pl.ds(k0, nkeys)], p)
        o_ref[:, hp * hd:(hp + 1) * hd] = (ot[:hd, :] / ot[hd:hd + 1, :]).T.astype(o_ref.dtype)

    def run(k0, nkeys, band):
        for hp in range(n_heads):
            run_head(hp, k0, nkeys, band)

    if mode == "slc":
        for v in range(nq):
            pl.when(qi == v)(functools.partial(run, 0, (v + 1) * tq, False))
    else:
        pl.when(qi == 0)(functools.partial(run, 0, tq, False))
        if nq > 1:
            pl.when(qi > 0)(lambda: run(pl.multiple_of((qi - 1) * tq, tq), 2 * tq, True))


def prompt_attn(q, k_rows, vt, bias, *, mode, nb, kvh, group, hd, tq, heads_per_step=2):
    m = q.shape[0]
    t = m // nb
    tq = _tile(t, tq)
    nq = t // tq
    kw = k_rows.shape[1] // kvh
    assert mode == "slc" or tq == WINDOW or nq == 1
    hw = heads_per_step * hd
    hsteps = group // heads_per_step
    in_specs = [pl.BlockSpec((tq, hw), lambda b, g, i, h: (b * nq + i, g * hsteps + h))]
    args = [q]
    if mode == "slc":
        in_specs.append(pl.BlockSpec((1, 1, tq, hd), lambda b, g, i, h: (b, g, i, 0)))
        args.append(bias)
    in_specs += [pl.BlockSpec((t, kw), lambda b, g, i, h: (b, g)),
                 pl.BlockSpec((1, 1, hd + V_ROWS_PAD, t), lambda b, g, i, h: (b, g, 0, 0))]
    args += [k_rows, vt]
    return pl.pallas_call(
        functools.partial(_prompt_attn_kernel, mode=mode, nq=nq, hd=hd),
        grid=(nb, kvh, nq, hsteps),
        in_specs=in_specs,
        out_specs=pl.BlockSpec((tq, hw), lambda b, g, i, h: (b * nq + i, g * hsteps + h)),
        out_shape=jax.ShapeDtypeStruct((m, kvh * group * hd), BF16),
        compiler_params=_params(("parallel", "parallel", "parallel", "parallel")),
        name="prompt_attn_" + mode,
    )(*args)


def _sample_attn_kernel(idx_ref, pt_ref, q_ref, kpos_ref, okx_ref, kn_ref, vn_ref, wk_ref, wv_ref,
                        wkn_ref, wvn_ref, cache_ref, oslc_ref, owin_ref, kbuf, vbuf, sem,
                        *, kvh, group, hd, t_valid, past, nsel, ks_sec, vs_sec):
    b = pl.program_id(0)
    g = pl.program_id(1)
    tp = q_ref.shape[2] // group
    n_blocks = past // SLC_BLOCK
    per_page = PAGE_SIZE // SLC_BLOCK
    n_fetch = t_valid * nsel

    def copies(f):
        t = f // nsel
        r = f % nsel
        blk = idx_ref[((b * kvh + g) * nsel + r) * tp + t]
        blk = jnp.minimum(blk, n_blocks - 1)
        page = pt_ref[b * (past // PAGE_SIZE) + blk // per_page]
        row0 = (blk % per_page) * SLC_BLOCK
        dst = pl.ds(f * SLC_BLOCK, SLC_BLOCK)
        ck = pltpu.make_async_copy(cache_ref.at[page, pl.ds(row0, SLC_BLOCK), ks_sec, g, :],
                                   kbuf.at[dst, :], sem.at[0])
        cv = pltpu.make_async_copy(cache_ref.at[page, pl.ds(row0, SLC_BLOCK), vs_sec, g, :],
                                   vbuf.at[dst, :], sem.at[1])
        return ck, cv

    def start(f, c):
        ck, cv = copies(f)
        ck.start()
        cv.start()
        return c

    def wait(f, c):
        ck, cv = copies(f)
        ck.wait()
        cv.wait()
        return c

    lax.fori_loop(0, n_fetch, start, 0)

    scale = hd ** -0.5
    qa = q_ref[0, 0].astype(BF16)
    trow = jnp.right_shift(lax.broadcasted_iota(jnp.int32, (tp * group, 1), 0), group.bit_length() - 1)
    wk = wk_ref[0].astype(BF16)
    wrows = wk.shape[0]
    s_c = _dot_nt(qa, wk) * scale
    s_n = _dot_nt(qa, wkn_ref[...].astype(BF16)) * scale
    d_c = (past + trow) - (past - wrows + lax.broadcasted_iota(jnp.int32, (1, wrows), 1))
    jn = lax.broadcasted_iota(jnp.int32, (1, tp), 1)
    d_n = trow - jn
    m_c = (d_c >= 0) & (d_c < WINDOW)
    m_n = (d_n >= 0) & (d_n < WINDOW) & (jn < t_valid)
    s_c = jnp.where(m_c, s_c, NEG)
    s_n = jnp.where(m_n, s_n, NEG)
    mx = jnp.maximum(jnp.max(s_c, axis=-1, keepdims=True), jnp.max(s_n, axis=-1, keepdims=True))
    e_c = jnp.where(m_c, jnp.exp(s_c - mx), 0.0)
    e_n = jnp.where(m_n, jnp.exp(s_n - mx), 0.0)
    den = jnp.maximum(jnp.sum(e_c, axis=-1, keepdims=True) + jnp.sum(e_n, axis=-1, keepdims=True), TINY)
    o_w = _dot((e_c / den).astype(BF16), wv_ref[0].astype(BF16)) + \
        _dot((e_n / den).astype(BF16), wvn_ref[...].astype(BF16))
    owin_ref[0, 0] = o_w.astype(owin_ref.dtype)

    lax.fori_loop(0, n_fetch, wait, 0)

    kn = kn_ref[...].astype(BF16)
    vn = vn_ref[...].astype(BF16)
    span = nsel * SLC_BLOCK
    oslc_ref[...] = jnp.zeros(oslc_ref.shape, oslc_ref.dtype)
    for t in range(t_valid):
        qt = q_ref[0, 0, t * group:(t + 1) * group, :].astype(BF16)
        kt = kbuf[t * span:(t + 1) * span, :].astype(BF16)
        vt = vbuf[t * span:(t + 1) * span, :].astype(BF16)
        kp = kpos_ref[0, 0, t:t + 1, :]
        okv = okx_ref[0, 0, t:t + 1, :] > 0.5
        qpos = past + t
        m_g = (kp <= qpos) & okv & (kp < past)
        s_g = jnp.where(m_g, _dot_nt(qt, kt) * scale, NEG)
        selrow = jnp.zeros((1, tp), F32)
        for j in range(t_valid):
            hit = jnp.max(jnp.where((kp == past + j) & okv, 1.0, 0.0), axis=-1, keepdims=True)
            selrow = selrow + jnp.where(jn == j, hit, 0.0)
        m_w = (selrow > 0.5) & (jn <= t)
        s_w = jnp.where(m_w, _dot_nt(qt, kn) * scale, NEG)
        mx = jnp.maximum(jnp.max(s_g, axis=-1, keepdims=True), jnp.max(s_w, axis=-1, keepdims=True))
        e_g = jnp.where(m_g, jnp.exp(s_g - mx), 0.0)
        e_w = jnp.where(m_w, jnp.exp(s_w - mx), 0.0)
        den = jnp.sum(e_g, axis=-1, keepdims=True) + jnp.sum(e_w, axis=-1, keepdims=True)
        o_t = _dot((e_g / den).astype(BF16), vt) + _dot((e_w / den).astype(BF16), vn)
        oslc_ref[0, 0, t * group:(t + 1) * group, :] = o_t.astype(oslc_ref.dtype)


def sample_attn(q4, idx, okf, kv_new, cache_kv4, cache_win3, page_table, *, kvh, group, hd, t_valid, past):
    nb = q4.shape[0]
    tp = q4.shape[2] // group
    nsel = idx.shape[2]
    span = nsel * SLC_BLOCK
    wrows = cache_win3.shape[1]
    idx_t = idx.transpose(0, 1, 3, 2)
    kpos = (idx_t[..., None] * SLC_BLOCK + jnp.arange(SLC_BLOCK, dtype=jnp.int32)).reshape(nb, kvh, tp, span)
    okx = jnp.broadcast_to(okf.transpose(0, 1, 3, 2)[..., None], (nb, kvh, tp, nsel, SLC_BLOCK)).reshape(
        nb, kvh, tp, span)
    grid_spec = pltpu.PrefetchScalarGridSpec(
        num_scalar_prefetch=2,
        grid=(nb, kvh),
        in_specs=[
            pl.BlockSpec((1, 1, tp * group, hd), lambda b, g, *_: (b, g, 0, 0)),
            pl.BlockSpec((1, 1, tp, span), lambda b, g, *_: (b, g, 0, 0)),
            pl.BlockSpec((1, 1, tp, span), lambda b, g, *_: (b, g, 0, 0)),
            pl.BlockSpec((tp, hd), lambda b, g, *_: (b, 2 * kvh + g)),
            pl.BlockSpec((tp, hd), lambda b, g, *_: (b, 3 * kvh + g)),
            pl.BlockSpec((1, wrows, hd), lambda b, g, *_: (b, 0, g)),
            pl.BlockSpec((1, wrows, hd), lambda b, g, *_: (b, 0, kvh + g)),
            pl.BlockSpec((tp, hd), lambda b, g, *_: (b, 4 * kvh + g)),
            pl.BlockSpec((tp, hd), lambda b, g, *_: (b, 5 * kvh + g)),
            pl.BlockSpec(memory_space=pl.ANY),
        ],
        out_specs=[pl.BlockSpec((1, 1, tp * group, hd), lambda b, g, *_: (b, g, 0, 0)),
                   pl.BlockSpec((1, 1, tp * group, hd), lambda b, g, *_: (b, g, 0, 0))],
        scratch_shapes=[pltpu.VMEM((t_valid * span, hd), F32),
                        pltpu.VMEM((t_valid * span, hd), F32),
                        pltpu.SemaphoreType.DMA((2,))],
    )
    return pl.pallas_call(
        functools.partial(_sample_attn_kernel, kvh=kvh, group=group, hd=hd, t_valid=t_valid, past=past,
                          nsel=nsel, ks_sec=2, vs_sec=3),
        grid_spec=grid_spec,
        out_shape=[jax.ShapeDtypeStruct(q4.shape, F32), jax.ShapeDtypeStruct(q4.shape, F32)],
        compiler_params=_params(("arbitrary", "arbitrary")),
        name="sample_attn",
    )(idx.reshape(-1), page_table.reshape(-1), q4, kpos, okx, kv_new, kv_new, cache_win3, cache_win3,
      kv_new, kv_new, cache_kv4)


def _combine_kernel(oc_ref, os_ref, ow_ref, g_ref, o_ref, *, heads, hd):
    gates = g_ref[...]
    for h in range(heads):
        sl = slice(h * hd, (h + 1) * hd)
        o = (gates[:, h:h + 1] * oc_ref[:, sl].astype(F32)
             + gates[:, heads + h:heads + h + 1] * os_ref[:, sl].astype(F32)
             + gates[:, 2 * heads + h:2 * heads + h + 1] * ow_ref[:, sl].astype(F32))
        o_ref[:, sl] = o.astype(o_ref.dtype)


def combine(o_cmp, o_slc, o_win, gates, heads, hd):
    m, d = o_cmp.shape
    tm = _tile(m, 256)
    spec = pl.BlockSpec((tm, d), lambda i: (i, 0))
    return pl.pallas_call(
        functools.partial(_combine_kernel, heads=heads, hd=hd),
        grid=(m // tm,),
        in_specs=[spec, spec, spec, pl.BlockSpec((tm, 3 * heads), lambda i: (i, 0))],
        out_specs=spec,
        out_shape=jax.ShapeDtypeStruct((m, d), BF16),
        compiler_params=_params(("parallel",)),
        name="combine",
    )(o_cmp, o_slc, o_win, gates)


def _rope_tables_half(pos, half, theta):
    inv = jnp.power(theta, -jnp.arange(half, dtype=F32) / half)
    ang = pos.astype(F32)[:, None] * inv[None, :]
    return jnp.cos(ang), jnp.sin(ang)


def _nsa_rope_tables(pos, hd):
    half = hd // 8
    cos, sin = _rope_tables_half(pos, half, ROPE_THETA)
    n = pos.shape[0]
    ones = jnp.ones((n, hd - 2 * half), F32)
    zeros = jnp.zeros((n, hd - 2 * half), F32)
    zh = jnp.zeros((n, half), F32)
    c = jnp.concatenate([cos, cos, ones], axis=1)
    s_up = jnp.concatenate([-sin, zh, zeros], axis=1)
    s_dn = jnp.concatenate([zh, sin, zeros], axis=1)
    return c, s_up, s_dn


class _Stream:
    def __init__(self, x, nb, t_rows, t_valid, pos0, ret_s0, conv_state, ctx):
        self.x, self.nb, self.t_rows, self.t_valid, self.pos0 = x, nb, t_rows, t_valid, pos0
        self.ret_s0, self.conv_state, self.ctx = ret_s0, conv_state, ctx
        self.m = nb * t_rows
        self.pos = pos0 + jnp.arange(t_rows)


def _nsa_branches(st, q, kv32, kv16, w_pairs, pe_term, w, *, hd, kvh, group):
    nb, t_rows, t_valid, pos0, m = st.nb, st.t_rows, st.t_valid, st.pos0, st.m
    qw = kvh * group * hd
    win_new = kv32[:, 4 * kvh * hd:].reshape(nb, t_rows, 2, kvh, hd)[:, :t_valid]
    if st.ctx is None:
        p = cmp_proj_rows(kv32, nb, t_rows, w_pairs, kvh, hd)
        kvc = cmp_finish(p, pe_term, w["cmp_w2"], kvh)
        n_seg = t_rows // CMP_STRIDE
        n_c = n_seg - CMP_LEN // CMP_STRIDE + 1
        n_slc = -(-t_rows // SLC_BLOCK)
        o_cmp, imp = cmp_attn(q, kvc, nb=nb, kvh=kvh, group=group, hd=hd, n_c=n_c, n_slc=n_slc,
                              pos0=pos0, tq=512)
        bias = select_blocks(imp, n_slc=n_slc, pos0=pos0, emit_idx=False)
        bias = jnp.pad(bias, ((0, 0), (0, 0), (0, 0), (0, hd - bias.shape[-1])))
        kv5 = kv16.reshape(nb, t_rows, 6, kvh, hd)
        onehot = (jnp.arange(t_rows)[:, None] // SLC_BLOCK == jnp.arange(hd)[None, :]).astype(BF16)
        k_slc = jnp.concatenate([kv5[:, :, 2], jnp.broadcast_to(onehot[None, :, None, :], (nb, t_rows, kvh, hd))],
                                axis=-1).reshape(m, kvh * 2 * hd)
        k_win = kv5[:, :, 4].reshape(m, kvh * hd)
        ones_rows = jnp.zeros((nb, kvh, V_ROWS_PAD, t_rows), BF16).at[:, :, 0].set(1.0)

        def v_t(sec):
            return jnp.concatenate([kv5[:, :, sec].transpose(0, 2, 3, 1), ones_rows], axis=2)

        o_slc = prompt_attn(q, k_slc, v_t(3), bias, mode="slc", nb=nb, kvh=kvh, group=group, hd=hd, tq=WINDOW)
        o_win = prompt_attn(q, k_win, v_t(5), None, mode="win", nb=nb, kvh=kvh, group=group, hd=hd, tq=WINDOW)
        win = win_new[:, t_valid - min(WINDOW, t_valid):]
    else:
        cache_kv, cache_win, page_table = st.ctx
        past = pos0
        p = cmp_proj_pages(cache_kv, page_table, w_pairs, kvh, hd, pages_per_step=min(16, page_table.shape[1]))
        kvc = cmp_finish(p, pe_term, w["cmp_w2"], kvh)
        n_seg = (past + t_valid) // CMP_STRIDE
        n_c = n_seg - CMP_LEN // CMP_STRIDE + 1
        n_slc = -(-(past + t_valid) // SLC_BLOCK)
        o_cmp, imp = cmp_attn(q, kvc, nb=nb, kvh=kvh, group=group, hd=hd, n_c=n_c, n_slc=n_slc,
                              pos0=pos0, tq=t_rows)
        idx, okf = select_blocks(imp, n_slc=n_slc, pos0=pos0, emit_idx=True)
        q4 = q.astype(F32).reshape(nb, t_rows, kvh, group, hd).transpose(0, 2, 1, 3, 4).reshape(
            nb, kvh, t_rows * group, hd)
        wrows = cache_win.shape[1]
        o_slc4, o_win4 = sample_attn(q4, idx, okf, kv32, cache_kv,
                                     cache_win.reshape(nb, wrows, 2 * kvh * hd), page_table,
                                     kvh=kvh, group=group, hd=hd, t_valid=t_valid, past=past)

        def rows(o4):
            return o4.reshape(nb, kvh, t_rows, group, hd).transpose(0, 2, 1, 3, 4).reshape(m, qw).astype(BF16)

        o_slc, o_win = rows(o_slc4), rows(o_win4)
        win = jnp.concatenate([cache_win, win_new], axis=1)[:, t_valid:]
    return o_cmp, o_slc, o_win, win


def _trunk(sp, ss, w):
    d = sp.x.shape[1]
    streams = (sp, ss)
    tm = 1024
    d_ff = w["ffn_conv_w"].shape[-1]
    heads_r = RET_HEADS
    dk = d // heads_r
    dv = 2 * dk
    hd = d // NSA_HEADS
    kvh = NSA_KV_HEADS
    group = NSA_HEADS // kvh
    qw = NSA_HEADS * hd
    rope_half = hd // 8

    def norm(xs, g, dt=BF16):
        return [rmsnorm(x, g, dt) for x in xs]

    def ffn(xs, layer):
        hs = norm(xs, w["norm_ffn"][layer])
        st = ss.conv_state[layer]
        f1 = jnp.zeros((ss.nb, ss.t_rows, d_ff), F32).at[:, 0].set(st[:, 1])
        f2 = jnp.zeros((ss.nb, ss.t_rows, d_ff), F32).at[:, 0].set(st[:, 0]).at[:, 1].set(st[:, 1])
        (a_p, tail_p), (a_s, tail_s) = ffn_up(
            hs[0], hs[1], w["ffn_w_up"], layer, d_ff, w["ffn_conv_w"][layer], w["ffn_conv_b"][layer], tm=tm,
            state=sp.conv_state[layer], fill=(f1.reshape(ss.m, d_ff), f2.reshape(ss.m, d_ff)), seg=ss.t_rows)
        cs = [tail_p[:, SUBLANES - 2:],
              tail_s.reshape(ss.nb, ss.t_rows, d_ff)[:, ss.t_valid - 2:ss.t_valid]]
        ys = [matmul(a, w["ffn_w_down16"][layer], d, tm=tm, tn=512, tk=d_ff // 2, out_dtypes=[F32], res=x)
              for a, x in zip((a_p, a_s), xs)]
        return ys, cs

    xs = [sp.x, ss.x]
    hs = norm(xs, w["norm_mix"][0])
    (proj_p,), (proj_s,) = matmul_ws(hs[0], hs[1], w["ret_w_in"], w["ret_w_in"].shape[-1], tm=tm, tn=512,
                                     out_dtypes=[BF16])
    o_ret, s_ret = [], []
    for st, proj in zip(streams, (proj_p, proj_s)):
        chunk = math.gcd(st.t_valid, RET_CHUNK)
        cpad = chunk if st.ctx is None else st.t_rows
        cos_r, sin_r = _rope_tables_half(st.pos, dk // 2, RET_THETA)
        o, s = retention(proj, st.ret_s0, cos_r, sin_r, nb=st.nb, heads=heads_r, dk=dk, dv=dv,
                         chunk=cpad, c_valid=chunk)
        o_ret.append(o)
        s_ret.append(s)
    xs = [matmul(o, w["ret_w_out16"], d, tm=tm, tn=1024, tk=2048, out_dtypes=[F32], res=x)
          for o, x in zip(o_ret, xs)]
    xs, conv0 = ffn(xs, 0)

    hk = norm(xs, w["kv_norm"])
    rope_p = _nsa_rope_tables(sp.pos, hd)
    rope_s = tuple(jnp.tile(tb, (ss.nb, 1)) for tb in _nsa_rope_tables(ss.pos, hd))
    (kv32_p, kv16_p), (kv32_s, kv16_s) = matmul_ws(
        hk[0], hk[1], w["kv_w"], 6 * kvh * hd, tm=tm, tn=kvh * hd, out_dtypes=[F32, BF16],
        rope=(rope_p, rope_s), rope_every=2, rope_half=rope_half)
    kv32, kv16 = (kv32_p, kv32_s), (kv16_p, kv16_s)
    kv_rows = [kv[:, :4 * kvh * hd].reshape(st.nb, st.t_rows, 4, kvh, hd)[:, :st.t_valid]
               for st, kv in zip(streams, kv32)]

    hs = norm(xs, w["norm_mix"][1])
    (q_p,), (q_s,) = matmul_ws(hs[0], hs[1], w["nsa_w_in"], qw, tm=tm, tn=512, out_dtypes=[BF16],
                               rope=(rope_p, rope_s), rope_every=1, rope_half=rope_half)
    w_pairs = _cmp_w1_pairs(w["cmp_w1"], hd)
    pe_term = cmp_pe_term(w["cmp_pe"], w["cmp_w1"])
    os, wins = [], []
    for st, h, q, k32, k16 in zip(streams, hs, (q_p, q_s), kv32, kv16):
        gates = matmul(h, w["nsa_w_gate16"], 3 * NSA_HEADS, tm=tm, tn=3 * NSA_HEADS, tk=d, out_dtypes=[F32],
                       act="sigmoid")
        o_cmp, o_slc, o_win, win = _nsa_branches(st, q, k32, k16, w_pairs, pe_term, w, hd=hd, kvh=kvh, group=group)
        os.append(combine(o_cmp, o_slc, o_win, gates, NSA_HEADS, hd))
        wins.append(win)
    (x_p,), (x_s,) = matmul_ws(os[0], os[1], w["nsa_w_out"], d, tm=tm, tn=512, out_dtypes=[F32], res=xs)
    xs, conv1 = ffn([x_p, x_s], 1)
    ys = norm(xs, w["norm_final"], F32)
    convs = [jnp.stack([c0, c1]) for c0, c1 in zip(conv0, conv1)]
    return [(y, s[None], cv, kvr, win) for y, s, cv, kvr, win in zip(ys, s_ret, convs, kv_rows, wins)]


def kernel(x_prompt, x_sample, cache_kv, cache_win, state_ret, state_conv, page_table, norm_mix, norm_ffn,
           ret_w_in, ret_w_out, kv_norm, kv_w, cmp_pe, cmp_w1, cmp_w2, nsa_w_in, nsa_w_out, ffn_w_up, ffn_conv_w,
           ffn_conv_b, ffn_w_down, norm_final):
    b, t, d = x_prompt.shape
    db, dt, _ = x_sample.shape
    past = page_table.shape[1] * cache_kv.shape[1]
    qw = nsa_w_out.shape[1]
    w = dict(
        norm_mix=norm_mix, norm_ffn=norm_ffn, kv_norm=kv_norm, norm_final=norm_final,
        cmp_pe=cmp_pe, cmp_w1=cmp_w1, cmp_w2=cmp_w2, ffn_conv_w=ffn_conv_w, ffn_conv_b=ffn_conv_b,
        ret_w_in=ret_w_in[0], kv_w=kv_w, nsa_w_in=nsa_w_in[0], nsa_w_out=nsa_w_out[0], ffn_w_up=ffn_w_up,
        ret_w_out16=ret_w_out[0].astype(BF16), nsa_w_gate16=nsa_w_in[0][:, qw:].astype(BF16),
        ffn_w_down16=[ffn_w_down[l].astype(BF16) for l in range(ffn_w_down.shape[0])],
    )
    heads_r = RET_HEADS
    dk = d // heads_r
    ret0 = jnp.zeros((b, heads_r, dk, 2 * dk), F32)
    conv0 = jnp.zeros((state_conv.shape[0], b, 2, ffn_conv_w.shape[-1]), F32)
    tp = 2 * SUBLANES
    xs = jnp.zeros((db, tp, d), F32).at[:, :dt].set(x_sample).reshape(db * tp, d)
    sp = _Stream(x_prompt.reshape(b * t, d), b, t, t, 0, ret0, conv0, None)
    ss = _Stream(xs, db, tp, dt, past, state_ret[0], state_conv, (cache_kv, cache_win, page_table))
    (y_p, ret_p, conv_p, kv_p, win_p), (y_s, ret_s, conv_s, kv_s, win_s) = _trunk(sp, ss, w)
    y_s = y_s.reshape(db, tp, d)[:, :dt]
    return (y_p.reshape(b, t, d), y_s, kv_p, kv_s, win_p, win_s, ret_p, ret_s, conv_p, conv_s)
```

```python
import functools
import math

import jax
import jax.numpy as jnp
from jax import lax
from jax.experimental import pallas as pl
from jax.experimental.pallas import tpu as pltpu

F32 = jnp.float32
BF16 = jnp.bfloat16

RET_HEADS = 16
RET_CHUNK = 512
RET_THETA = 10000.0
NSA_HEADS = 32
NSA_KV_HEADS = 4
CMP_LEN = 32
CMP_STRIDE = 16
SLC_BLOCK = 64
N_SELECT = 16
WINDOW = 512
ROPE_THETA = 500000.0
PAGE_SIZE = 128
EPS = 1e-6
NEG = -1e30
FORCE = 1e9
TINY = 1e-20

LANES = 128
SUBLANES = 8
VMEM_LIMIT = 56 * 1024 * 1024


def _params(sem):
    return pltpu.CompilerParams(dimension_semantics=sem, vmem_limit_bytes=VMEM_LIMIT)


def _tile(dim, pref):
    if dim <= pref:
        return dim
    t = pref
    while dim % t:
        t //= 2
    return t


def _sigmoid(x):
    return 1.0 / (1.0 + jnp.exp(-x))


def _dot(a, b):
    return jnp.dot(a, b, preferred_element_type=F32)


def _dot_nt(a, b):
    return lax.dot_general(a, b, (((1,), (1,)), ((), ())), preferred_element_type=F32)


def _rmsnorm_kernel(x_ref, g_ref, o_ref):
    x = x_ref[...]
    y = x * lax.rsqrt(jnp.mean(x * x, axis=-1, keepdims=True) + EPS)
    o_ref[...] = (y * g_ref[...]).astype(o_ref.dtype)


def rmsnorm(x, g, out_dtype):
    m, d = x.shape
    tm = _tile(m, 256)
    return pl.pallas_call(
        _rmsnorm_kernel,
        grid=(m // tm,),
        in_specs=[pl.BlockSpec((tm, d), lambda i: (i, 0)),
                  pl.BlockSpec((1, d), lambda i: (0, 0))],
        out_specs=pl.BlockSpec((tm, d), lambda i: (i, 0)),
        out_shape=jax.ShapeDtypeStruct((m, d), out_dtype),
        compiler_params=_params(("parallel",)),
        name="rmsnorm",
    )(x, g.reshape(1, d).astype(F32))


def _rope_lanes(x, c, s_up, s_dn, half):
    return x * c + pltpu.roll(x, LANES - half, 1) * s_up + pltpu.roll(x, half, 1) * s_dn


def _mm_epilogue(acc, j, res_ref, rope_refs, out_refs, *, rope_every, rope_half, act):
    def store(val):
        for o in out_refs:
            o[...] = val.astype(o.dtype)

    if res_ref is not None:
        acc = acc + res_ref[...]
    if act == "sigmoid":
        acc = _sigmoid(acc)
    if not rope_every:
        store(acc)
        return
    tn = acc.shape[1]

    def roped():
        c, su, sd = (r[...] for r in rope_refs)
        pieces = [_rope_lanes(acc[:, a:a + LANES], c, su, sd, rope_half) for a in range(0, tn, LANES)]
        store(jnp.concatenate(pieces, axis=1))

    if rope_every == 1:
        roped()
    else:
        pl.when(j % rope_every == 0)(roped)
        pl.when(j % rope_every != 0)(lambda: store(acc))


def _mm_kernel(*refs, nk, has_res, rope_every, rope_half, act, n_out):
    x_ref, w_ref = refs[0], refs[1]
    pos = 2
    res_ref = None
    if has_res:
        res_ref = refs[pos]
        pos += 1
    rope_refs = None
    if rope_every:
        rope_refs = refs[pos:pos + 3]
        pos += 3
    out_refs = refs[pos:pos + n_out]
    acc_ref = refs[pos + n_out] if nk > 1 else None
    j = pl.program_id(1)
    k = pl.program_id(2)

    part = _dot(x_ref[...], w_ref[...])
    finish = functools.partial(_mm_epilogue, j=j, res_ref=res_ref, rope_refs=rope_refs, out_refs=out_refs,
                               rope_every=rope_every, rope_half=rope_half, act=act)

    if nk == 1:
        finish(part)
    else:
        @pl.when(k == 0)
        def _():
            acc_ref[...] = part

        @pl.when(k > 0)
        def _():
            acc_ref[...] += part

        @pl.when(k == nk - 1)
        def _():
            finish(acc_ref[...])


def matmul(x, w, n, *, tm, tn, tk, out_dtypes, res=None, rope=None, rope_every=0, rope_half=0, act=None):
    m, kdim = x.shape
    tm, tn, tk = _tile(m, tm), _tile(n, tn), _tile(kdim, tk)
    nk = kdim // tk
    in_specs = [pl.BlockSpec((tm, tk), lambda i, j, k: (i, k)),
                pl.BlockSpec((tk, tn), lambda i, j, k: (k, j))]
    args = [x, w]
    if res is not None:
        in_specs.append(pl.BlockSpec((tm, tn), lambda i, j, k: (i, j)))
        args.append(res)
    if rope is not None:
        period = rope[0].shape[0]
        nper = period // tm
        for t in rope:
            in_specs.append(pl.BlockSpec((tm, LANES), lambda i, j, k: (i % nper, 0)))
            args.append(t)
    out_specs = [pl.BlockSpec((tm, tn), lambda i, j, k: (i, j)) for _ in out_dtypes]
    out_shape = [jax.ShapeDtypeStruct((m, n), dt) for dt in out_dtypes]
    scratch = [pltpu.VMEM((tm, tn), F32)] if nk > 1 else []
    outs = pl.pallas_call(
        functools.partial(_mm_kernel, nk=nk, has_res=res is not None,
                          rope_every=rope_every if rope is not None else 0, rope_half=rope_half,
                          act=act, n_out=len(out_dtypes)),
        grid=(m // tm, n // tn, nk),
        in_specs=in_specs, out_specs=out_specs, out_shape=out_shape,
        scratch_shapes=scratch,
        compiler_params=_params(("parallel", "parallel", "arbitrary")),
        name="matmul",
    )(*args)
    return outs if len(outs) > 1 else outs[0]


def _round_rows_job(j, i, src_ref, dst_ref, nblk):
    @pl.when((i == 0) & (j < nblk))
    def _():
        dst_ref[...] = src_ref[...].astype(BF16)


def _round_rows_specs(src, nblk, layer=None):
    rows, cols = src.shape[-2:]
    rb = rows // nblk
    if layer is None:
        in_spec = pl.BlockSpec((rb, cols), lambda j, i: (jnp.minimum(j, nblk - 1), 0))
    else:
        in_spec = pl.BlockSpec((None, rb, cols), lambda j, i: (layer, jnp.minimum(j, nblk - 1), 0))
    out_spec = pl.BlockSpec((rb, cols), lambda j, i: (jnp.minimum(j, nblk - 1), 0))
    return in_spec, out_spec, jax.ShapeDtypeStruct((rows, cols), BF16)


def _mm_ws_kernel(*refs, has_res, rope_every, rope_half, n_out, round_blocks):
    x_ref, xs_ref, w_ref = refs[:3]
    pos = 3
    if round_blocks:
        rsrc_ref = refs[pos]
        pos += 1
    res_ref = ress_ref = rope_refs = ropes_refs = None
    if has_res:
        res_ref, ress_ref = refs[pos:pos + 2]
        pos += 2
    if rope_every:
        rope_refs, ropes_refs = refs[pos:pos + 3], refs[pos + 3:pos + 6]
        pos += 6
    out_refs = refs[pos:pos + n_out]
    outs_refs = refs[pos + n_out:pos + 2 * n_out]
    pos += 2 * n_out
    j = pl.program_id(0)
    i = pl.program_id(1)
    if round_blocks:
        _round_rows_job(j, i, rsrc_ref, refs[pos], round_blocks)
        pos += 1
    w16 = refs[pos]

    @pl.when(i == 0)
    def _():
        w16[...] = w_ref[...].astype(BF16)

    kw = dict(rope_every=rope_every, rope_half=rope_half, act=None)
    _mm_epilogue(_dot(x_ref[...], w16[...]), j, res_ref, rope_refs, out_refs, **kw)

    @pl.when(i == pl.num_programs(1) - 1)
    def _():
        _mm_epilogue(_dot(xs_ref[...], w16[...]), j, ress_ref, ropes_refs, outs_refs, **kw)


def matmul_ws(x, xs, w, n, *, tm, tn, out_dtypes, res=None, rope=None, rope_every=0, rope_half=0,
              round_rows=None):
    m, kdim = x.shape
    ms = xs.shape[0]
    tm, tn = _tile(m, tm), _tile(n, tn)
    in_specs = [pl.BlockSpec((tm, kdim), lambda j, i: (i, 0)),
                pl.BlockSpec((ms, kdim), lambda j, i: (0, 0)),
                pl.BlockSpec((kdim, tn), lambda j, i: (0, j))]
    args = [x, xs, w]
    round_blocks = 0
    if round_rows is not None:
        round_blocks = round_rows[1]
        assert round_blocks <= n // tn
        r_in, r_out, r_shape = _round_rows_specs(*round_rows)
        in_specs.append(r_in)
        args.append(round_rows[0])
    if res is not None:
        in_specs += [pl.BlockSpec((tm, tn), lambda j, i: (i, j)), pl.BlockSpec((ms, tn), lambda j, i: (0, j))]
        args += list(res)
    if rope is not None:
        nper = rope[0][0].shape[0] // tm
        in_specs += [pl.BlockSpec((tm, LANES), lambda j, i: (i % nper, 0))] * 3
        in_specs += [pl.BlockSpec((ms, LANES), lambda j, i: (0, 0))] * 3
        args += list(rope[0]) + list(rope[1])
    out_specs = ([pl.BlockSpec((tm, tn), lambda j, i: (i, j)) for _ in out_dtypes]
                 + [pl.BlockSpec((ms, tn), lambda j, i: (0, j)) for _ in out_dtypes])
    out_shape = ([jax.ShapeDtypeStruct((m, n), dt) for dt in out_dtypes]
                 + [jax.ShapeDtypeStruct((ms, n), dt) for dt in out_dtypes])
    if round_blocks:
        out_specs.append(r_out)
        out_shape.append(r_shape)
    outs = pl.pallas_call(
        functools.partial(_mm_ws_kernel, has_res=res is not None,
                          rope_every=rope_every if rope is not None else 0, rope_half=rope_half,
                          n_out=len(out_dtypes), round_blocks=round_blocks),
        grid=(n // tn, m // tm),
        in_specs=in_specs, out_specs=out_specs, out_shape=out_shape,
        scratch_shapes=[pltpu.VMEM((kdim, tn), BF16)],
        compiler_params=_params(("arbitrary" if round_blocks else "parallel", "arbitrary")),
        name="matmul_ws",
    )(*args)
    k = len(out_dtypes)
    if round_blocks:
        return outs[:k], outs[k:2 * k], outs[2 * k]
    return outs[:k], outs[k:]


def _ffn_up_kernel(x_ref, xs_ref, wv_ref, wg_ref, cw_ref, cb_ref, st_ref, f1_ref, f2_ref, wd_ref,
                   a_ref, tail_ref, as_ref, tails_ref, wd16_ref, w16, carry_ref, *, tiles_per_seq, seg):
    i = pl.program_id(1)
    tn = wv_ref.shape[1]
    _round_rows_job(pl.program_id(0), i, wd_ref, wd16_ref, pl.num_programs(0))

    @pl.when(i == 0)
    def _():
        w16[:, :tn] = wv_ref[...].astype(BF16)
        w16[:, tn:] = wg_ref[...].astype(BF16)

    def val_gate(x):
        r = _dot(x, w16[...])
        return r[:, :tn], r[:, tn:]

    cb = cb_ref[...]
    cw0, cw1, cw2 = cw_ref[0:1, :], cw_ref[1:2, :], cw_ref[2:3, :]

    def gated(val, gate, g1, g2):
        conv = cb + cw0 * g2 + cw1 * g1 + cw2 * gate
        return (val * (conv * _sigmoid(conv))).astype(BF16)

    @pl.when(i % tiles_per_seq == 0)
    def _():
        carry_ref[...] = st_ref[0]

    tm = x_ref.shape[0]
    row = lax.broadcasted_iota(jnp.int32, (SUBLANES, 1), 0)
    val, gate = val_gate(x_ref[...])
    c0 = carry_ref[0:1, :]
    c1 = carry_ref[1:2, :]
    g1 = pltpu.roll(gate, 1, 0)
    g2 = pltpu.roll(gate, 2, 0)
    g1 = jnp.concatenate([jnp.where(row == 0, c1, g1[:SUBLANES]), g1[SUBLANES:]], axis=0)
    g2 = jnp.concatenate([jnp.where(row == 0, c0, jnp.where(row == 1, c1, g2[:SUBLANES])), g2[SUBLANES:]], axis=0)
    a_ref[...] = gated(val, gate, g1, g2)
    carry_ref[...] = gate[tm - 2:tm, :]
    nt = tail_ref.shape[1]
    tail_ref[0] = gate[tm - nt:tm, :]

    @pl.when(i == pl.num_programs(1) - 1)
    def _():
        val, gate = val_gate(xs_ref[...])
        t = jnp.bitwise_and(lax.broadcasted_iota(jnp.int32, (val.shape[0], 1), 0), seg - 1)
        g1 = jnp.where(t == 0, f1_ref[...], pltpu.roll(gate, 1, 0))
        g2 = jnp.where(t < 2, f2_ref[...], pltpu.roll(gate, 2, 0))
        as_ref[...] = gated(val, gate, g1, g2)
        tails_ref[...] = gate


def ffn_up(h, hs, w_up, w_down, layer, d_ff, conv_w, conv_b, *, tm, state, fill, seg):
    m, kdim = h.shape
    ms = hs.shape[0]
    assert seg & (seg - 1) == 0
    tn = _tile(d_ff, 256)
    nj = d_ff // tn
    tm = _tile(m, tm)
    nseq = state.shape[0]
    tps = (m // nseq) // tm
    wd_in, wd_out, wd_shape = _round_rows_specs(w_down, nj, layer)
    a, tail, a_s, tail_s, wd16 = pl.pallas_call(
        functools.partial(_ffn_up_kernel, tiles_per_seq=tps, seg=seg),
        grid=(nj, m // tm),
        in_specs=[pl.BlockSpec((tm, kdim), lambda j, i: (i, 0)),
                  pl.BlockSpec((ms, kdim), lambda j, i: (0, 0)),
                  pl.BlockSpec((None, kdim, tn), lambda j, i: (layer, 0, j)),
                  pl.BlockSpec((None, kdim, tn), lambda j, i: (layer, 0, nj + j)),
                  pl.BlockSpec((3, tn), lambda j, i: (0, j)),
                  pl.BlockSpec((1, tn), lambda j, i: (0, j)),
                  pl.BlockSpec((1, 2, tn), lambda j, i: (i // tps, 0, j)),
                  pl.BlockSpec((ms, tn), lambda j, i: (0, j)),
                  pl.BlockSpec((ms, tn), lambda j, i: (0, j)),
                  wd_in],
        out_specs=[pl.BlockSpec((tm, tn), lambda j, i: (i, j)),
                   pl.BlockSpec((1, SUBLANES, tn), lambda j, i: (i // tps, 0, j)),
                   pl.BlockSpec((ms, tn), lambda j, i: (0, j)),
                   pl.BlockSpec((ms, tn), lambda j, i: (0, j)),
                   wd_out],
        out_shape=[jax.ShapeDtypeStruct((m, d_ff), BF16),
                   jax.ShapeDtypeStruct((nseq, SUBLANES, d_ff), F32),
                   jax.ShapeDtypeStruct((ms, d_ff), BF16),
                   jax.ShapeDtypeStruct((ms, d_ff), F32),
                   wd_shape],
        scratch_shapes=[pltpu.VMEM((kdim, 2 * tn), BF16), pltpu.VMEM((2, tn), F32)],
        compiler_params=_params(("parallel", "arbitrary")),
        name="ffn_up",
    )(h, hs, w_up, w_up, conv_w, conv_b.reshape(1, d_ff), state, fill[0], fill[1], w_down)
    return (a, tail), (a_s, tail_s), wd16


def _ret_kernel(lg_ref, q_ref, k_ref, v_ref, g_ref, cos_ref, sin_ref, s0_ref,
                o_ref, s_out_ref, s_scr, intra_scr, *, c_valid, dk):
    h = pl.program_id(1)
    c = pl.program_id(2)
    nc = pl.num_programs(2)
    half = dk // 2
    lg = lg_ref[h]
    cp = q_ref.shape[0]

    @pl.when(c == 0)
    def _():
        s_scr[...] = s0_ref[0, 0]
        di = lax.broadcasted_iota(jnp.int32, (cp, cp), 0)
        dj = lax.broadcasted_iota(jnp.int32, (cp, cp), 1)
        diff = di - dj
        intra_scr[...] = jnp.where((diff >= 0) & (dj < c_valid),
                                   jnp.exp(lg * jnp.maximum(diff, 0).astype(F32)), 0.0)

    cos = cos_ref[...]
    sin = sin_ref[...]

    def rope(x):
        x1 = x[:, :half]
        x2 = x[:, half:]
        return jnp.concatenate([x1 * cos - x2 * sin, x2 * cos + x1 * sin], axis=1)

    q = rope(q_ref[...].astype(F32))
    k = rope(k_ref[...].astype(F32)) * (dk ** -0.5)
    v = v_ref[...]
    ri = lax.broadcasted_iota(jnp.int32, (cp, 1), 0)
    rif = ri.astype(F32)
    read_decay = jnp.exp(lg * (rif + 1.0))
    write_decay = jnp.where(ri < c_valid, jnp.exp(lg * jnp.maximum(c_valid - 1.0 - rif, 0.0)), 0.0)

    s = s_scr[...]
    att = _dot_nt(q.astype(BF16), k.astype(BF16)) * intra_scr[...]
    o = _dot(att.astype(BF16), v) + _dot((q * read_decay).astype(BF16), s.astype(BF16))
    kw = (k * write_decay).astype(BF16)
    chunk_decay = jnp.exp(lg * jnp.full((1, 1), float(c_valid), F32))
    s_new = s * chunk_decay + lax.dot_general(kw, v, (((0,), (0,)), ((), ())), preferred_element_type=F32)
    s_scr[...] = s_new

    @pl.when(c == nc - 1)
    def _():
        s_out_ref[0, 0] = s_new

    on = o * lax.rsqrt(jnp.mean(o * o, axis=-1, keepdims=True) + EPS)
    gate = g_ref[...].astype(F32)
    o_ref[...] = (on * (gate * _sigmoid(gate))).astype(o_ref.dtype)


def retention(proj, s0, cos, sin, *, nb, heads, dk, dv, chunk, c_valid):
    m = proj.shape[0]
    t = m // nb
    nc = t // chunk
    qb = heads
    vb = 2 * heads * dk // dv
    log_g = jnp.log1p(-jnp.exp2(-5.0 - jnp.arange(heads, dtype=F32)))
    grid_spec = pltpu.PrefetchScalarGridSpec(
        num_scalar_prefetch=1,
        grid=(nb, heads, nc),
        in_specs=[
            pl.BlockSpec((chunk, dk), lambda b, h, c, lg: (b * nc + c, h)),
            pl.BlockSpec((chunk, dk), lambda b, h, c, lg: (b * nc + c, qb + h)),
            pl.BlockSpec((chunk, dv), lambda b, h, c, lg: (b * nc + c, vb + h)),
            pl.BlockSpec((chunk, dv), lambda b, h, c, lg: (b * nc + c, vb + heads + h)),
            pl.BlockSpec((chunk, dk // 2), lambda b, h, c, lg: (c, 0)),
            pl.BlockSpec((chunk, dk // 2), lambda b, h, c, lg: (c, 0)),
            pl.BlockSpec((1, 1, dk, dv), lambda b, h, c, lg: (b, h, 0, 0)),
        ],
        out_specs=[
            pl.BlockSpec((chunk, dv), lambda b, h, c, lg: (b * nc + c, h)),
            pl.BlockSpec((1, 1, dk, dv), lambda b, h, c, lg: (b, h, 0, 0)),
        ],
        scratch_shapes=[pltpu.VMEM((dk, dv), F32), pltpu.VMEM((chunk, chunk), F32)],
    )
    o, s = pl.pallas_call(
        functools.partial(_ret_kernel, c_valid=c_valid, dk=dk),
        grid_spec=grid_spec,
        out_shape=[jax.ShapeDtypeStruct((m, heads * dv), BF16),
                   jax.ShapeDtypeStruct((nb, heads, dk, dv), F32)],
        compiler_params=_params(("parallel", "parallel", "arbitrary")),
        name="retention",
    )(log_g, proj, proj, proj, proj, cos, sin, s0)
    return o, s


def _cmp_project(planes, w_ref, p_ref, kvh, hd):
    n_seg = planes.shape[1] // CMP_STRIDE
    for br in range(2):
        acc = jnp.zeros((kvh * n_seg, 2 * hd), F32)
        for sp in range(CMP_STRIDE // 2):
            pieces = []
            for g in range(kvh):
                plane = planes.at[br * kvh + g]
                a = plane[pl.ds(2 * sp, n_seg, stride=CMP_STRIDE), :]
                b = plane[pl.ds(2 * sp + 1, n_seg, stride=CMP_STRIDE), :]
                pieces.append(jnp.concatenate([a, b], axis=1))
            lhs = jnp.concatenate(pieces, axis=0).astype(BF16)
            acc = acc + _dot(lhs, w_ref[br, sp])
        for g in range(kvh):
            p_ref[0, br * kvh + g] = acc[g * n_seg:(g + 1) * n_seg, :]


def _cmp_proj_rows_kernel(x_ref, w_ref, p_ref, r_ref, *, kvh, hd):
    for hh in range(2 * kvh):
        r_ref[hh] = x_ref[:, hh * hd:(hh + 1) * hd]
    _cmp_project(r_ref, w_ref, p_ref, kvh, hd)


def _cmp_proj_pages_kernel(pt_ref, cache_ref, w_ref, p_ref, r_ref, sem, *, kvh, hd, pages_per_step, n_pages):
    b = pl.program_id(0)
    c = pl.program_id(1)
    nchunk = pl.num_programs(1)
    step = b * nchunk + c
    nsteps = pl.num_programs(0) * nchunk
    page = cache_ref.shape[1]

    def copies(st, slot):
        bb = st // nchunk
        cc = st % nchunk
        out = []
        for u in range(pages_per_step):
            phys = pt_ref[bb * n_pages + cc * pages_per_step + u]
            for hh in range(2 * kvh):
                out.append(pltpu.make_async_copy(
                    cache_ref.at[phys, :, hh // kvh, hh % kvh, :],
                    r_ref.at[slot, hh, pl.ds(u * page, page), :],
                    sem.at[slot]))
        return out

    slot = step % 2

    @pl.when(step == 0)
    def _():
        for cp in copies(step, slot):
            cp.start()

    @pl.when(step + 1 < nsteps)
    def _():
        for cp in copies(step + 1, 1 - slot):
            cp.start()

    for cp in copies(step, slot):
        cp.wait()
    _cmp_project(r_ref.at[slot], w_ref, p_ref, kvh, hd)


def _cmp_w1_pairs(cmp_w1, hd):
    r = CMP_LEN // CMP_STRIDE
    e = cmp_w1.shape[-1]
    w = cmp_w1.reshape(2, r, CMP_STRIDE // 2, 2, hd, e)
    w = w.transpose(0, 2, 3, 4, 1, 5)
    return w.reshape(2, CMP_STRIDE // 2, 2 * hd, r * e).astype(BF16)


def cmp_proj_rows(rows2d, nb, t, w_pairs, kvh, hd):
    n_seg = t // CMP_STRIDE
    width = 2 * kvh * hd
    return pl.pallas_call(
        functools.partial(_cmp_proj_rows_kernel, kvh=kvh, hd=hd),
        grid=(nb,),
        in_specs=[pl.BlockSpec((t, width), lambda b: (b, 0)),
                  pl.BlockSpec(w_pairs.shape, lambda b: (0, 0, 0, 0))],
        out_specs=pl.BlockSpec((1, 2 * kvh, n_seg, 2 * hd), lambda b: (b, 0, 0, 0)),
        out_shape=jax.ShapeDtypeStruct((nb, 2 * kvh, n_seg, 2 * hd), F32),
        scratch_shapes=[pltpu.VMEM((2 * kvh, t, hd), F32)],
        compiler_params=_params(("parallel",)),
        name="cmp_proj_prompt",
    )(rows2d, w_pairs)


def cmp_proj_pages(cache, page_table, w_pairs, kvh, hd, pages_per_step):
    nb, n_pages = page_table.shape
    page = cache.shape[1]
    n_seg = pages_per_step * page // CMP_STRIDE
    nchunk = n_pages // pages_per_step
    grid_spec = pltpu.PrefetchScalarGridSpec(
        num_scalar_prefetch=1,
        grid=(nb, nchunk),
        in_specs=[pl.BlockSpec(memory_space=pl.ANY),
                  pl.BlockSpec(w_pairs.shape, lambda b, c, pt: (0, 0, 0, 0))],
        out_specs=pl.BlockSpec((1, 2 * kvh, n_seg, 2 * hd), lambda b, c, pt: (b, 0, c, 0)),
        scratch_shapes=[pltpu.VMEM((2, 2 * kvh, pages_per_step * page, hd), F32),
                        pltpu.SemaphoreType.DMA((2,))],
    )
    return pl.pallas_call(
        functools.partial(_cmp_proj_pages_kernel, kvh=kvh, hd=hd, pages_per_step=pages_per_step,
                          n_pages=n_pages),
        grid_spec=grid_spec,
        out_shape=jax.ShapeDtypeStruct((nb, 2 * kvh, nchunk * n_seg, 2 * hd), F32),
        compiler_params=_params(("arbitrary", "arbitrary")),
        name="cmp_proj_sample",
    )(page_table.reshape(-1), cache, w_pairs)


def _cmp_pe_kernel(pe_ref, w_ref, o_ref):
    o_ref[0] = _dot(pe_ref[0], w_ref[0])


def cmp_pe_term(cmp_pe, cmp_w1):
    _, n, hd = cmp_pe.shape
    e = cmp_w1.shape[-1]
    pe_flat = jnp.zeros((2, SUBLANES, n * hd), BF16).at[:, 0].set(cmp_pe.reshape(2, n * hd).astype(BF16))
    w_flat = cmp_w1.reshape(2, n * hd, e).astype(BF16)
    return pl.pallas_call(
        _cmp_pe_kernel,
        grid=(2,),
        in_specs=[pl.BlockSpec((1, SUBLANES, n * hd), lambda b: (b, 0, 0)),
                  pl.BlockSpec((1, n * hd, e), lambda b: (b, 0, 0))],
        out_specs=pl.BlockSpec((1, SUBLANES, e), lambda b: (b, 0, 0)),
        out_shape=jax.ShapeDtypeStruct((2, SUBLANES, e), F32),
        compiler_params=_params(("parallel",)),
        name="cmp_pe",
    )(pe_flat, w_flat)


def _cmp_finish_kernel(p_ref, pe_ref, w2_ref, o_ref):
    p = p_ref[0, 0]
    n_seg = p.shape[0]
    e = p.shape[1] // 2
    hid = p[:, :e] + pltpu.roll(p[:, e:], n_seg - 1, 0) + pe_ref[0, 0:1, :]
    c = math.sqrt(2.0 / math.pi)
    act = 0.5 * hid * (1.0 + jnp.tanh(c * (hid + 0.044715 * (hid * hid * hid))))
    o_ref[0, 0] = _dot(act.astype(BF16), w2_ref[0]).astype(o_ref.dtype)


def cmp_finish(p, pe_term, cmp_w2, kvh):
    nb, nu, n_seg, e2 = p.shape
    e = e2 // 2
    hd = cmp_w2.shape[-1]
    return pl.pallas_call(
        _cmp_finish_kernel,
        grid=(nb, nu),
        in_specs=[pl.BlockSpec((1, 1, n_seg, e2), lambda b, u: (b, u, 0, 0)),
                  pl.BlockSpec((1, SUBLANES, e), lambda b, u: (u // kvh, 0, 0)),
                  pl.BlockSpec((1, e, hd), lambda b, u: (u // kvh, 0, 0))],
        out_specs=pl.BlockSpec((1, 1, n_seg, hd), lambda b, u: (b, u, 0, 0)),
        out_shape=jax.ShapeDtypeStruct((nb, nu, n_seg, hd), BF16),
        compiler_params=_params(("parallel", "parallel")),
        name="cmp_finish",
    )(p, pe_term, cmp_w2.astype(BF16))


def _cmp_attn_kernel(q_ref, kc_ref, vc_ref, agg_ref, o_ref, imp_ref, *, group, hd, n_c, pos0):
    qi = pl.program_id(2)
    tq = q_ref.shape[0]
    kc = kc_ref[0, 0]
    vc = vc_ref[0, 0]
    agg = agg_ref[...]
    n_cp = kc.shape[0]
    qs = jnp.concatenate([q_ref[:, h * hd:(h + 1) * hd] for h in range(group)], axis=0)
    qpos1 = pos0 + qi * tq + lax.broadcasted_iota(jnp.int32, (tq, 1), 0)
    qpos = jnp.concatenate([qpos1] * group, axis=0)
    ci = lax.broadcasted_iota(jnp.int32, (1, n_cp), 1)
    cm = ((ci * CMP_STRIDE + (CMP_LEN - 1)) <= qpos) & (ci < n_c)
    s = jnp.where(cm, _dot_nt(qs, kc) * (hd ** -0.5), NEG)
    e = jnp.where(cm, jnp.exp(s - jnp.max(s, axis=-1, keepdims=True)), 0.0)
    p = e / jnp.maximum(jnp.sum(e, axis=-1, keepdims=True), TINY)
    pb = p.astype(BF16)
    o = _dot(pb, vc)
    psum = jnp.zeros((tq, n_cp), F32)
    for h in range(group):
        o_ref[:, h * hd:(h + 1) * hd] = o[h * tq:(h + 1) * tq, :].astype(o_ref.dtype)
        psum = psum + pb[h * tq:(h + 1) * tq, :].astype(F32)
    hi = psum.astype(BF16)
    lo = (psum - hi.astype(F32)).astype(BF16)
    imp_ref[0, 0] = _dot_nt(agg, hi) + _dot_nt(agg, lo)


def _slc_aggregation_t(n_cp, n_slc, rows):
    rs = SLC_BLOCK // CMP_STRIDE
    rc = CMP_LEN // CMP_STRIDE
    i = jnp.arange(n_cp)[None, :]
    j = jnp.arange(rows)[:, None]
    w = sum((i == j * rs + m - n).astype(F32) for m in range(rs) for n in range(rc))
    return jnp.where(j < n_slc, w, 0.0).astype(BF16)


def cmp_attn(q, kvc, *, nb, kvh, group, hd, n_c, n_slc, pos0, tq):
    m = q.shape[0]
    t = m // nb
    tq = _tile(t, tq)
    nq = t // tq
    n_cp = kvc.shape[2]
    rows = -(-n_slc // SUBLANES) * SUBLANES
    agg_t = _slc_aggregation_t(n_cp, n_slc, rows)
    gw = group * hd
    return pl.pallas_call(
        functools.partial(_cmp_attn_kernel, group=group, hd=hd, n_c=n_c, pos0=pos0),
        grid=(nb, kvh, nq),
        in_specs=[pl.BlockSpec((tq, gw), lambda b, g, i: (b * nq + i, g)),
                  pl.BlockSpec((1, 1, n_cp, hd), lambda b, g, i: (b, g, 0, 0)),
                  pl.BlockSpec((1, 1, n_cp, hd), lambda b, g, i: (b, kvh + g, 0, 0)),
                  pl.BlockSpec((rows, n_cp), lambda b, g, i: (0, 0))],
        out_specs=[pl.BlockSpec((tq, gw), lambda b, g, i: (b * nq + i, g)),
                   pl.BlockSpec((1, 1, rows, tq), lambda b, g, i: (b, g, 0, i))],
        out_shape=[jax.ShapeDtypeStruct((m, kvh * gw), BF16),
                   jax.ShapeDtypeStruct((nb, kvh, rows, t), F32)],
        compiler_params=_params(("parallel", "parallel", "parallel")),
        name="cmp_attn",
    )(q, kvc, kvc, agg_t)


def _select_kernel(imp_ref, *rest, n_slc, pos0, t_len, emit_idx):
    if emit_idx:
        idx_ref, ok_ref, sc_ref = rest
    else:
        bias_ref, sc_ref = rest
    rows, width = imp_ref.shape
    lane = pl.program_id(0) * width + lax.broadcasted_iota(jnp.int32, (1, width), 1)
    qpos = pos0 + jnp.bitwise_and(lane, t_len - 1)
    blk = lax.broadcasted_iota(jnp.int32, (rows, width), 0)
    cur = jnp.right_shift(qpos, SLC_BLOCK.bit_length() - 1)
    visible = blk * SLC_BLOCK <= qpos
    forced = (blk == 0) | (blk == cur) | (blk == cur - 1)
    score = jnp.where(visible, jnp.where(forced, FORCE, imp_ref[...]), NEG)
    sc_ref[...] = score

    def count(i, cnt):
        si = sc_ref[pl.ds(i, 1), :]
        beats = (si > score) | ((si == score) & (i < blk))
        return cnt + jnp.where(beats, 1, 0)

    rank = lax.fori_loop(0, n_slc, count, jnp.zeros((rows, width), jnp.int32))
    in_range = blk < n_slc
    if emit_idx:
        blk_f = blk.astype(F32)
        for r in range(idx_ref.shape[0]):
            hit = (rank == r) & in_range
            idx_r = jnp.sum(jnp.where(hit, blk_f, 0.0), axis=0, keepdims=True)
            idx_ref[r:r + 1, :] = idx_r.astype(jnp.int32)
            ok_ref[r:r + 1, :] = jnp.sum(jnp.where(hit & visible, 1.0, 0.0), axis=0, keepdims=True)
    else:
        chosen = (rank < N_SELECT) & visible & in_range
        bias_ref[...] = jnp.where(chosen, 0.0, NEG).astype(bias_ref.dtype)


def select_blocks(imp, *, n_slc, pos0, emit_idx):
    nb, kvh, rows, t = imp.shape
    assert t & (t - 1) == 0
    lanes = nb * kvh * t
    imp2 = imp.transpose(2, 0, 1, 3).reshape(rows, lanes)
    width = _tile(lanes, 2048)
    spec = pl.BlockSpec((rows, width), lambda i: (0, i))
    if emit_idx:
        nsel = min(N_SELECT, n_slc)
        o_spec = pl.BlockSpec((nsel, width), lambda i: (0, i))
        out_specs = [o_spec, o_spec]
        out_shape = [jax.ShapeDtypeStruct((nsel, lanes), jnp.int32), jax.ShapeDtypeStruct((nsel, lanes), F32)]
    else:
        out_specs = [spec]
        out_shape = [jax.ShapeDtypeStruct((rows, lanes), BF16)]
    outs = pl.pallas_call(
        functools.partial(_select_kernel, n_slc=n_slc, pos0=pos0, t_len=t, emit_idx=emit_idx),
        grid=(lanes // width,),
        in_specs=[spec], out_specs=out_specs, out_shape=out_shape,
        scratch_shapes=[pltpu.VMEM((rows, width), F32)],
        compiler_params=_params(("parallel",)),
        name="select_blocks",
    )(imp2)
    if emit_idx:
        return tuple(o.reshape(-1, nb, kvh, t).transpose(1, 2, 0, 3) for o in outs)
    return outs[0].reshape(rows, nb, kvh, t).transpose(1, 2, 3, 0)


V_ROWS_PAD = 2 * SUBLANES


def _prompt_attn_kernel(*refs, mode, nq, hd):
    if mode == "slc":
        q_ref, bias_ref, k_ref, vt_ref, o_ref = refs
    else:
        q_ref, k_ref, vt_ref, o_ref = refs
    qi = pl.program_id(2)
    tq = q_ref.shape[0]
    n_heads = q_ref.shape[1] // hd
    kl = lax.broadcasted_iota(jnp.int32, (tq, tq), 0)
    ql = lax.broadcasted_iota(jnp.int32, (tq, tq), 1)

    def run_head(hp, k0, nkeys, band):
        q = (q_ref[:, hp * hd:(hp + 1) * hd].astype(F32) * (hd ** -0.5 * math.log2(math.e))).astype(BF16)
        if mode == "slc":
            q = jnp.concatenate([q, bias_ref[0, 0]], axis=1)
        st = _dot_nt(k_ref[pl.ds(k0, nkeys), :], q)
        parts = []
        if nkeys > tq:
            top = st[:nkeys - tq, :]
            if band:
                top = jnp.where(kl > ql, top, NEG)
            parts.append(top)
        parts.append(jnp.where(kl <= ql, st[nkeys - tq:, :], NEG))
        st = jnp.concatenate(parts, axis=0) if len(parts) > 1 else parts[0]
        p = jnp.exp2(st - jnp.max(st, axis=0, keepdims=True)).astype(BF16)
        ot = _dot(vt_ref[0, 0, :, pl.ds(k0, nkeys)], p)
        o_ref[:, hp * hd:(hp + 1) * hd] = (ot[:hd, :] / ot[hd:hd + 1, :]).T.astype(o_ref.dtype)

    def run(k0, nkeys, band):
        for hp in range(n_heads):
            run_head(hp, k0, nkeys, band)

    if mode == "slc":
        for v in range(nq):
            pl.when(qi == v)(functools.partial(run, 0, (v + 1) * tq, False))
    else:
        pl.when(qi == 0)(functools.partial(run, 0, tq, False))
        if nq > 1:
            pl.when(qi > 0)(lambda: run(pl.multiple_of((qi - 1) * tq, tq), 2 * tq, True))


def prompt_attn(q, k_rows, vt, bias, *, mode, nb, kvh, group, hd, tq, heads_per_step=4):
    m = q.shape[0]
    t = m // nb
    tq = _tile(t, tq)
    nq = t // tq
    kw = k_rows.shape[1] // kvh
    assert mode == "slc" or tq == WINDOW or nq == 1
    hw = heads_per_step * hd
    hsteps = group // heads_per_step
    in_specs = [pl.BlockSpec((tq, hw), lambda b, g, i, h: (b * nq + i, g * hsteps + h))]
    args = [q]
    if mode == "slc":
        in_specs.append(pl.BlockSpec((1, 1, tq, hd), lambda b, g, i, h: (b, g, i, 0)))
        args.append(bias)
    in_specs += [pl.BlockSpec((t, kw), lambda b, g, i, h: (b, g)),
                 pl.BlockSpec((1, 1, hd + V_ROWS_PAD, t), lambda b, g, i, h: (b, g, 0, 0))]
    args += [k_rows, vt]
    return pl.pallas_call(
        functools.partial(_prompt_attn_kernel, mode=mode, nq=nq, hd=hd),
        grid=(nb, kvh, nq, hsteps),
        in_specs=in_specs,
        out_specs=pl.BlockSpec((tq, hw), lambda b, g, i, h: (b * nq + i, g * hsteps + h)),
        out_shape=jax.ShapeDtypeStruct((m, kvh * group * hd), BF16),
        compiler_params=_params(("parallel", "parallel", "parallel", "parallel")),
        name="prompt_attn_" + mode,
    )(*args)


def _sample_attn_kernel(idx_ref, pt_ref, q_ref, kpos_ref, okx_ref, kn_ref, vn_ref, wk_ref, wv_ref,
                        wkn_ref, wvn_ref, cache_ref, oslc_ref, owin_ref, kbuf, vbuf, sem,
                        *, kvh, group, hd, t_valid, past, nsel, ks_sec, vs_sec):
    b = pl.program_id(0)
    g = pl.program_id(1)
    tp = q_ref.shape[2] // group
    n_blocks = past // SLC_BLOCK
    per_page = PAGE_SIZE // SLC_BLOCK
    n_fetch = t_valid * nsel

    def copies(f):
        t = f // nsel
        r = f % nsel
        blk = idx_ref[((b * kvh + g) * nsel + r) * tp + t]
        blk = jnp.minimum(blk, n_blocks - 1)
        page = pt_ref[b * (past // PAGE_SIZE) + blk // per_page]
        row0 = (blk % per_page) * SLC_BLOCK
        dst = pl.ds(f * SLC_BLOCK, SLC_BLOCK)
        ck = pltpu.make_async_copy(cache_ref.at[page, pl.ds(row0, SLC_BLOCK), ks_sec, g, :],
                                   kbuf.at[dst, :], sem.at[0])
        cv = pltpu.make_async_copy(cache_ref.at[page, pl.ds(row0, SLC_BLOCK), vs_sec, g, :],
                                   vbuf.at[dst, :], sem.at[1])
        return ck, cv

    def start(f, c):
        ck, cv = copies(f)
        ck.start()
        cv.start()
        return c

    def wait(f, c):
        ck, cv = copies(f)
        ck.wait()
        cv.wait()
        return c

    lax.fori_loop(0, n_fetch, start, 0)

    scale = hd ** -0.5
    qa = q_ref[0, 0].astype(BF16)
    trow = jnp.right_shift(lax.broadcasted_iota(jnp.int32, (tp * group, 1), 0), group.bit_length() - 1)
    wk = wk_ref[0].astype(BF16)
    wrows = wk.shape[0]
    s_c = _dot_nt(qa, wk) * scale
    s_n = _dot_nt(qa, wkn_ref[...].astype(BF16)) * scale
    d_c = (past + trow) - (past - wrows + lax.broadcasted_iota(jnp.int32, (1, wrows), 1))
    jn = lax.broadcasted_iota(jnp.int32, (1, tp), 1)
    d_n = trow - jn
    m_c = (d_c >= 0) & (d_c < WINDOW)
    m_n = (d_n >= 0) & (d_n < WINDOW) & (jn < t_valid)
    s_c = jnp.where(m_c, s_c, NEG)
    s_n = jnp.where(m_n, s_n, NEG)
    mx = jnp.maximum(jnp.max(s_c, axis=-1, keepdims=True), jnp.max(s_n, axis=-1, keepdims=True))
    e_c = jnp.where(m_c, jnp.exp(s_c - mx), 0.0)
    e_n = jnp.where(m_n, jnp.exp(s_n - mx), 0.0)
    den = jnp.maximum(jnp.sum(e_c, axis=-1, keepdims=True) + jnp.sum(e_n, axis=-1, keepdims=True), TINY)
    o_w = _dot((e_c / den).astype(BF16), wv_ref[0].astype(BF16)) + \
        _dot((e_n / den).astype(BF16), wvn_ref[...].astype(BF16))
    owin_ref[0, 0] = o_w.astype(owin_ref.dtype)

    lax.fori_loop(0, n_fetch, wait, 0)

    kn = kn_ref[...].astype(BF16)
    vn = vn_ref[...].astype(BF16)
    span = nsel * SLC_BLOCK
    oslc_ref[...] = jnp.zeros(oslc_ref.shape, oslc_ref.dtype)
    for t in range(t_valid):
        qt = q_ref[0, 0, t * group:(t + 1) * group, :].astype(BF16)
        kt = kbuf[t * span:(t + 1) * span, :].astype(BF16)
        vt = vbuf[t * span:(t + 1) * span, :].astype(BF16)
        kp = kpos_ref[0, 0, t:t + 1, :]
        okv = okx_ref[0, 0, t:t + 1, :] > 0.5
        qpos = past + t
        m_g = (kp <= qpos) & okv & (kp < past)
        s_g = jnp.where(m_g, _dot_nt(qt, kt) * scale, NEG)
        selrow = jnp.zeros((1, tp), F32)
        for j in range(t_valid):
            hit = jnp.max(jnp.where((kp == past + j) & okv, 1.0, 0.0), axis=-1, keepdims=True)
            selrow = selrow + jnp.where(jn == j, hit, 0.0)
        m_w = (selrow > 0.5) & (jn <= t)
        s_w = jnp.where(m_w, _dot_nt(qt, kn) * scale, NEG)
        mx = jnp.maximum(jnp.max(s_g, axis=-1, keepdims=True), jnp.max(s_w, axis=-1, keepdims=True))
        e_g = jnp.where(m_g, jnp.exp(s_g - mx), 0.0)
        e_w = jnp.where(m_w, jnp.exp(s_w - mx), 0.0)
        den = jnp.sum(e_g, axis=-1, keepdims=True) + jnp.sum(e_w, axis=-1, keepdims=True)
        o_t = _dot((e_g / den).astype(BF16), vt) + _dot((e_w / den).astype(BF16), vn)
        oslc_ref[0, 0, t * group:(t + 1) * group, :] = o_t.astype(oslc_ref.dtype)


def sample_attn(q4, idx, okf, kv_new, cache_kv4, cache_win3, page_table, *, kvh, group, hd, t_valid, past):
    nb = q4.shape[0]
    tp = q4.shape[2] // group
    nsel = idx.shape[2]
    span = nsel * SLC_BLOCK
    wrows = cache_win3.shape[1]
    idx_t = idx.transpose(0, 1, 3, 2)
    kpos = (idx_t[..., None] * SLC_BLOCK + jnp.arange(SLC_BLOCK, dtype=jnp.int32)).reshape(nb, kvh, tp, span)
    okx = jnp.broadcast_to(okf.transpose(0, 1, 3, 2)[..., None], (nb, kvh, tp, nsel, SLC_BLOCK)).reshape(
        nb, kvh, tp, span)
    grid_spec = pltpu.PrefetchScalarGridSpec(
        num_scalar_prefetch=2,
        grid=(nb, kvh),
        in_specs=[
            pl.BlockSpec((1, 1, tp * group, hd), lambda b, g, *_: (b, g, 0, 0)),
            pl.BlockSpec((1, 1, tp, span), lambda b, g, *_: (b, g, 0, 0)),
            pl.BlockSpec((1, 1, tp, span), lambda b, g, *_: (b, g, 0, 0)),
            pl.BlockSpec((tp, hd), lambda b, g, *_: (b, 2 * kvh + g)),
            pl.BlockSpec((tp, hd), lambda b, g, *_: (b, 3 * kvh + g)),
            pl.BlockSpec((1, wrows, hd), lambda b, g, *_: (b, 0, g)),
            pl.BlockSpec((1, wrows, hd), lambda b, g, *_: (b, 0, kvh + g)),
            pl.BlockSpec((tp, hd), lambda b, g, *_: (b, 4 * kvh + g)),
            pl.BlockSpec((tp, hd), lambda b, g, *_: (b, 5 * kvh + g)),
            pl.BlockSpec(memory_space=pl.ANY),
        ],
        out_specs=[pl.BlockSpec((1, 1, tp * group, hd), lambda b, g, *_: (b, g, 0, 0)),
                   pl.BlockSpec((1, 1, tp * group, hd), lambda b, g, *_: (b, g, 0, 0))],
        scratch_shapes=[pltpu.VMEM((t_valid * span, hd), F32),
                        pltpu.VMEM((t_valid * span, hd), F32),
                        pltpu.SemaphoreType.DMA((2,))],
    )
    return pl.pallas_call(
        functools.partial(_sample_attn_kernel, kvh=kvh, group=group, hd=hd, t_valid=t_valid, past=past,
                          nsel=nsel, ks_sec=2, vs_sec=3),
        grid_spec=grid_spec,
        out_shape=[jax.ShapeDtypeStruct(q4.shape, F32), jax.ShapeDtypeStruct(q4.shape, F32)],
        compiler_params=_params(("arbitrary", "arbitrary")),
        name="sample_attn",
    )(idx.reshape(-1), page_table.reshape(-1), q4, kpos, okx, kv_new, kv_new, cache_win3, cache_win3,
      kv_new, kv_new, cache_kv4)


def _combine_kernel(oc_ref, os_ref, ow_ref, g_ref, o_ref, *, heads, hd):
    gates = g_ref[...]
    for h in range(heads):
        sl = slice(h * hd, (h + 1) * hd)
        o = (gates[:, h:h + 1] * oc_ref[:, sl].astype(F32)
             + gates[:, heads + h:heads + h + 1] * os_ref[:, sl].astype(F32)
             + gates[:, 2 * heads + h:2 * heads + h + 1] * ow_ref[:, sl].astype(F32))
        o_ref[:, sl] = o.astype(o_ref.dtype)


def combine(o_cmp, o_slc, o_win, gates, heads, hd):
    m, d = o_cmp.shape
    tm = _tile(m, 256)
    spec = pl.BlockSpec((tm, d), lambda i: (i, 0))
    return pl.pallas_call(
        functools.partial(_combine_kernel, heads=heads, hd=hd),
        grid=(m // tm,),
        in_specs=[spec, spec, spec, pl.BlockSpec((tm, 3 * heads), lambda i: (i, 0))],
        out_specs=spec,
        out_shape=jax.ShapeDtypeStruct((m, d), BF16),
        compiler_params=_params(("parallel",)),
        name="combine",
    )(o_cmp, o_slc, o_win, gates)


def _rope_tables_half(pos, half, theta):
    inv = jnp.power(theta, -jnp.arange(half, dtype=F32) / half)
    ang = pos.astype(F32)[:, None] * inv[None, :]
    return jnp.cos(ang), jnp.sin(ang)


def _nsa_rope_tables(pos, hd):
    half = hd // 8
    cos, sin = _rope_tables_half(pos, half, ROPE_THETA)
    n = pos.shape[0]
    ones = jnp.ones((n, hd - 2 * half), F32)
    zeros = jnp.zeros((n, hd - 2 * half), F32)
    zh = jnp.zeros((n, half), F32)
    c = jnp.concatenate([cos, cos, ones], axis=1)
    s_up = jnp.concatenate([-sin, zh, zeros], axis=1)
    s_dn = jnp.concatenate([zh, sin, zeros], axis=1)
    return c, s_up, s_dn


class _Stream:
    def __init__(self, x, nb, t_rows, t_valid, pos0, ret_s0, conv_state, ctx):
        self.x, self.nb, self.t_rows, self.t_valid, self.pos0 = x, nb, t_rows, t_valid, pos0
        self.ret_s0, self.conv_state, self.ctx = ret_s0, conv_state, ctx
        self.m = nb * t_rows
        self.pos = pos0 + jnp.arange(t_rows)


def _nsa_branches(st, q, kv32, kv16, w_pairs, pe_term, w, *, hd, kvh, group):
    nb, t_rows, t_valid, pos0, m = st.nb, st.t_rows, st.t_valid, st.pos0, st.m
    qw = kvh * group * hd
    win_new = kv32[:, 4 * kvh * hd:].reshape(nb, t_rows, 2, kvh, hd)[:, :t_valid]
    if st.ctx is None:
        p = cmp_proj_rows(kv32, nb, t_rows, w_pairs, kvh, hd)
        kvc = cmp_finish(p, pe_term, w["cmp_w2"], kvh)
        n_seg = t_rows // CMP_STRIDE
        n_c = n_seg - CMP_LEN // CMP_STRIDE + 1
        n_slc = -(-t_rows // SLC_BLOCK)
        o_cmp, imp = cmp_attn(q, kvc, nb=nb, kvh=kvh, group=group, hd=hd, n_c=n_c, n_slc=n_slc,
                              pos0=pos0, tq=512)
        bias = select_blocks(imp, n_slc=n_slc, pos0=pos0, emit_idx=False)
        bias = jnp.pad(bias, ((0, 0), (0, 0), (0, 0), (0, hd - bias.shape[-1])))
        kv5 = kv16.reshape(nb, t_rows, 6, kvh, hd)
        onehot = (jnp.arange(t_rows)[:, None] // SLC_BLOCK == jnp.arange(hd)[None, :]).astype(BF16)
        k_slc = jnp.concatenate([kv5[:, :, 2], jnp.broadcast_to(onehot[None, :, None, :], (nb, t_rows, kvh, hd))],
                                axis=-1).reshape(m, kvh * 2 * hd)
        k_win = kv5[:, :, 4].reshape(m, kvh * hd)
        ones_rows = jnp.zeros((nb, kvh, V_ROWS_PAD, t_rows), BF16).at[:, :, 0].set(1.0)

        def v_t(sec):
            return jnp.concatenate([kv5[:, :, sec].transpose(0, 2, 3, 1), ones_rows], axis=2)

        o_slc = prompt_attn(q, k_slc, v_t(3), bias, mode="slc", nb=nb, kvh=kvh, group=group, hd=hd, tq=WINDOW)
        o_win = prompt_attn(q, k_win, v_t(5), None, mode="win", nb=nb, kvh=kvh, group=group, hd=hd, tq=WINDOW)
        win = win_new[:, t_valid - min(WINDOW, t_valid):]
    else:
        cache_kv, cache_win, page_table = st.ctx
        past = pos0
        p = cmp_proj_pages(cache_kv, page_table, w_pairs, kvh, hd, pages_per_step=min(16, page_table.shape[1]))
        kvc = cmp_finish(p, pe_term, w["cmp_w2"], kvh)
        n_seg = (past + t_valid) // CMP_STRIDE
        n_c = n_seg - CMP_LEN // CMP_STRIDE + 1
        n_slc = -(-(past + t_valid) // SLC_BLOCK)
        o_cmp, imp = cmp_attn(q, kvc, nb=nb, kvh=kvh, group=group, hd=hd, n_c=n_c, n_slc=n_slc,
                              pos0=pos0, tq=t_rows)
        idx, okf = select_blocks(imp, n_slc=n_slc, pos0=pos0, emit_idx=True)
        q4 = q.astype(F32).reshape(nb, t_rows, kvh, group, hd).transpose(0, 2, 1, 3, 4).reshape(
            nb, kvh, t_rows * group, hd)
        wrows = cache_win.shape[1]
        o_slc4, o_win4 = sample_attn(q4, idx, okf, kv32, cache_kv,
                                     cache_win.reshape(nb, wrows, 2 * kvh * hd), page_table,
                                     kvh=kvh, group=group, hd=hd, t_valid=t_valid, past=past)

        def rows(o4):
            return o4.reshape(nb, kvh, t_rows, group, hd).transpose(0, 2, 1, 3, 4).reshape(m, qw).astype(BF16)

        o_slc, o_win = rows(o_slc4), rows(o_win4)
        win = jnp.concatenate([cache_win, win_new], axis=1)[:, t_valid:]
    return o_cmp, o_slc, o_win, win


def _trunk(sp, ss, w):
    d = sp.x.shape[1]
    streams = (sp, ss)
    tm = 1024
    d_ff = w["ffn_conv_w"].shape[-1]
    heads_r = RET_HEADS
    dk = d // heads_r
    dv = 2 * dk
    hd = d // NSA_HEADS
    kvh = NSA_KV_HEADS
    group = NSA_HEADS // kvh
    qw = NSA_HEADS * hd
    rope_half = hd // 8

    def norm(xs, g, dt=BF16):
        return [rmsnorm(x, g, dt) for x in xs]

    def ffn(xs, layer):
        hs = norm(xs, w["norm_ffn"][layer])
        st = ss.conv_state[layer]
        f1 = jnp.zeros((ss.nb, ss.t_rows, d_ff), F32).at[:, 0].set(st[:, 1])
        f2 = jnp.zeros((ss.nb, ss.t_rows, d_ff), F32).at[:, 0].set(st[:, 0]).at[:, 1].set(st[:, 1])
        (a_p, tail_p), (a_s, tail_s), w_down16 = ffn_up(
            hs[0], hs[1], w["ffn_w_up"], w["ffn_w_down"], layer, d_ff, w["ffn_conv_w"][layer],
            w["ffn_conv_b"][layer], tm=tm,
            state=sp.conv_state[layer], fill=(f1.reshape(ss.m, d_ff), f2.reshape(ss.m, d_ff)), seg=ss.t_rows)
        cs = [tail_p[:, SUBLANES - 2:],
              tail_s.reshape(ss.nb, ss.t_rows, d_ff)[:, ss.t_valid - 2:ss.t_valid]]
        ys = [matmul(a, w_down16, d, tm=512, tn=512, tk=d_ff, out_dtypes=[F32], res=x)
              for a, x in zip((a_p, a_s), xs)]
        return ys, cs

    xs = [sp.x, ss.x]
    hs = norm(xs, w["norm_mix"][0])
    ret_w_out = w["ret_w_out"]
    (proj_p,), (proj_s,), ret_w_out16 = matmul_ws(
        hs[0], hs[1], w["ret_w_in"], w["ret_w_in"].shape[-1], tm=tm, tn=512, out_dtypes=[BF16],
        round_rows=(ret_w_out, ret_w_out.shape[0] // 256))
    o_ret, s_ret = [], []
    for st, proj in zip(streams, (proj_p, proj_s)):
        chunk = math.gcd(st.t_valid, RET_CHUNK)
        cpad = chunk if st.ctx is None else st.t_rows
        cos_r, sin_r = _rope_tables_half(st.pos, dk // 2, RET_THETA)
        o, s = retention(proj, st.ret_s0, cos_r, sin_r, nb=st.nb, heads=heads_r, dk=dk, dv=dv,
                         chunk=cpad, c_valid=chunk)
        o_ret.append(o)
        s_ret.append(s)
    xs = [matmul(o, ret_w_out16, d, tm=512, tn=512, tk=heads_r * dv, out_dtypes=[F32], res=x)
          for o, x in zip(o_ret, xs)]
    xs, conv0 = ffn(xs, 0)

    hk = norm(xs, w["kv_norm"])
    rope_p = _nsa_rope_tables(sp.pos, hd)
    rope_s = tuple(jnp.tile(tb, (ss.nb, 1)) for tb in _nsa_rope_tables(ss.pos, hd))
    (kv32_p, kv16_p), (kv32_s, kv16_s) = matmul_ws(
        hk[0], hk[1], w["kv_w"], 6 * kvh * hd, tm=tm, tn=kvh * hd, out_dtypes=[F32, BF16],
        rope=(rope_p, rope_s), rope_every=2, rope_half=rope_half)
    kv32, kv16 = (kv32_p, kv32_s), (kv16_p, kv16_s)
    kv_rows = [kv[:, :4 * kvh * hd].reshape(st.nb, st.t_rows, 4, kvh, hd)[:, :st.t_valid]
               for st, kv in zip(streams, kv32)]

    hs = norm(xs, w["norm_mix"][1])
    (q_p,), (q_s,) = matmul_ws(hs[0], hs[1], w["nsa_w_in"], qw, tm=tm, tn=512, out_dtypes=[BF16],
                               rope=(rope_p, rope_s), rope_every=1, rope_half=rope_half)
    w_pairs = _cmp_w1_pairs(w["cmp_w1"], hd)
    pe_term = cmp_pe_term(w["cmp_pe"], w["cmp_w1"])
    os, wins = [], []
    for st, h, q, k32, k16 in zip(streams, hs, (q_p, q_s), kv32, kv16):
        gates = matmul(h, w["nsa_w_gate16"], 3 * NSA_HEADS, tm=tm, tn=3 * NSA_HEADS, tk=d, out_dtypes=[F32],
                       act="sigmoid")
        o_cmp, o_slc, o_win, win = _nsa_branches(st, q, k32, k16, w_pairs, pe_term, w, hd=hd, kvh=kvh, group=group)
        os.append(combine(o_cmp, o_slc, o_win, gates, NSA_HEADS, hd))
        wins.append(win)
    (x_p,), (x_s,) = matmul_ws(os[0], os[1], w["nsa_w_out"], d, tm=tm, tn=512, out_dtypes=[F32], res=xs)
    xs, conv1 = ffn([x_p, x_s], 1)
    ys = norm(xs, w["norm_final"], F32)
    convs = [jnp.stack([c0, c1]) for c0, c1 in zip(conv0, conv1)]
    return [(y, s[None], cv, kvr, win) for y, s, cv, kvr, win in zip(ys, s_ret, convs, kv_rows, wins)]


def kernel(x_prompt, x_sample, cache_kv, cache_win, state_ret, state_conv, page_table, norm_mix, norm_ffn,
           ret_w_in, ret_w_out, kv_norm, kv_w, cmp_pe, cmp_w1, cmp_w2, nsa_w_in, nsa_w_out, ffn_w_up, ffn_conv_w,
           ffn_conv_b, ffn_w_down, norm_final):
    b, t, d = x_prompt.shape
    db, dt, _ = x_sample.shape
    past = page_table.shape[1] * cache_kv.shape[1]
    qw = nsa_w_out.shape[1]
    w = dict(
        norm_mix=norm_mix, norm_ffn=norm_ffn, kv_norm=kv_norm, norm_final=norm_final,
        cmp_pe=cmp_pe, cmp_w1=cmp_w1, cmp_w2=cmp_w2, ffn_conv_w=ffn_conv_w, ffn_conv_b=ffn_conv_b,
        ret_w_in=ret_w_in[0], kv_w=kv_w, nsa_w_in=nsa_w_in[0], nsa_w_out=nsa_w_out[0], ffn_w_up=ffn_w_up,
        ret_w_out=ret_w_out[0], ffn_w_down=ffn_w_down, nsa_w_gate16=nsa_w_in[0][:, qw:].astype(BF16),
    )
    heads_r = RET_HEADS
    dk = d // heads_r
    ret0 = jnp.zeros((b, heads_r, dk, 2 * dk), F32)
    conv0 = jnp.zeros((state_conv.shape[0], b, 2, ffn_conv_w.shape[-1]), F32)
    tp = 2 * SUBLANES
    xs = jnp.zeros((db, tp, d), F32).at[:, :dt].set(x_sample).reshape(db * tp, d)
    sp = _Stream(x_prompt.reshape(b * t, d), b, t, t, 0, ret0, conv0, None)
    ss = _Stream(xs, db, tp, dt, past, state_ret[0], state_conv, (cache_kv, cache_win, page_table))
    (y_p, ret_p, conv_p, kv_p, win_p), (y_s, ret_s, conv_s, kv_s, win_s) = _trunk(sp, ss, w)
    y_s = y_s.reshape(db, tp, d)[:, :dt]
    return (y_p.reshape(b, t, d), y_s, kv_p, kv_s, win_p, win_s, ret_p, ret_s, conv_p, conv_s)
```

```python
import functools
import math

import jax
import jax.numpy as jnp
from jax import lax
from jax.experimental import pallas as pl
from jax.experimental.pallas import tpu as pltpu

F32 = jnp.float32
BF16 = jnp.bfloat16

RET_HEADS = 16
RET_CHUNK = 512
RET_THETA = 10000.0
NSA_HEADS = 32
NSA_KV_HEADS = 4
CMP_LEN = 32
CMP_STRIDE = 16
SLC_BLOCK = 64
N_SELECT = 16
WINDOW = 512
ROPE_THETA = 500000.0
PAGE_SIZE = 128
EPS = 1e-6
NEG = -1e30
FORCE = 1e9
TINY = 1e-20

LANES = 128
SUBLANES = 8
VMEM_LIMIT = 60 * 1024 * 1024


def _params(sem):
    return pltpu.CompilerParams(dimension_semantics=sem, vmem_limit_bytes=VMEM_LIMIT)


def _tile(dim, pref):
    if dim <= pref:
        return dim
    t = pref
    while dim % t:
        t //= 2
    return t


def _sigmoid(x):
    return 1.0 / (1.0 + jnp.exp(-x))


def _dot(a, b):
    return jnp.dot(a, b, preferred_element_type=F32)


def _dot_nt(a, b):
    return lax.dot_general(a, b, (((1,), (1,)), ((), ())), preferred_element_type=F32)


def _rmsnorm_kernel(x_ref, g_ref, o_ref):
    x = x_ref[...]
    y = x * lax.rsqrt(jnp.mean(x * x, axis=-1, keepdims=True) + EPS)
    o_ref[...] = (y * g_ref[...]).astype(o_ref.dtype)


def rmsnorm(x, g, out_dtype):
    m, d = x.shape
    tm = _tile(m, 256)
    return pl.pallas_call(
        _rmsnorm_kernel,
        grid=(m // tm,),
        in_specs=[pl.BlockSpec((tm, d), lambda i: (i, 0)),
                  pl.BlockSpec((1, d), lambda i: (0, 0))],
        out_specs=pl.BlockSpec((tm, d), lambda i: (i, 0)),
        out_shape=jax.ShapeDtypeStruct((m, d), out_dtype),
        compiler_params=_params(("parallel",)),
        name="rmsnorm",
    )(x, g.reshape(1, d).astype(F32))


def _rope_lanes(x, c, s_up, s_dn, half):
    return x * c + pltpu.roll(x, LANES - half, 1) * s_up + pltpu.roll(x, half, 1) * s_dn


def _mm_epilogue(acc, j, res_ref, rope_refs, out_refs, *, rope_every, rope_half, act):
    def store(val):
        for o in out_refs:
            o[...] = val.astype(o.dtype)

    if res_ref is not None:
        acc = acc + res_ref[...]
    if act == "sigmoid":
        acc = _sigmoid(acc)
    if not rope_every:
        store(acc)
        return
    tn = acc.shape[1]

    def roped():
        c, su, sd = (r[...] for r in rope_refs)
        pieces = [_rope_lanes(acc[:, a:a + LANES], c, su, sd, rope_half) for a in range(0, tn, LANES)]
        store(jnp.concatenate(pieces, axis=1))

    if rope_every == 1:
        roped()
    else:
        pl.when(j % rope_every == 0)(roped)
        pl.when(j % rope_every != 0)(lambda: store(acc))


def _mm_kernel(*refs, nk, has_res, rope_every, rope_half, act, n_out):
    x_ref, w_ref = refs[0], refs[1]
    pos = 2
    res_ref = None
    if has_res:
        res_ref = refs[pos]
        pos += 1
    rope_refs = None
    if rope_every:
        rope_refs = refs[pos:pos + 3]
        pos += 3
    out_refs = refs[pos:pos + n_out]
    acc_ref = refs[pos + n_out] if nk > 1 else None
    j = pl.program_id(1)
    k = pl.program_id(2)

    part = _dot(x_ref[...], w_ref[...].astype(BF16))
    finish = functools.partial(_mm_epilogue, j=j, res_ref=res_ref, rope_refs=rope_refs, out_refs=out_refs,
                               rope_every=rope_every, rope_half=rope_half, act=act)

    if nk == 1:
        finish(part)
    else:
        @pl.when(k == 0)
        def _():
            acc_ref[...] = part

        @pl.when(k > 0)
        def _():
            acc_ref[...] += part

        @pl.when(k == nk - 1)
        def _():
            finish(acc_ref[...])


def matmul(x, w, n, *, tm, tn, tk, out_dtypes, res=None, rope=None, rope_every=0, rope_half=0, act=None):
    m, kdim = x.shape
    tm, tn, tk = _tile(m, tm), _tile(n, tn), _tile(kdim, tk)
    nk = kdim // tk
    in_specs = [pl.BlockSpec((tm, tk), lambda i, j, k: (i, k)),
                pl.BlockSpec((tk, tn), lambda i, j, k: (k, j))]
    args = [x, w]
    if res is not None:
        in_specs.append(pl.BlockSpec((tm, tn), lambda i, j, k: (i, j)))
        args.append(res)
    if rope is not None:
        period = rope[0].shape[0]
        nper = period // tm
        for t in rope:
            in_specs.append(pl.BlockSpec((tm, LANES), lambda i, j, k: (i % nper, 0)))
            args.append(t)
    out_specs = [pl.BlockSpec((tm, tn), lambda i, j, k: (i, j)) for _ in out_dtypes]
    out_shape = [jax.ShapeDtypeStruct((m, n), dt) for dt in out_dtypes]
    scratch = [pltpu.VMEM((tm, tn), F32)] if nk > 1 else []
    outs = pl.pallas_call(
        functools.partial(_mm_kernel, nk=nk, has_res=res is not None,
                          rope_every=rope_every if rope is not None else 0, rope_half=rope_half,
                          act=act, n_out=len(out_dtypes)),
        grid=(m // tm, n // tn, nk),
        in_specs=in_specs, out_specs=out_specs, out_shape=out_shape,
        scratch_shapes=scratch,
        compiler_params=_params(("parallel", "parallel", "arbitrary")),
        name="matmul",
    )(*args)
    return outs if len(outs) > 1 else outs[0]


def _round_rows_job(j, i, src_ref, dst_ref, nblk):
    @pl.when((i == 0) & (j < nblk))
    def _():
        dst_ref[...] = src_ref[...].astype(BF16)


def _round_rows_specs(src, nblk, layer=None):
    rows, cols = src.shape[-2:]
    rb = rows // nblk
    if layer is None:
        in_spec = pl.BlockSpec((rb, cols), lambda j, i: (jnp.minimum(j, nblk - 1), 0))
    else:
        in_spec = pl.BlockSpec((None, rb, cols), lambda j, i: (layer, jnp.minimum(j, nblk - 1), 0))
    out_spec = pl.BlockSpec((rb, cols), lambda j, i: (jnp.minimum(j, nblk - 1), 0))
    return in_spec, out_spec, jax.ShapeDtypeStruct((rows, cols), BF16)


def _mm_ws_kernel(*refs, has_res, rope_every, rope_half, n_out, round_blocks):
    x_ref, xs_ref, w_ref = refs[:3]
    pos = 3
    if round_blocks:
        rsrc_ref = refs[pos]
        pos += 1
    res_ref = ress_ref = rope_refs = ropes_refs = None
    if has_res:
        res_ref, ress_ref = refs[pos:pos + 2]
        pos += 2
    if rope_every:
        rope_refs, ropes_refs = refs[pos:pos + 3], refs[pos + 3:pos + 6]
        pos += 6
    out_refs = refs[pos:pos + n_out]
    outs_refs = refs[pos + n_out:pos + 2 * n_out]
    pos += 2 * n_out
    j = pl.program_id(0)
    i = pl.program_id(1)
    if round_blocks:
        _round_rows_job(j, i, rsrc_ref, refs[pos], round_blocks)
        pos += 1
    w16 = refs[pos]

    @pl.when(i == 0)
    def _():
        w16[...] = w_ref[...].astype(BF16)

    kw = dict(rope_every=rope_every, rope_half=rope_half, act=None)
    _mm_epilogue(_dot(x_ref[...], w16[...]), j, res_ref, rope_refs, out_refs, **kw)

    @pl.when(i == pl.num_programs(1) - 1)
    def _():
        _mm_epilogue(_dot(xs_ref[...], w16[...]), j, ress_ref, ropes_refs, outs_refs, **kw)


def matmul_ws(x, xs, w, n, *, tm, tn, out_dtypes, res=None, rope=None, rope_every=0, rope_half=0,
              round_rows=None):
    m, kdim = x.shape
    ms = xs.shape[0]
    tm, tn = _tile(m, tm), _tile(n, tn)
    in_specs = [pl.BlockSpec((tm, kdim), lambda j, i: (i, 0)),
                pl.BlockSpec((ms, kdim), lambda j, i: (0, 0)),
                pl.BlockSpec((kdim, tn), lambda j, i: (0, j))]
    args = [x, xs, w]
    round_blocks = 0
    if round_rows is not None:
        round_blocks = round_rows[1]
        assert round_blocks <= n // tn
        r_in, r_out, r_shape = _round_rows_specs(*round_rows)
        in_specs.append(r_in)
        args.append(round_rows[0])
    if res is not None:
        in_specs += [pl.BlockSpec((tm, tn), lambda j, i: (i, j)), pl.BlockSpec((ms, tn), lambda j, i: (0, j))]
        args += list(res)
    if rope is not None:
        nper = rope[0][0].shape[0] // tm
        in_specs += [pl.BlockSpec((tm, LANES), lambda j, i: (i % nper, 0))] * 3
        in_specs += [pl.BlockSpec((ms, LANES), lambda j, i: (0, 0))] * 3
        args += list(rope[0]) + list(rope[1])
    out_specs = ([pl.BlockSpec((tm, tn), lambda j, i: (i, j)) for _ in out_dtypes]
                 + [pl.BlockSpec((ms, tn), lambda j, i: (0, j)) for _ in out_dtypes])
    out_shape = ([jax.ShapeDtypeStruct((m, n), dt) for dt in out_dtypes]
                 + [jax.ShapeDtypeStruct((ms, n), dt) for dt in out_dtypes])
    if round_blocks:
        out_specs.append(r_out)
        out_shape.append(r_shape)
    outs = pl.pallas_call(
        functools.partial(_mm_ws_kernel, has_res=res is not None,
                          rope_every=rope_every if rope is not None else 0, rope_half=rope_half,
                          n_out=len(out_dtypes), round_blocks=round_blocks),
        grid=(n // tn, m // tm),
        in_specs=in_specs, out_specs=out_specs, out_shape=out_shape,
        scratch_shapes=[pltpu.VMEM((kdim, tn), BF16)],
        compiler_params=_params(("arbitrary" if round_blocks else "parallel", "arbitrary")),
        name="matmul_ws",
    )(*args)
    k = len(out_dtypes)
    if round_blocks:
        return outs[:k], outs[k:2 * k], outs[2 * k]
    return outs[:k], outs[k:]


def _ffn_up_kernel(x_ref, xs_ref, wv_ref, wg_ref, cw_ref, cb_ref, st_ref, f1_ref, f2_ref, wd_ref,
                   a_ref, tail_ref, as_ref, tails_ref, wd16_ref, w16, carry_ref, *, tiles_per_seq, seg):
    i = pl.program_id(1)
    tn = wv_ref.shape[1]
    _round_rows_job(pl.program_id(0), i, wd_ref, wd16_ref, pl.num_programs(0))

    @pl.when(i == 0)
    def _():
        w16[:, :tn] = wv_ref[...].astype(BF16)
        w16[:, tn:] = wg_ref[...].astype(BF16)

    def val_gate(x):
        r = _dot(x, w16[...])
        return r[:, :tn], r[:, tn:]

    cb = cb_ref[...]
    cw0, cw1, cw2 = cw_ref[0:1, :], cw_ref[1:2, :], cw_ref[2:3, :]

    def gated(val, gate, g1, g2):
        conv = cb + cw0 * g2 + cw1 * g1 + cw2 * gate
        return (val * (conv * _sigmoid(conv))).astype(BF16)

    @pl.when(i % tiles_per_seq == 0)
    def _():
        carry_ref[...] = st_ref[0]

    tm = x_ref.shape[0]
    row = lax.broadcasted_iota(jnp.int32, (SUBLANES, 1), 0)
    val, gate = val_gate(x_ref[...])
    c0 = carry_ref[0:1, :]
    c1 = carry_ref[1:2, :]
    g1 = pltpu.roll(gate, 1, 0)
    g2 = pltpu.roll(gate, 2, 0)
    g1 = jnp.concatenate([jnp.where(row == 0, c1, g1[:SUBLANES]), g1[SUBLANES:]], axis=0)
    g2 = jnp.concatenate([jnp.where(row == 0, c0, jnp.where(row == 1, c1, g2[:SUBLANES])), g2[SUBLANES:]], axis=0)
    a_ref[...] = gated(val, gate, g1, g2)
    carry_ref[...] = gate[tm - 2:tm, :]
    nt = tail_ref.shape[1]
    tail_ref[0] = gate[tm - nt:tm, :]

    @pl.when(i == pl.num_programs(1) - 1)
    def _():
        val, gate = val_gate(xs_ref[...])
        t = jnp.bitwise_and(lax.broadcasted_iota(jnp.int32, (val.shape[0], 1), 0), seg - 1)
        g1 = jnp.where(t == 0, f1_ref[...], pltpu.roll(gate, 1, 0))
        g2 = jnp.where(t < 2, f2_ref[...], pltpu.roll(gate, 2, 0))
        as_ref[...] = gated(val, gate, g1, g2)
        tails_ref[...] = gate


def ffn_up(h, hs, w_up, w_down, layer, d_ff, conv_w, conv_b, *, tm, state, fill, seg):
    m, kdim = h.shape
    ms = hs.shape[0]
    assert seg & (seg - 1) == 0
    tn = _tile(d_ff, 256)
    nj = d_ff // tn
    tm = _tile(m, tm)
    nseq = state.shape[0]
    tps = (m // nseq) // tm
    wd_in, wd_out, wd_shape = _round_rows_specs(w_down, nj, layer)
    a, tail, a_s, tail_s, wd16 = pl.pallas_call(
        functools.partial(_ffn_up_kernel, tiles_per_seq=tps, seg=seg),
        grid=(nj, m // tm),
        in_specs=[pl.BlockSpec((tm, kdim), lambda j, i: (i, 0)),
                  pl.BlockSpec((ms, kdim), lambda j, i: (0, 0)),
                  pl.BlockSpec((None, kdim, tn), lambda j, i: (layer, 0, j)),
                  pl.BlockSpec((None, kdim, tn), lambda j, i: (layer, 0, nj + j)),
                  pl.BlockSpec((3, tn), lambda j, i: (0, j)),
                  pl.BlockSpec((1, tn), lambda j, i: (0, j)),
                  pl.BlockSpec((1, 2, tn), lambda j, i: (i // tps, 0, j)),
                  pl.BlockSpec((ms, tn), lambda j, i: (0, j)),
                  pl.BlockSpec((ms, tn), lambda j, i: (0, j)),
                  wd_in],
        out_specs=[pl.BlockSpec((tm, tn), lambda j, i: (i, j)),
                   pl.BlockSpec((1, SUBLANES, tn), lambda j, i: (i // tps, 0, j)),
                   pl.BlockSpec((ms, tn), lambda j, i: (0, j)),
                   pl.BlockSpec((ms, tn), lambda j, i: (0, j)),
                   wd_out],
        out_shape=[jax.ShapeDtypeStruct((m, d_ff), BF16),
                   jax.ShapeDtypeStruct((nseq, SUBLANES, d_ff), F32),
                   jax.ShapeDtypeStruct((ms, d_ff), BF16),
                   jax.ShapeDtypeStruct((ms, d_ff), F32),
                   wd_shape],
        scratch_shapes=[pltpu.VMEM((kdim, 2 * tn), BF16), pltpu.VMEM((2, tn), F32)],
        compiler_params=_params(("parallel", "arbitrary")),
        name="ffn_up",
    )(h, hs, w_up, w_up, conv_w, conv_b.reshape(1, d_ff), state, fill[0], fill[1], w_down)
    return (a, tail), (a_s, tail_s), wd16


def _ret_kernel(lg_ref, q_ref, k_ref, v_ref, g_ref, cos_ref, sin_ref, s0_ref,
                o_ref, s_out_ref, s_scr, intra_scr, *, c_valid, dk):
    h = pl.program_id(1)
    c = pl.program_id(2)
    nc = pl.num_programs(2)
    half = dk // 2
    lg = lg_ref[h]
    cp = q_ref.shape[0]

    @pl.when(c == 0)
    def _():
        s_scr[...] = s0_ref[0, 0]
        di = lax.broadcasted_iota(jnp.int32, (cp, cp), 0)
        dj = lax.broadcasted_iota(jnp.int32, (cp, cp), 1)
        diff = di - dj
        intra_scr[...] = jnp.where((diff >= 0) & (dj < c_valid),
                                   jnp.exp(lg * jnp.maximum(diff, 0).astype(F32)), 0.0)

    cos = cos_ref[...]
    sin = sin_ref[...]

    def rope(x):
        x1 = x[:, :half]
        x2 = x[:, half:]
        return jnp.concatenate([x1 * cos - x2 * sin, x2 * cos + x1 * sin], axis=1)

    q = rope(q_ref[...].astype(F32))
    k = rope(k_ref[...].astype(F32)) * (dk ** -0.5)
    v = v_ref[...]
    ri = lax.broadcasted_iota(jnp.int32, (cp, 1), 0)
    rif = ri.astype(F32)
    read_decay = jnp.exp(lg * (rif + 1.0))
    write_decay = jnp.where(ri < c_valid, jnp.exp(lg * jnp.maximum(c_valid - 1.0 - rif, 0.0)), 0.0)

    s = s_scr[...]
    att = _dot_nt(q.astype(BF16), k.astype(BF16)) * intra_scr[...]
    o = _dot(att.astype(BF16), v) + _dot((q * read_decay).astype(BF16), s.astype(BF16))
    kw = (k * write_decay).astype(BF16)
    chunk_decay = jnp.exp(lg * jnp.full((1, 1), float(c_valid), F32))
    s_new = s * chunk_decay + lax.dot_general(kw, v, (((0,), (0,)), ((), ())), preferred_element_type=F32)
    s_scr[...] = s_new

    @pl.when(c == nc - 1)
    def _():
        s_out_ref[0, 0] = s_new

    on = o * lax.rsqrt(jnp.mean(o * o, axis=-1, keepdims=True) + EPS)
    gate = g_ref[...].astype(F32)
    o_ref[...] = (on * (gate * _sigmoid(gate))).astype(o_ref.dtype)


def retention(proj, s0, cos, sin, *, nb, heads, dk, dv, chunk, c_valid):
    m = proj.shape[0]
    t = m // nb
    nc = t // chunk
    qb = heads
    vb = 2 * heads * dk // dv
    log_g = jnp.log1p(-jnp.exp2(-5.0 - jnp.arange(heads, dtype=F32)))
    grid_spec = pltpu.PrefetchScalarGridSpec(
        num_scalar_prefetch=1,
        grid=(nb, heads, nc),
        in_specs=[
            pl.BlockSpec((chunk, dk), lambda b, h, c, lg: (b * nc + c, h)),
            pl.BlockSpec((chunk, dk), lambda b, h, c, lg: (b * nc + c, qb + h)),
            pl.BlockSpec((chunk, dv), lambda b, h, c, lg: (b * nc + c, vb + h)),
            pl.BlockSpec((chunk, dv), lambda b, h, c, lg: (b * nc + c, vb + heads + h)),
            pl.BlockSpec((chunk, dk // 2), lambda b, h, c, lg: (c, 0)),
            pl.BlockSpec((chunk, dk // 2), lambda b, h, c, lg: (c, 0)),
            pl.BlockSpec((1, 1, dk, dv), lambda b, h, c, lg: (b, h, 0, 0)),
        ],
        out_specs=[
            pl.BlockSpec((chunk, dv), lambda b, h, c, lg: (b * nc + c, h)),
            pl.BlockSpec((1, 1, dk, dv), lambda b, h, c, lg: (b, h, 0, 0)),
        ],
        scratch_shapes=[pltpu.VMEM((dk, dv), F32), pltpu.VMEM((chunk, chunk), F32)],
    )
    o, s = pl.pallas_call(
        functools.partial(_ret_kernel, c_valid=c_valid, dk=dk),
        grid_spec=grid_spec,
        out_shape=[jax.ShapeDtypeStruct((m, heads * dv), BF16),
                   jax.ShapeDtypeStruct((nb, heads, dk, dv), F32)],
        compiler_params=_params(("parallel", "parallel", "arbitrary")),
        name="retention",
    )(log_g, proj, proj, proj, proj, cos, sin, s0)
    return o, s


def _cmp_project(planes, w_ref, p_ref, kvh, hd):
    n_seg = planes.shape[1] // CMP_STRIDE
    for br in range(2):
        acc = jnp.zeros((kvh * n_seg, 2 * hd), F32)
        for sp in range(CMP_STRIDE // 2):
            pieces = []
            for g in range(kvh):
                plane = planes.at[br * kvh + g]
                a = plane[pl.ds(2 * sp, n_seg, stride=CMP_STRIDE), :]
                b = plane[pl.ds(2 * sp + 1, n_seg, stride=CMP_STRIDE), :]
                pieces.append(jnp.concatenate([a, b], axis=1))
            lhs = jnp.concatenate(pieces, axis=0).astype(BF16)
            acc = acc + _dot(lhs, w_ref[br, sp])
        for g in range(kvh):
            p_ref[0, br * kvh + g] = acc[g * n_seg:(g + 1) * n_seg, :]


def _cmp_proj_rows_kernel(x_ref, w_ref, p_ref, r_ref, *, kvh, hd):
    for hh in range(2 * kvh):
        r_ref[hh] = x_ref[:, hh * hd:(hh + 1) * hd]
    _cmp_project(r_ref, w_ref, p_ref, kvh, hd)


def _cmp_proj_pages_kernel(pt_ref, cache_ref, w_ref, p_ref, r_ref, sem, *, kvh, hd, pages_per_step, n_pages):
    b = pl.program_id(0)
    c = pl.program_id(1)
    nchunk = pl.num_programs(1)
    step = b * nchunk + c
    nsteps = pl.num_programs(0) * nchunk
    page = cache_ref.shape[1]

    def copies(st, slot):
        bb = st // nchunk
        cc = st % nchunk
        out = []
        for u in range(pages_per_step):
            phys = pt_ref[bb * n_pages + cc * pages_per_step + u]
            for hh in range(2 * kvh):
                out.append(pltpu.make_async_copy(
                    cache_ref.at[phys, :, hh // kvh, hh % kvh, :],
                    r_ref.at[slot, hh, pl.ds(u * page, page), :],
                    sem.at[slot]))
        return out

    slot = step % 2

    @pl.when(step == 0)
    def _():
        for cp in copies(step, slot):
            cp.start()

    @pl.when(step + 1 < nsteps)
    def _():
        for cp in copies(step + 1, 1 - slot):
            cp.start()

    for cp in copies(step, slot):
        cp.wait()
    _cmp_project(r_ref.at[slot], w_ref, p_ref, kvh, hd)


def _cmp_w1_pairs(cmp_w1, hd):
    r = CMP_LEN // CMP_STRIDE
    e = cmp_w1.shape[-1]
    w = cmp_w1.reshape(2, r, CMP_STRIDE // 2, 2, hd, e)
    w = w.transpose(0, 2, 3, 4, 1, 5)
    return w.reshape(2, CMP_STRIDE // 2, 2 * hd, r * e).astype(BF16)


def cmp_proj_rows(rows2d, nb, t, w_pairs, kvh, hd):
    n_seg = t // CMP_STRIDE
    width = 2 * kvh * hd
    return pl.pallas_call(
        functools.partial(_cmp_proj_rows_kernel, kvh=kvh, hd=hd),
        grid=(nb,),
        in_specs=[pl.BlockSpec((t, width), lambda b: (b, 0)),
                  pl.BlockSpec(w_pairs.shape, lambda b: (0, 0, 0, 0))],
        out_specs=pl.BlockSpec((1, 2 * kvh, n_seg, 2 * hd), lambda b: (b, 0, 0, 0)),
        out_shape=jax.ShapeDtypeStruct((nb, 2 * kvh, n_seg, 2 * hd), F32),
        scratch_shapes=[pltpu.VMEM((2 * kvh, t, hd), F32)],
        compiler_params=_params(("parallel",)),
        name="cmp_proj_prompt",
    )(rows2d, w_pairs)


def cmp_proj_pages(cache, page_table, w_pairs, kvh, hd, pages_per_step):
    nb, n_pages = page_table.shape
    page = cache.shape[1]
    n_seg = pages_per_step * page // CMP_STRIDE
    nchunk = n_pages // pages_per_step
    grid_spec = pltpu.PrefetchScalarGridSpec(
        num_scalar_prefetch=1,
        grid=(nb, nchunk),
        in_specs=[pl.BlockSpec(memory_space=pl.ANY),
                  pl.BlockSpec(w_pairs.shape, lambda b, c, pt: (0, 0, 0, 0))],
        out_specs=pl.BlockSpec((1, 2 * kvh, n_seg, 2 * hd), lambda b, c, pt: (b, 0, c, 0)),
        scratch_shapes=[pltpu.VMEM((2, 2 * kvh, pages_per_step * page, hd), F32),
                        pltpu.SemaphoreType.DMA((2,))],
    )
    return pl.pallas_call(
        functools.partial(_cmp_proj_pages_kernel, kvh=kvh, hd=hd, pages_per_step=pages_per_step,
                          n_pages=n_pages),
        grid_spec=grid_spec,
        out_shape=jax.ShapeDtypeStruct((nb, 2 * kvh, nchunk * n_seg, 2 * hd), F32),
        compiler_params=_params(("arbitrary", "arbitrary")),
        name="cmp_proj_sample",
    )(page_table.reshape(-1), cache, w_pairs)


def _cmp_pe_kernel(pe_ref, w_ref, o_ref):
    o_ref[0] = _dot(pe_ref[0], w_ref[0])


def cmp_pe_term(cmp_pe, cmp_w1):
    _, n, hd = cmp_pe.shape
    e = cmp_w1.shape[-1]
    pe_flat = jnp.zeros((2, SUBLANES, n * hd), BF16).at[:, 0].set(cmp_pe.reshape(2, n * hd).astype(BF16))
    w_flat = cmp_w1.reshape(2, n * hd, e).astype(BF16)
    return pl.pallas_call(
        _cmp_pe_kernel,
        grid=(2,),
        in_specs=[pl.BlockSpec((1, SUBLANES, n * hd), lambda b: (b, 0, 0)),
                  pl.BlockSpec((1, n * hd, e), lambda b: (b, 0, 0))],
        out_specs=pl.BlockSpec((1, SUBLANES, e), lambda b: (b, 0, 0)),
        out_shape=jax.ShapeDtypeStruct((2, SUBLANES, e), F32),
        compiler_params=_params(("parallel",)),
        name="cmp_pe",
    )(pe_flat, w_flat)


def _cmp_finish_kernel(p_ref, pe_ref, w2_ref, o_ref):
    p = p_ref[0, 0]
    n_seg = p.shape[0]
    e = p.shape[1] // 2
    hid = p[:, :e] + pltpu.roll(p[:, e:], n_seg - 1, 0) + pe_ref[0, 0:1, :]
    c = math.sqrt(2.0 / math.pi)
    act = 0.5 * hid * (1.0 + jnp.tanh(c * (hid + 0.044715 * (hid * hid * hid))))
    o_ref[0, 0] = _dot(act.astype(BF16), w2_ref[0]).astype(o_ref.dtype)


def cmp_finish(p, pe_term, cmp_w2, kvh):
    nb, nu, n_seg, e2 = p.shape
    e = e2 // 2
    hd = cmp_w2.shape[-1]
    return pl.pallas_call(
        _cmp_finish_kernel,
        grid=(nb, nu),
        in_specs=[pl.BlockSpec((1, 1, n_seg, e2), lambda b, u: (b, u, 0, 0)),
                  pl.BlockSpec((1, SUBLANES, e), lambda b, u: (u // kvh, 0, 0)),
                  pl.BlockSpec((1, e, hd), lambda b, u: (u // kvh, 0, 0))],
        out_specs=pl.BlockSpec((1, 1, n_seg, hd), lambda b, u: (b, u, 0, 0)),
        out_shape=jax.ShapeDtypeStruct((nb, nu, n_seg, hd), BF16),
        compiler_params=_params(("parallel", "parallel")),
        name="cmp_finish",
    )(p, pe_term, cmp_w2.astype(BF16))


def _cmp_attn_kernel(q_ref, kc_ref, vc_ref, agg_ref, o_ref, imp_ref, *, group, hd, n_c, pos0):
    qi = pl.program_id(2)
    tq = q_ref.shape[0]
    kc = kc_ref[0, 0]
    vc = vc_ref[0, 0]
    agg = agg_ref[...]
    n_cp = kc.shape[0]
    qs = jnp.concatenate([q_ref[:, h * hd:(h + 1) * hd] for h in range(group)], axis=0)
    qpos1 = pos0 + qi * tq + lax.broadcasted_iota(jnp.int32, (tq, 1), 0)
    qpos = jnp.concatenate([qpos1] * group, axis=0)
    ci = lax.broadcasted_iota(jnp.int32, (1, n_cp), 1)
    cm = ((ci * CMP_STRIDE + (CMP_LEN - 1)) <= qpos) & (ci < n_c)
    s = jnp.where(cm, _dot_nt(qs, kc) * (hd ** -0.5), NEG)
    e = jnp.where(cm, jnp.exp(s - jnp.max(s, axis=-1, keepdims=True)), 0.0)
    p = e / jnp.maximum(jnp.sum(e, axis=-1, keepdims=True), TINY)
    pb = p.astype(BF16)
    o = _dot(pb, vc)
    psum = jnp.zeros((tq, n_cp), F32)
    for h in range(group):
        o_ref[:, h * hd:(h + 1) * hd] = o[h * tq:(h + 1) * tq, :].astype(o_ref.dtype)
        psum = psum + pb[h * tq:(h + 1) * tq, :].astype(F32)
    hi = psum.astype(BF16)
    lo = (psum - hi.astype(F32)).astype(BF16)
    imp_ref[0, 0] = _dot_nt(agg, hi) + _dot_nt(agg, lo)


def _slc_aggregation_t(n_cp, n_slc, rows):
    rs = SLC_BLOCK // CMP_STRIDE
    rc = CMP_LEN // CMP_STRIDE
    i = jnp.arange(n_cp)[None, :]
    j = jnp.arange(rows)[:, None]
    w = sum((i == j * rs + m - n).astype(F32) for m in range(rs) for n in range(rc))
    return jnp.where(j < n_slc, w, 0.0).astype(BF16)


def cmp_attn(q, kvc, *, nb, kvh, group, hd, n_c, n_slc, pos0, tq):
    m = q.shape[0]
    t = m // nb
    tq = _tile(t, tq)
    nq = t // tq
    n_cp = kvc.shape[2]
    rows = -(-n_slc // SUBLANES) * SUBLANES
    agg_t = _slc_aggregation_t(n_cp, n_slc, rows)
    gw = group * hd
    return pl.pallas_call(
        functools.partial(_cmp_attn_kernel, group=group, hd=hd, n_c=n_c, pos0=pos0),
        grid=(nb, kvh, nq),
        in_specs=[pl.BlockSpec((tq, gw), lambda b, g, i: (b * nq + i, g)),
                  pl.BlockSpec((1, 1, n_cp, hd), lambda b, g, i: (b, g, 0, 0)),
                  pl.BlockSpec((1, 1, n_cp, hd), lambda b, g, i: (b, kvh + g, 0, 0)),
                  pl.BlockSpec((rows, n_cp), lambda b, g, i: (0, 0))],
        out_specs=[pl.BlockSpec((tq, gw), lambda b, g, i: (b * nq + i, g)),
                   pl.BlockSpec((1, 1, rows, tq), lambda b, g, i: (b, g, 0, i))],
        out_shape=[jax.ShapeDtypeStruct((m, kvh * gw), BF16),
                   jax.ShapeDtypeStruct((nb, kvh, rows, t), F32)],
        compiler_params=_params(("parallel", "parallel", "parallel")),
        name="cmp_attn",
    )(q, kvc, kvc, agg_t)


def _select_kernel(imp_ref, *rest, n_slc, pos0, t_len, emit_idx):
    if emit_idx:
        idx_ref, ok_ref, sc_ref = rest
    else:
        bias_ref, sc_ref = rest
    rows, width = imp_ref.shape
    lane = pl.program_id(0) * width + lax.broadcasted_iota(jnp.int32, (1, width), 1)
    qpos = pos0 + jnp.bitwise_and(lane, t_len - 1)
    blk = lax.broadcasted_iota(jnp.int32, (rows, width), 0)
    cur = jnp.right_shift(qpos, SLC_BLOCK.bit_length() - 1)
    visible = blk * SLC_BLOCK <= qpos
    forced = (blk == 0) | (blk == cur) | (blk == cur - 1)
    score = jnp.where(visible, jnp.where(forced, FORCE, imp_ref[...]), NEG)
    sc_ref[...] = score

    def count(i, cnt):
        si = sc_ref[pl.ds(i, 1), :]
        beats = (si > score) | ((si == score) & (i < blk))
        return cnt + jnp.where(beats, 1, 0)

    rank = lax.fori_loop(0, n_slc, count, jnp.zeros((rows, width), jnp.int32))
    in_range = blk < n_slc
    if emit_idx:
        blk_f = blk.astype(F32)
        for r in range(idx_ref.shape[0]):
            hit = (rank == r) & in_range
            idx_r = jnp.sum(jnp.where(hit, blk_f, 0.0), axis=0, keepdims=True)
            idx_ref[r:r + 1, :] = idx_r.astype(jnp.int32)
            ok_ref[r:r + 1, :] = jnp.sum(jnp.where(hit & visible, 1.0, 0.0), axis=0, keepdims=True)
    else:
        chosen = (rank < N_SELECT) & visible & in_range
        bias_ref[...] = jnp.where(chosen, 0.0, NEG).astype(bias_ref.dtype)


def select_blocks(imp, *, n_slc, pos0, emit_idx):
    nb, kvh, rows, t = imp.shape
    assert t & (t - 1) == 0
    lanes = nb * kvh * t
    imp2 = imp.transpose(2, 0, 1, 3).reshape(rows, lanes)
    width = _tile(lanes, 2048)
    spec = pl.BlockSpec((rows, width), lambda i: (0, i))
    if emit_idx:
        nsel = min(N_SELECT, n_slc)
        o_spec = pl.BlockSpec((nsel, width), lambda i: (0, i))
        out_specs = [o_spec, o_spec]
        out_shape = [jax.ShapeDtypeStruct((nsel, lanes), jnp.int32), jax.ShapeDtypeStruct((nsel, lanes), F32)]
    else:
        out_specs = [spec]
        out_shape = [jax.ShapeDtypeStruct((rows, lanes), BF16)]
    outs = pl.pallas_call(
        functools.partial(_select_kernel, n_slc=n_slc, pos0=pos0, t_len=t, emit_idx=emit_idx),
        grid=(lanes // width,),
        in_specs=[spec], out_specs=out_specs, out_shape=out_shape,
        scratch_shapes=[pltpu.VMEM((rows, width), F32)],
        compiler_params=_params(("parallel",)),
        name="select_blocks",
    )(imp2)
    if emit_idx:
        return tuple(o.reshape(-1, nb, kvh, t).transpose(1, 2, 0, 3) for o in outs)
    return outs[0].reshape(rows, nb, kvh, t).transpose(1, 2, 3, 0)


V_ROWS_PAD = 2 * SUBLANES


def _prompt_attn_kernel(q_ref, bias_ref, ks_ref, vts_ref, kw_ref, vtw_ref, oc_ref, gc_ref, gs_ref, gw_ref,
                        o_ref, *, nq, hd):
    qi = pl.program_id(2)
    hs = pl.program_id(3)
    tq = q_ref.shape[0]
    n_heads = q_ref.shape[1] // hd
    kl = lax.broadcasted_iota(jnp.int32, (tq, tq), 0)
    ql = lax.broadcasted_iota(jnp.int32, (tq, tq), 1)

    def branch(k_ref, vt_ref, q, k0, nkeys, band):
        st = _dot_nt(k_ref[k0:k0 + nkeys, :], q)
        parts = []
        if nkeys > tq:
            top = st[:nkeys - tq, :]
            if band:
                top = jnp.where(kl > ql, top, NEG)
            parts.append(top)
        parts.append(jnp.where(kl <= ql, st[nkeys - tq:, :], NEG))
        st = jnp.concatenate(parts, axis=0) if len(parts) > 1 else parts[0]
        p = jnp.exp2(st - jnp.max(st, axis=0, keepdims=True)).astype(BF16)
        ot = _dot(vt_ref[0, 0, :, k0:k0 + nkeys], p)
        return ot[:hd, :] / ot[hd:hd + 1, :]

    def run(v):
        for hp in range(n_heads):
            sl = slice(hp * hd, (hp + 1) * hd)
            row = pl.ds(hs * n_heads + hp, 1)
            q = (q_ref[:, sl].astype(F32) * (hd ** -0.5 * math.log2(math.e))).astype(BF16)
            o_s = branch(ks_ref, vts_ref, jnp.concatenate([q, bias_ref[0, 0]], axis=1), 0, (v + 1) * tq, False)
            o_w = branch(kw_ref, vtw_ref, q, max(v - 1, 0) * tq, min(v + 1, 2) * tq, v > 0)
            o_c = oc_ref[:, sl].astype(F32).T
            o_t = o_c * gc_ref[row, :] + o_s * gs_ref[row, :] + o_w * gw_ref[row, :]
            o_ref[:, sl] = o_t.T.astype(o_ref.dtype)

    for v in range(nq):
        pl.when(qi == v)(functools.partial(run, v))


def prompt_attn(q, o_cmp, gates_t, bias, k_slc, vt_slc, k_win, vt_win, *, nb, kvh, group, hd, tq,
                heads_per_step=4):
    m = q.shape[0]
    t = m // nb
    tq = _tile(t, tq)
    nq = t // tq
    assert tq == WINDOW or nq == 1
    hw = heads_per_step * hd
    hsteps = group // heads_per_step
    head_spec = pl.BlockSpec((tq, hw), lambda b, g, i, h: (b * nq + i, g * hsteps + h))
    vt_spec = pl.BlockSpec((1, 1, hd + V_ROWS_PAD, t), lambda b, g, i, h: (b, g, 0, 0))

    def gate_spec(branch):
        return pl.BlockSpec((group, tq), lambda b, g, i, h: (branch * kvh + g, b * nq + i))

    return pl.pallas_call(
        functools.partial(_prompt_attn_kernel, nq=nq, hd=hd),
        grid=(nb, kvh, nq, hsteps),
        in_specs=[head_spec,
                  pl.BlockSpec((1, 1, tq, hd), lambda b, g, i, h: (b, g, i, 0)),
                  pl.BlockSpec((t, 2 * hd), lambda b, g, i, h: (b, g)), vt_spec,
                  pl.BlockSpec((t, hd), lambda b, g, i, h: (b, g)), vt_spec,
                  head_spec, gate_spec(0), gate_spec(1), gate_spec(2)],
        out_specs=head_spec,
        out_shape=jax.ShapeDtypeStruct((m, kvh * group * hd), BF16),
        compiler_params=_params(("parallel", "parallel", "parallel", "parallel")),
        name="prompt_attn",
    )(q, bias, k_slc, vt_slc, k_win, vt_win, o_cmp, gates_t, gates_t, gates_t)


def _sample_attn_kernel(idx_ref, pt_ref, q_ref, kpos_ref, okx_ref, kn_ref, vn_ref, wk_ref, wv_ref,
                        wkn_ref, wvn_ref, cache_ref, oslc_ref, owin_ref, kbuf, vbuf, sem,
                        *, kvh, group, hd, t_valid, past, nsel, ks_sec, vs_sec):
    b = pl.program_id(0)
    g = pl.program_id(1)
    tp = q_ref.shape[2] // group
    n_blocks = past // SLC_BLOCK
    per_page = PAGE_SIZE // SLC_BLOCK
    n_fetch = t_valid * nsel

    def copies(f):
        t = f // nsel
        r = f % nsel
        blk = idx_ref[((b * kvh + g) * nsel + r) * tp + t]
        blk = jnp.minimum(blk, n_blocks - 1)
        page = pt_ref[b * (past // PAGE_SIZE) + blk // per_page]
        row0 = (blk % per_page) * SLC_BLOCK
        dst = pl.ds(f * SLC_BLOCK, SLC_BLOCK)
        ck = pltpu.make_async_copy(cache_ref.at[page, pl.ds(row0, SLC_BLOCK), ks_sec, g, :],
                                   kbuf.at[dst, :], sem.at[0])
        cv = pltpu.make_async_copy(cache_ref.at[page, pl.ds(row0, SLC_BLOCK), vs_sec, g, :],
                                   vbuf.at[dst, :], sem.at[1])
        return ck, cv

    def start(f, c):
        ck, cv = copies(f)
        ck.start()
        cv.start()
        return c

    def wait(f, c):
        ck, cv = copies(f)
        ck.wait()
        cv.wait()
        return c

    lax.fori_loop(0, n_fetch, start, 0)

    scale = hd ** -0.5
    qa = q_ref[0, 0].astype(BF16)
    trow = jnp.right_shift(lax.broadcasted_iota(jnp.int32, (tp * group, 1), 0), group.bit_length() - 1)
    wk = wk_ref[0].astype(BF16)
    wrows = wk.shape[0]
    s_c = _dot_nt(qa, wk) * scale
    s_n = _dot_nt(qa, wkn_ref[...].astype(BF16)) * scale
    d_c = (past + trow) - (past - wrows + lax.broadcasted_iota(jnp.int32, (1, wrows), 1))
    jn = lax.broadcasted_iota(jnp.int32, (1, tp), 1)
    d_n = trow - jn
    m_c = (d_c >= 0) & (d_c < WINDOW)
    m_n = (d_n >= 0) & (d_n < WINDOW) & (jn < t_valid)
    s_c = jnp.where(m_c, s_c, NEG)
    s_n = jnp.where(m_n, s_n, NEG)
    mx = jnp.maximum(jnp.max(s_c, axis=-1, keepdims=True), jnp.max(s_n, axis=-1, keepdims=True))
    e_c = jnp.where(m_c, jnp.exp(s_c - mx), 0.0)
    e_n = jnp.where(m_n, jnp.exp(s_n - mx), 0.0)
    den = jnp.maximum(jnp.sum(e_c, axis=-1, keepdims=True) + jnp.sum(e_n, axis=-1, keepdims=True), TINY)
    o_w = _dot((e_c / den).astype(BF16), wv_ref[0].astype(BF16)) + \
        _dot((e_n / den).astype(BF16), wvn_ref[...].astype(BF16))
    owin_ref[0, 0] = o_w.astype(owin_ref.dtype)

    lax.fori_loop(0, n_fetch, wait, 0)

    kn = kn_ref[...].astype(BF16)
    vn = vn_ref[...].astype(BF16)
    span = nsel * SLC_BLOCK
    oslc_ref[...] = jnp.zeros(oslc_ref.shape, oslc_ref.dtype)
    for t in range(t_valid):
        qt = q_ref[0, 0, t * group:(t + 1) * group, :].astype(BF16)
        kt = kbuf[t * span:(t + 1) * span, :].astype(BF16)
        vt = vbuf[t * span:(t + 1) * span, :].astype(BF16)
        kp = kpos_ref[0, 0, t:t + 1, :]
        okv = okx_ref[0, 0, t:t + 1, :] > 0.5
        qpos = past + t
        m_g = (kp <= qpos) & okv & (kp < past)
        s_g = jnp.where(m_g, _dot_nt(qt, kt) * scale, NEG)
        selrow = jnp.zeros((1, tp), F32)
        for j in range(t_valid):
            hit = jnp.max(jnp.where((kp == past + j) & okv, 1.0, 0.0), axis=-1, keepdims=True)
            selrow = selrow + jnp.where(jn == j, hit, 0.0)
        m_w = (selrow > 0.5) & (jn <= t)
        s_w = jnp.where(m_w, _dot_nt(qt, kn) * scale, NEG)
        mx = jnp.maximum(jnp.max(s_g, axis=-1, keepdims=True), jnp.max(s_w, axis=-1, keepdims=True))
        e_g = jnp.where(m_g, jnp.exp(s_g - mx), 0.0)
        e_w = jnp.where(m_w, jnp.exp(s_w - mx), 0.0)
        den = jnp.sum(e_g, axis=-1, keepdims=True) + jnp.sum(e_w, axis=-1, keepdims=True)
        o_t = _dot((e_g / den).astype(BF16), vt) + _dot((e_w / den).astype(BF16), vn)
        oslc_ref[0, 0, t * group:(t + 1) * group, :] = o_t.astype(oslc_ref.dtype)


def sample_attn(q4, idx, okf, kv_new, cache_kv4, cache_win3, page_table, *, kvh, group, hd, t_valid, past):
    nb = q4.shape[0]
    tp = q4.shape[2] // group
    nsel = idx.shape[2]
    span = nsel * SLC_BLOCK
    wrows = cache_win3.shape[1]
    idx_t = idx.transpose(0, 1, 3, 2)
    kpos = (idx_t[..., None] * SLC_BLOCK + jnp.arange(SLC_BLOCK, dtype=jnp.int32)).reshape(nb, kvh, tp, span)
    okx = jnp.broadcast_to(okf.transpose(0, 1, 3, 2)[..., None], (nb, kvh, tp, nsel, SLC_BLOCK)).reshape(
        nb, kvh, tp, span)
    grid_spec = pltpu.PrefetchScalarGridSpec(
        num_scalar_prefetch=2,
        grid=(nb, kvh),
        in_specs=[
            pl.BlockSpec((1, 1, tp * group, hd), lambda b, g, *_: (b, g, 0, 0)),
            pl.BlockSpec((1, 1, tp, span), lambda b, g, *_: (b, g, 0, 0)),
            pl.BlockSpec((1, 1, tp, span), lambda b, g, *_: (b, g, 0, 0)),
            pl.BlockSpec((tp, hd), lambda b, g, *_: (b, 2 * kvh + g)),
            pl.BlockSpec((tp, hd), lambda b, g, *_: (b, 3 * kvh + g)),
            pl.BlockSpec((1, wrows, hd), lambda b, g, *_: (b, 0, g)),
            pl.BlockSpec((1, wrows, hd), lambda b, g, *_: (b, 0, kvh + g)),
            pl.BlockSpec((tp, hd), lambda b, g, *_: (b, 4 * kvh + g)),
            pl.BlockSpec((tp, hd), lambda b, g, *_: (b, 5 * kvh + g)),
            pl.BlockSpec(memory_space=pl.ANY),
        ],
        out_specs=[pl.BlockSpec((1, 1, tp * group, hd), lambda b, g, *_: (b, g, 0, 0)),
                   pl.BlockSpec((1, 1, tp * group, hd), lambda b, g, *_: (b, g, 0, 0))],
        scratch_shapes=[pltpu.VMEM((t_valid * span, hd), F32),
                        pltpu.VMEM((t_valid * span, hd), F32),
                        pltpu.SemaphoreType.DMA((2,))],
    )
    return pl.pallas_call(
        functools.partial(_sample_attn_kernel, kvh=kvh, group=group, hd=hd, t_valid=t_valid, past=past,
                          nsel=nsel, ks_sec=2, vs_sec=3),
        grid_spec=grid_spec,
        out_shape=[jax.ShapeDtypeStruct(q4.shape, F32), jax.ShapeDtypeStruct(q4.shape, F32)],
        compiler_params=_params(("arbitrary", "arbitrary")),
        name="sample_attn",
    )(idx.reshape(-1), page_table.reshape(-1), q4, kpos, okx, kv_new, kv_new, cache_win3, cache_win3,
      kv_new, kv_new, cache_kv4)


def _combine_kernel(oc_ref, os_ref, ow_ref, g_ref, o_ref, *, heads, hd):
    gates = g_ref[...]
    for h in range(heads):
        sl = slice(h * hd, (h + 1) * hd)
        o = (gates[:, h:h + 1] * oc_ref[:, sl].astype(F32)
             + gates[:, heads + h:heads + h + 1] * os_ref[:, sl].astype(F32)
             + gates[:, 2 * heads + h:2 * heads + h + 1] * ow_ref[:, sl].astype(F32))
        o_ref[:, sl] = o.astype(o_ref.dtype)


def combine(o_cmp, o_slc, o_win, gates, heads, hd):
    m, d = o_cmp.shape
    tm = _tile(m, 256)
    spec = pl.BlockSpec((tm, d), lambda i: (i, 0))
    return pl.pallas_call(
        functools.partial(_combine_kernel, heads=heads, hd=hd),
        grid=(m // tm,),
        in_specs=[spec, spec, spec, pl.BlockSpec((tm, 3 * heads), lambda i: (i, 0))],
        out_specs=spec,
        out_shape=jax.ShapeDtypeStruct((m, d), BF16),
        compiler_params=_params(("parallel",)),
        name="combine",
    )(o_cmp, o_slc, o_win, gates)


def _rope_tables_half(pos, half, theta):
    inv = jnp.power(theta, -jnp.arange(half, dtype=F32) / half)
    ang = pos.astype(F32)[:, None] * inv[None, :]
    return jnp.cos(ang), jnp.sin(ang)


def _nsa_rope_tables(pos, hd):
    half = hd // 8
    cos, sin = _rope_tables_half(pos, half, ROPE_THETA)
    n = pos.shape[0]
    ones = jnp.ones((n, hd - 2 * half), F32)
    zeros = jnp.zeros((n, hd - 2 * half), F32)
    zh = jnp.zeros((n, half), F32)
    c = jnp.concatenate([cos, cos, ones], axis=1)
    s_up = jnp.concatenate([-sin, zh, zeros], axis=1)
    s_dn = jnp.concatenate([zh, sin, zeros], axis=1)
    return c, s_up, s_dn


class _Stream:
    def __init__(self, x, nb, t_rows, t_valid, pos0, ret_s0, conv_state, ctx):
        self.x, self.nb, self.t_rows, self.t_valid, self.pos0 = x, nb, t_rows, t_valid, pos0
        self.ret_s0, self.conv_state, self.ctx = ret_s0, conv_state, ctx
        self.m = nb * t_rows
        self.pos = pos0 + jnp.arange(t_rows)


def _nsa_branches(st, q, gates, kv32, kv16, w_pairs, pe_term, w, *, hd, kvh, group):
    nb, t_rows, t_valid, pos0, m = st.nb, st.t_rows, st.t_valid, st.pos0, st.m
    qw = kvh * group * hd
    win_new = kv32[:, 4 * kvh * hd:].reshape(nb, t_rows, 2, kvh, hd)[:, :t_valid]
    if st.ctx is None:
        p = cmp_proj_rows(kv32, nb, t_rows, w_pairs, kvh, hd)
        kvc = cmp_finish(p, pe_term, w["cmp_w2"], kvh)
        n_seg = t_rows // CMP_STRIDE
        n_c = n_seg - CMP_LEN // CMP_STRIDE + 1
        n_slc = -(-t_rows // SLC_BLOCK)
        o_cmp, imp = cmp_attn(q, kvc, nb=nb, kvh=kvh, group=group, hd=hd, n_c=n_c, n_slc=n_slc,
                              pos0=pos0, tq=512)
        bias = select_blocks(imp, n_slc=n_slc, pos0=pos0, emit_idx=False)
        bias = jnp.pad(bias, ((0, 0), (0, 0), (0, 0), (0, hd - bias.shape[-1])))
        kv5 = kv16.reshape(nb, t_rows, 6, kvh, hd)
        onehot = (jnp.arange(t_rows)[:, None] // SLC_BLOCK == jnp.arange(hd)[None, :]).astype(BF16)
        k_slc = jnp.concatenate([kv5[:, :, 2], jnp.broadcast_to(onehot[None, :, None, :], (nb, t_rows, kvh, hd))],
                                axis=-1).reshape(m, kvh * 2 * hd)
        k_win = kv5[:, :, 4].reshape(m, kvh * hd)
        ones_rows = jnp.zeros((nb, kvh, V_ROWS_PAD, t_rows), BF16).at[:, :, 0].set(1.0)

        def v_t(sec):
            return jnp.concatenate([kv5[:, :, sec].transpose(0, 2, 3, 1), ones_rows], axis=2)

        o = prompt_attn(q, o_cmp, gates.T, bias, k_slc, v_t(3), k_win, v_t(5), nb=nb, kvh=kvh, group=group, hd=hd,
                        tq=WINDOW)
        win = win_new[:, t_valid - min(WINDOW, t_valid):]
    else:
        cache_kv, cache_win, page_table = st.ctx
        past = pos0
        p = cmp_proj_pages(cache_kv, page_table, w_pairs, kvh, hd, pages_per_step=min(16, page_table.shape[1]))
        kvc = cmp_finish(p, pe_term, w["cmp_w2"], kvh)
        n_seg = (past + t_valid) // CMP_STRIDE
        n_c = n_seg - CMP_LEN // CMP_STRIDE + 1
        n_slc = -(-(past + t_valid) // SLC_BLOCK)
        o_cmp, imp = cmp_attn(q, kvc, nb=nb, kvh=kvh, group=group, hd=hd, n_c=n_c, n_slc=n_slc,
                              pos0=pos0, tq=t_rows)
        idx, okf = select_blocks(imp, n_slc=n_slc, pos0=pos0, emit_idx=True)
        q4 = q.astype(F32).reshape(nb, t_rows, kvh, group, hd).transpose(0, 2, 1, 3, 4).reshape(
            nb, kvh, t_rows * group, hd)
        wrows = cache_win.shape[1]
        o_slc4, o_win4 = sample_attn(q4, idx, okf, kv32, cache_kv,
                                     cache_win.reshape(nb, wrows, 2 * kvh * hd), page_table,
                                     kvh=kvh, group=group, hd=hd, t_valid=t_valid, past=past)

        def rows(o4):
            return o4.reshape(nb, kvh, t_rows, group, hd).transpose(0, 2, 1, 3, 4).reshape(m, qw).astype(BF16)

        win = jnp.concatenate([cache_win, win_new], axis=1)[:, t_valid:]
        o = combine(o_cmp, rows(o_slc4), rows(o_win4), gates, kvh * group, hd)
    return o, win


def _trunk(sp, ss, w):
    d = sp.x.shape[1]
    streams = (sp, ss)
    tm = 1024
    d_ff = w["ffn_conv_w"].shape[-1]
    heads_r = RET_HEADS
    dk = d // heads_r
    dv = 2 * dk
    hd = d // NSA_HEADS
    kvh = NSA_KV_HEADS
    group = NSA_HEADS // kvh
    qw = NSA_HEADS * hd
    rope_half = hd // 8

    def norm(xs, g, dt=BF16):
        return [rmsnorm(x, g, dt) for x in xs]

    def ffn(xs, layer):
        hs = norm(xs, w["norm_ffn"][layer])
        st = ss.conv_state[layer]
        f1 = jnp.zeros((ss.nb, ss.t_rows, d_ff), F32).at[:, 0].set(st[:, 1])
        f2 = jnp.zeros((ss.nb, ss.t_rows, d_ff), F32).at[:, 0].set(st[:, 0]).at[:, 1].set(st[:, 1])
        (a_p, tail_p), (a_s, tail_s), w_down16 = ffn_up(
            hs[0], hs[1], w["ffn_w_up"], w["ffn_w_down"], layer, d_ff, w["ffn_conv_w"][layer],
            w["ffn_conv_b"][layer], tm=tm,
            state=sp.conv_state[layer], fill=(f1.reshape(ss.m, d_ff), f2.reshape(ss.m, d_ff)), seg=ss.t_rows)
        cs = [tail_p[:, SUBLANES - 2:],
              tail_s.reshape(ss.nb, ss.t_rows, d_ff)[:, ss.t_valid - 2:ss.t_valid]]
        ys = [matmul(a, w_down16, d, tm=512, tn=512, tk=d_ff, out_dtypes=[F32], res=x)
              for a, x in zip((a_p, a_s), xs)]
        return ys, cs

    xs = [sp.x, ss.x]
    hs = norm(xs, w["norm_mix"][0])
    ret_w_out16 = w["ret_w_out"].astype(BF16)
    (proj_p,), (proj_s,) = matmul_ws(hs[0], hs[1], w["ret_w_in"], w["ret_w_in"].shape[-1], tm=512, tn=1024,
                                     out_dtypes=[BF16])
    o_ret, s_ret = [], []
    for st, proj in zip(streams, (proj_p, proj_s)):
        chunk = math.gcd(st.t_valid, RET_CHUNK)
        cpad = chunk if st.ctx is None else st.t_rows
        cos_r, sin_r = _rope_tables_half(st.pos, dk // 2, RET_THETA)
        o, s = retention(proj, st.ret_s0, cos_r, sin_r, nb=st.nb, heads=heads_r, dk=dk, dv=dv,
                         chunk=cpad, c_valid=chunk)
        o_ret.append(o)
        s_ret.append(s)
    xs = [matmul(o, ret_w_out16, d, tm=512, tn=512, tk=heads_r * dv, out_dtypes=[F32], res=x)
          for o, x in zip(o_ret, xs)]
    xs, conv0 = ffn(xs, 0)

    hk = norm(xs, w["kv_norm"])
    rope_p = _nsa_rope_tables(sp.pos, hd)
    rope_s = tuple(jnp.tile(tb, (ss.nb, 1)) for tb in _nsa_rope_tables(ss.pos, hd))
    (kv32_p, kv16_p), (kv32_s, kv16_s) = matmul_ws(
        hk[0], hk[1], w["kv_w"], 6 * kvh * hd, tm=tm, tn=kvh * hd, out_dtypes=[F32, BF16],
        rope=(rope_p, rope_s), rope_every=2, rope_half=rope_half)
    kv32, kv16 = (kv32_p, kv32_s), (kv16_p, kv16_s)
    kv_rows = [kv[:, :4 * kvh * hd].reshape(st.nb, st.t_rows, 4, kvh, hd)[:, :st.t_valid]
               for st, kv in zip(streams, kv32)]

    hs = norm(xs, w["norm_mix"][1])
    (q_p,), (q_s,) = matmul_ws(hs[0], hs[1], w["nsa_w_in"], qw, tm=512, tn=1024, out_dtypes=[BF16],
                               rope=(rope_p, rope_s), rope_every=1, rope_half=rope_half)
    w_pairs = _cmp_w1_pairs(w["cmp_w1"], hd)
    pe_term = cmp_pe_term(w["cmp_pe"], w["cmp_w1"])
    os, wins = [], []
    for st, h, q, k32, k16 in zip(streams, hs, (q_p, q_s), kv32, kv16):
        gates = matmul(h, w["nsa_w_gate16"], 3 * NSA_HEADS, tm=tm, tn=3 * NSA_HEADS, tk=d, out_dtypes=[F32],
                       act="sigmoid")
        o, win = _nsa_branches(st, q, gates, k32, k16, w_pairs, pe_term, w, hd=hd, kvh=kvh, group=group)
        os.append(o)
        wins.append(win)
    (x_p,), (x_s,) = matmul_ws(os[0], os[1], w["nsa_w_out"], d, tm=tm, tn=512, out_dtypes=[F32], res=xs)
    xs, conv1 = ffn([x_p, x_s], 1)
    ys = norm(xs, w["norm_final"], F32)
    convs = [jnp.stack([c0, c1]) for c0, c1 in zip(conv0, conv1)]
    return [(y, s[None], cv, kvr, win) for y, s, cv, kvr, win in zip(ys, s_ret, convs, kv_rows, wins)]


def kernel(x_prompt, x_sample, cache_kv, cache_win, state_ret, state_conv, page_table, norm_mix, norm_ffn,
           ret_w_in, ret_w_out, kv_norm, kv_w, cmp_pe, cmp_w1, cmp_w2, nsa_w_in, nsa_w_out, ffn_w_up, ffn_conv_w,
           ffn_conv_b, ffn_w_down, norm_final):
    b, t, d = x_prompt.shape
    db, dt, _ = x_sample.shape
    past = page_table.shape[1] * cache_kv.shape[1]
    qw = nsa_w_out.shape[1]
    w = dict(
        norm_mix=norm_mix, norm_ffn=norm_ffn, kv_norm=kv_norm, norm_final=norm_final,
        cmp_pe=cmp_pe, cmp_w1=cmp_w1, cmp_w2=cmp_w2, ffn_conv_w=ffn_conv_w, ffn_conv_b=ffn_conv_b,
        ret_w_in=ret_w_in[0], kv_w=kv_w, nsa_w_in=nsa_w_in[0], nsa_w_out=nsa_w_out[0], ffn_w_up=ffn_w_up,
        ret_w_out=ret_w_out[0], ffn_w_down=ffn_w_down, nsa_w_gate16=nsa_w_in[0][:, qw:].astype(BF16),
    )
    heads_r = RET_HEADS
    dk = d // heads_r
    ret0 = jnp.zeros((b, heads_r, dk, 2 * dk), F32)
    conv0 = jnp.zeros((state_conv.shape[0], b, 2, ffn_conv_w.shape[-1]), F32)
    tp = 2 * SUBLANES
    xs = jnp.zeros((db, tp, d), F32).at[:, :dt].set(x_sample).reshape(db * tp, d)
    sp = _Stream(x_prompt.reshape(b * t, d), b, t, t, 0, ret0, conv0, None)
    ss = _Stream(xs, db, tp, dt, past, state_ret[0], state_conv, (cache_kv, cache_win, page_table))
    (y_p, ret_p, conv_p, kv_p, win_p), (y_s, ret_s, conv_s, kv_s, win_s) = _trunk(sp, ss, w)
    y_s = y_s.reshape(db, tp, d)[:, :dt]
    return (y_p.reshape(b, t, d), y_s, kv_p, kv_s, win_p, win_s, ret_p, ret_s, conv_p, conv_s)
```

```python
import functools
import math

import jax
import jax.numpy as jnp
from jax import lax
from jax.experimental import pallas as pl
from jax.experimental.pallas import tpu as pltpu

F32 = jnp.float32
BF16 = jnp.bfloat16

RET_HEADS = 16
RET_CHUNK = 512
RET_THETA = 10000.0
NSA_HEADS = 32
NSA_KV_HEADS = 4
CMP_LEN = 32
CMP_STRIDE = 16
SLC_BLOCK = 64
N_SELECT = 16
WINDOW = 512
ROPE_THETA = 500000.0
PAGE_SIZE = 128
EPS = 1e-6
NEG = -1e30
FORCE = 1e9
TINY = 1e-20

LANES = 128
SUBLANES = 8
VMEM_LIMIT = 60 * 1024 * 1024


def _params(sem):
    return pltpu.CompilerParams(dimension_semantics=sem, vmem_limit_bytes=VMEM_LIMIT)


def _tile(dim, pref):
    if dim <= pref:
        return dim
    t = pref
    while dim % t:
        t //= 2
    return t


def _sigmoid(x):
    return 1.0 / (1.0 + jnp.exp(-x))


def _dot(a, b):
    return jnp.dot(a, b, preferred_element_type=F32)


def _dot_nt(a, b):
    return lax.dot_general(a, b, (((1,), (1,)), ((), ())), preferred_element_type=F32)


def _rmsnorm_kernel(x_ref, g_ref, o_ref):
    x = x_ref[...]
    y = x * lax.rsqrt(jnp.mean(x * x, axis=-1, keepdims=True) + EPS)
    o_ref[...] = (y * g_ref[...]).astype(o_ref.dtype)


def rmsnorm(x, g, out_dtype):
    m, d = x.shape
    tm = _tile(m, 256)
    return pl.pallas_call(
        _rmsnorm_kernel,
        grid=(m // tm,),
        in_specs=[pl.BlockSpec((tm, d), lambda i: (i, 0)),
                  pl.BlockSpec((1, d), lambda i: (0, 0))],
        out_specs=pl.BlockSpec((tm, d), lambda i: (i, 0)),
        out_shape=jax.ShapeDtypeStruct((m, d), out_dtype),
        compiler_params=_params(("parallel",)),
        name="rmsnorm",
    )(x, g.reshape(1, d).astype(F32))


def _rope_lanes(x, c, s_up, s_dn, half):
    return x * c + pltpu.roll(x, LANES - half, 1) * s_up + pltpu.roll(x, half, 1) * s_dn


def _mm_epilogue(acc, j, res_ref, rope_refs, out_refs, *, rope_every, rope_half, act):
    def store(val):
        for o in out_refs:
            o[...] = val.astype(o.dtype)

    if res_ref is not None:
        acc = acc + res_ref[...]
    if act == "sigmoid":
        acc = _sigmoid(acc)
    if not rope_every:
        store(acc)
        return
    tn = acc.shape[1]

    def roped():
        c, su, sd = (r[...] for r in rope_refs)
        pieces = [_rope_lanes(acc[:, a:a + LANES], c, su, sd, rope_half) for a in range(0, tn, LANES)]
        store(jnp.concatenate(pieces, axis=1))

    if rope_every == 1:
        roped()
    else:
        pl.when(j % rope_every == 0)(roped)
        pl.when(j % rope_every != 0)(lambda: store(acc))


def _mm_kernel(*refs, nk, has_res, rope_every, rope_half, act, n_out):
    x_ref, w_ref = refs[0], refs[1]
    pos = 2
    res_ref = None
    if has_res:
        res_ref = refs[pos]
        pos += 1
    rope_refs = None
    if rope_every:
        rope_refs = refs[pos:pos + 3]
        pos += 3
    out_refs = refs[pos:pos + n_out]
    acc_ref = refs[pos + n_out] if nk > 1 else None
    j = pl.program_id(1)
    k = pl.program_id(2)

    part = _dot(x_ref[...], w_ref[...].astype(BF16))
    finish = functools.partial(_mm_epilogue, j=j, res_ref=res_ref, rope_refs=rope_refs, out_refs=out_refs,
                               rope_every=rope_every, rope_half=rope_half, act=act)

    if nk == 1:
        finish(part)
    else:
        @pl.when(k == 0)
        def _():
            acc_ref[...] = part

        @pl.when(k > 0)
        def _():
            acc_ref[...] += part

        @pl.when(k == nk - 1)
        def _():
            finish(acc_ref[...])


def matmul(x, w, n, *, tm, tn, tk, out_dtypes, res=None, rope=None, rope_every=0, rope_half=0, act=None):
    m, kdim = x.shape
    tm, tn, tk = _tile(m, tm), _tile(n, tn), _tile(kdim, tk)
    nk = kdim // tk
    in_specs = [pl.BlockSpec((tm, tk), lambda i, j, k: (i, k)),
                pl.BlockSpec((tk, tn), lambda i, j, k: (k, j))]
    args = [x, w]
    if res is not None:
        in_specs.append(pl.BlockSpec((tm, tn), lambda i, j, k: (i, j)))
        args.append(res)
    if rope is not None:
        period = rope[0].shape[0]
        nper = period // tm
        for t in rope:
            in_specs.append(pl.BlockSpec((tm, LANES), lambda i, j, k: (i % nper, 0)))
            args.append(t)
    out_specs = [pl.BlockSpec((tm, tn), lambda i, j, k: (i, j)) for _ in out_dtypes]
    out_shape = [jax.ShapeDtypeStruct((m, n), dt) for dt in out_dtypes]
    scratch = [pltpu.VMEM((tm, tn), F32)] if nk > 1 else []
    outs = pl.pallas_call(
        functools.partial(_mm_kernel, nk=nk, has_res=res is not None,
                          rope_every=rope_every if rope is not None else 0, rope_half=rope_half,
                          act=act, n_out=len(out_dtypes)),
        grid=(m // tm, n // tn, nk),
        in_specs=in_specs, out_specs=out_specs, out_shape=out_shape,
        scratch_shapes=scratch,
        compiler_params=_params(("parallel", "parallel", "arbitrary")),
        name="matmul",
    )(*args)
    return outs if len(outs) > 1 else outs[0]


def _round_rows_job(j, i, src_ref, dst_ref, nblk):
    @pl.when((i == 0) & (j < nblk))
    def _():
        dst_ref[...] = src_ref[...].astype(BF16)


def _round_rows_specs(src, nblk, layer=None):
    rows, cols = src.shape[-2:]
    rb = rows // nblk
    if layer is None:
        in_spec = pl.BlockSpec((rb, cols), lambda j, i: (jnp.minimum(j, nblk - 1), 0))
    else:
        in_spec = pl.BlockSpec((None, rb, cols), lambda j, i: (layer, jnp.minimum(j, nblk - 1), 0))
    out_spec = pl.BlockSpec((rb, cols), lambda j, i: (jnp.minimum(j, nblk - 1), 0))
    return in_spec, out_spec, jax.ShapeDtypeStruct((rows, cols), BF16)


def _mm_ws_kernel(*refs, has_res, rope_every, rope_half, n_out, round_blocks):
    x_ref, xs_ref, w_ref = refs[:3]
    pos = 3
    if round_blocks:
        rsrc_ref = refs[pos]
        pos += 1
    res_ref = ress_ref = rope_refs = ropes_refs = None
    if has_res:
        res_ref, ress_ref = refs[pos:pos + 2]
        pos += 2
    if rope_every:
        rope_refs, ropes_refs = refs[pos:pos + 3], refs[pos + 3:pos + 6]
        pos += 6
    out_refs = refs[pos:pos + n_out]
    outs_refs = refs[pos + n_out:pos + 2 * n_out]
    pos += 2 * n_out
    j = pl.program_id(0)
    i = pl.program_id(1)
    if round_blocks:
        _round_rows_job(j, i, rsrc_ref, refs[pos], round_blocks)
        pos += 1
    w16 = refs[pos]

    @pl.when(i == 0)
    def _():
        w16[...] = w_ref[...].astype(BF16)

    kw = dict(rope_every=rope_every, rope_half=rope_half, act=None)
    _mm_epilogue(_dot(x_ref[...], w16[...]), j, res_ref, rope_refs, out_refs, **kw)

    @pl.when(i == pl.num_programs(1) - 1)
    def _():
        _mm_epilogue(_dot(xs_ref[...], w16[...]), j, ress_ref, ropes_refs, outs_refs, **kw)


def matmul_ws(x, xs, w, n, *, tm, tn, out_dtypes, res=None, rope=None, rope_every=0, rope_half=0,
              round_rows=None):
    m, kdim = x.shape
    ms = xs.shape[0]
    tm, tn = _tile(m, tm), _tile(n, tn)
    in_specs = [pl.BlockSpec((tm, kdim), lambda j, i: (i, 0)),
                pl.BlockSpec((ms, kdim), lambda j, i: (0, 0)),
                pl.BlockSpec((kdim, tn), lambda j, i: (0, j))]
    args = [x, xs, w]
    round_blocks = 0
    if round_rows is not None:
        round_blocks = round_rows[1]
        assert round_blocks <= n // tn
        r_in, r_out, r_shape = _round_rows_specs(*round_rows)
        in_specs.append(r_in)
        args.append(round_rows[0])
    if res is not None:
        in_specs += [pl.BlockSpec((tm, tn), lambda j, i: (i, j)), pl.BlockSpec((ms, tn), lambda j, i: (0, j))]
        args += list(res)
    if rope is not None:
        nper = rope[0][0].shape[0] // tm
        in_specs += [pl.BlockSpec((tm, LANES), lambda j, i: (i % nper, 0))] * 3
        in_specs += [pl.BlockSpec((ms, LANES), lambda j, i: (0, 0))] * 3
        args += list(rope[0]) + list(rope[1])
    out_specs = ([pl.BlockSpec((tm, tn), lambda j, i: (i, j)) for _ in out_dtypes]
                 + [pl.BlockSpec((ms, tn), lambda j, i: (0, j)) for _ in out_dtypes])
    out_shape = ([jax.ShapeDtypeStruct((m, n), dt) for dt in out_dtypes]
                 + [jax.ShapeDtypeStruct((ms, n), dt) for dt in out_dtypes])
    if round_blocks:
        out_specs.append(r_out)
        out_shape.append(r_shape)
    outs = pl.pallas_call(
        functools.partial(_mm_ws_kernel, has_res=res is not None,
                          rope_every=rope_every if rope is not None else 0, rope_half=rope_half,
                          n_out=len(out_dtypes), round_blocks=round_blocks),
        grid=(n // tn, m // tm),
        in_specs=in_specs, out_specs=out_specs, out_shape=out_shape,
        scratch_shapes=[pltpu.VMEM((kdim, tn), BF16)],
        compiler_params=_params(("arbitrary" if round_blocks else "parallel", "arbitrary")),
        name="matmul_ws",
    )(*args)
    k = len(out_dtypes)
    if round_blocks:
        return outs[:k], outs[k:2 * k], outs[2 * k]
    return outs[:k], outs[k:]


def _ffn_up_kernel(x_ref, xs_ref, wv_ref, wg_ref, cw_ref, cb_ref, st_ref, f1_ref, f2_ref, wd_ref,
                   a_ref, tail_ref, as_ref, tails_ref, wd16_ref, w16, carry_ref, *, tiles_per_seq, seg):
    i = pl.program_id(1)
    tn = wv_ref.shape[1]
    _round_rows_job(pl.program_id(0), i, wd_ref, wd16_ref, pl.num_programs(0))

    @pl.when(i == 0)
    def _():
        w16[:, :tn] = wv_ref[...].astype(BF16)
        w16[:, tn:] = wg_ref[...].astype(BF16)

    def val_gate(x):
        r = _dot(x, w16[...])
        return r[:, :tn], r[:, tn:]

    cb = cb_ref[...]
    cw0, cw1, cw2 = cw_ref[0:1, :], cw_ref[1:2, :], cw_ref[2:3, :]

    def gated(val, gate, g1, g2):
        conv = cb + cw0 * g2 + cw1 * g1 + cw2 * gate
        return (val * (conv * _sigmoid(conv))).astype(BF16)

    @pl.when(i % tiles_per_seq == 0)
    def _():
        carry_ref[...] = st_ref[0]

    tm = x_ref.shape[0]
    row = lax.broadcasted_iota(jnp.int32, (SUBLANES, 1), 0)
    val, gate = val_gate(x_ref[...])
    c0 = carry_ref[0:1, :]
    c1 = carry_ref[1:2, :]
    g1 = pltpu.roll(gate, 1, 0)
    g2 = pltpu.roll(gate, 2, 0)
    g1 = jnp.concatenate([jnp.where(row == 0, c1, g1[:SUBLANES]), g1[SUBLANES:]], axis=0)
    g2 = jnp.concatenate([jnp.where(row == 0, c0, jnp.where(row == 1, c1, g2[:SUBLANES])), g2[SUBLANES:]], axis=0)
    a_ref[...] = gated(val, gate, g1, g2)
    carry_ref[...] = gate[tm - 2:tm, :]
    nt = tail_ref.shape[1]
    tail_ref[0] = gate[tm - nt:tm, :]

    @pl.when(i == pl.num_programs(1) - 1)
    def _():
        val, gate = val_gate(xs_ref[...])
        t = jnp.bitwise_and(lax.broadcasted_iota(jnp.int32, (val.shape[0], 1), 0), seg - 1)
        g1 = jnp.where(t == 0, f1_ref[...], pltpu.roll(gate, 1, 0))
        g2 = jnp.where(t < 2, f2_ref[...], pltpu.roll(gate, 2, 0))
        as_ref[...] = gated(val, gate, g1, g2)
        tails_ref[...] = gate


def ffn_up(h, hs, w_up, w_down, layer, d_ff, conv_w, conv_b, *, tm, state, fill, seg):
    m, kdim = h.shape
    ms = hs.shape[0]
    assert seg & (seg - 1) == 0
    tn = _tile(d_ff, 256)
    nj = d_ff // tn
    tm = _tile(m, tm)
    nseq = state.shape[0]
    tps = (m // nseq) // tm
    wd_in, wd_out, wd_shape = _round_rows_specs(w_down, nj, layer)
    a, tail, a_s, tail_s, wd16 = pl.pallas_call(
        functools.partial(_ffn_up_kernel, tiles_per_seq=tps, seg=seg),
        grid=(nj, m // tm),
        in_specs=[pl.BlockSpec((tm, kdim), lambda j, i: (i, 0)),
                  pl.BlockSpec((ms, kdim), lambda j, i: (0, 0)),
                  pl.BlockSpec((None, kdim, tn), lambda j, i: (layer, 0, j)),
                  pl.BlockSpec((None, kdim, tn), lambda j, i: (layer, 0, nj + j)),
                  pl.BlockSpec((3, tn), lambda j, i: (0, j)),
                  pl.BlockSpec((1, tn), lambda j, i: (0, j)),
                  pl.BlockSpec((1, 2, tn), lambda j, i: (i // tps, 0, j)),
                  pl.BlockSpec((ms, tn), lambda j, i: (0, j)),
                  pl.BlockSpec((ms, tn), lambda j, i: (0, j)),
                  wd_in],
        out_specs=[pl.BlockSpec((tm, tn), lambda j, i: (i, j)),
                   pl.BlockSpec((1, SUBLANES, tn), lambda j, i: (i // tps, 0, j)),
                   pl.BlockSpec((ms, tn), lambda j, i: (0, j)),
                   pl.BlockSpec((ms, tn), lambda j, i: (0, j)),
                   wd_out],
        out_shape=[jax.ShapeDtypeStruct((m, d_ff), BF16),
                   jax.ShapeDtypeStruct((nseq, SUBLANES, d_ff), F32),
                   jax.ShapeDtypeStruct((ms, d_ff), BF16),
                   jax.ShapeDtypeStruct((ms, d_ff), F32),
                   wd_shape],
        scratch_shapes=[pltpu.VMEM((kdim, 2 * tn), BF16), pltpu.VMEM((2, tn), F32)],
        compiler_params=_params(("parallel", "arbitrary")),
        name="ffn_up",
    )(h, hs, w_up, w_up, conv_w, conv_b.reshape(1, d_ff), state, fill[0], fill[1], w_down)
    return (a, tail), (a_s, tail_s), wd16


def _ret_kernel(lg_ref, q_ref, k_ref, v_ref, g_ref, cos_ref, sin_ref, s0_ref,
                o_ref, s_out_ref, s_scr, intra_scr, *, c_valid, dk):
    h = pl.program_id(1)
    c = pl.program_id(2)
    nc = pl.num_programs(2)
    half = dk // 2
    lg = lg_ref[h]
    cp = q_ref.shape[0]

    @pl.when(c == 0)
    def _():
        s_scr[...] = s0_ref[0, 0]
        di = lax.broadcasted_iota(jnp.int32, (cp, cp), 0)
        dj = lax.broadcasted_iota(jnp.int32, (cp, cp), 1)
        diff = di - dj
        intra_scr[...] = jnp.where((diff >= 0) & (dj < c_valid),
                                   jnp.exp(lg * jnp.maximum(diff, 0).astype(F32)), 0.0)

    cos = cos_ref[...]
    sin = sin_ref[...]

    def rope(x):
        x1 = x[:, :half]
        x2 = x[:, half:]
        return jnp.concatenate([x1 * cos - x2 * sin, x2 * cos + x1 * sin], axis=1)

    q = rope(q_ref[...].astype(F32))
    k = rope(k_ref[...].astype(F32)) * (dk ** -0.5)
    v = v_ref[...]
    ri = lax.broadcasted_iota(jnp.int32, (cp, 1), 0)
    rif = ri.astype(F32)
    read_decay = jnp.exp(lg * (rif + 1.0))
    write_decay = jnp.where(ri < c_valid, jnp.exp(lg * jnp.maximum(c_valid - 1.0 - rif, 0.0)), 0.0)

    s = s_scr[...]
    att = _dot_nt(q.astype(BF16), k.astype(BF16)) * intra_scr[...]
    o = _dot(att.astype(BF16), v) + _dot((q * read_decay).astype(BF16), s.astype(BF16))
    kw = (k * write_decay).astype(BF16)
    chunk_decay = jnp.exp(lg * jnp.full((1, 1), float(c_valid), F32))
    s_new = s * chunk_decay + lax.dot_general(kw, v, (((0,), (0,)), ((), ())), preferred_element_type=F32)
    s_scr[...] = s_new

    @pl.when(c == nc - 1)
    def _():
        s_out_ref[0, 0] = s_new

    on = o * lax.rsqrt(jnp.mean(o * o, axis=-1, keepdims=True) + EPS)
    gate = g_ref[...].astype(F32)
    o_ref[...] = (on * (gate * _sigmoid(gate))).astype(o_ref.dtype)


def retention(proj, s0, cos, sin, *, nb, heads, dk, dv, chunk, c_valid):
    m = proj.shape[0]
    t = m // nb
    nc = t // chunk
    qb = heads
    vb = 2 * heads * dk // dv
    log_g = jnp.log1p(-jnp.exp2(-5.0 - jnp.arange(heads, dtype=F32)))
    grid_spec = pltpu.PrefetchScalarGridSpec(
        num_scalar_prefetch=1,
        grid=(nb, heads, nc),
        in_specs=[
            pl.BlockSpec((chunk, dk), lambda b, h, c, lg: (b * nc + c, h)),
            pl.BlockSpec((chunk, dk), lambda b, h, c, lg: (b * nc + c, qb + h)),
            pl.BlockSpec((chunk, dv), lambda b, h, c, lg: (b * nc + c, vb + h)),
            pl.BlockSpec((chunk, dv), lambda b, h, c, lg: (b * nc + c, vb + heads + h)),
            pl.BlockSpec((chunk, dk // 2), lambda b, h, c, lg: (c, 0)),
            pl.BlockSpec((chunk, dk // 2), lambda b, h, c, lg: (c, 0)),
            pl.BlockSpec((1, 1, dk, dv), lambda b, h, c, lg: (b, h, 0, 0)),
        ],
        out_specs=[
            pl.BlockSpec((chunk, dv), lambda b, h, c, lg: (b * nc + c, h)),
            pl.BlockSpec((1, 1, dk, dv), lambda b, h, c, lg: (b, h, 0, 0)),
        ],
        scratch_shapes=[pltpu.VMEM((dk, dv), F32), pltpu.VMEM((chunk, chunk), F32)],
    )
    o, s = pl.pallas_call(
        functools.partial(_ret_kernel, c_valid=c_valid, dk=dk),
        grid_spec=grid_spec,
        out_shape=[jax.ShapeDtypeStruct((m, heads * dv), BF16),
                   jax.ShapeDtypeStruct((nb, heads, dk, dv), F32)],
        compiler_params=_params(("parallel", "parallel", "arbitrary")),
        name="retention",
    )(log_g, proj, proj, proj, proj, cos, sin, s0)
    return o, s


def _cmp_project(planes, w_ref, p_ref, kvh, hd):
    n_seg = planes.shape[1] // CMP_STRIDE
    for br in range(2):
        acc = jnp.zeros((kvh * n_seg, 2 * hd), F32)
        for sp in range(CMP_STRIDE // 2):
            pieces = []
            for g in range(kvh):
                plane = planes.at[br * kvh + g]
                a = plane[pl.ds(2 * sp, n_seg, stride=CMP_STRIDE), :]
                b = plane[pl.ds(2 * sp + 1, n_seg, stride=CMP_STRIDE), :]
                pieces.append(jnp.concatenate([a, b], axis=1))
            lhs = jnp.concatenate(pieces, axis=0).astype(BF16)
            acc = acc + _dot(lhs, w_ref[br, sp])
        for g in range(kvh):
            p_ref[0, br * kvh + g] = acc[g * n_seg:(g + 1) * n_seg, :]


def _cmp_proj_rows_kernel(x_ref, w_ref, p_ref, r_ref, *, kvh, hd):
    for hh in range(2 * kvh):
        r_ref[hh] = x_ref[:, hh * hd:(hh + 1) * hd]
    _cmp_project(r_ref, w_ref, p_ref, kvh, hd)


def _cmp_proj_pages_kernel(pt_ref, cache_ref, w_ref, p_ref, r_ref, sem, *, kvh, hd, pages_per_step, n_pages):
    b = pl.program_id(0)
    c = pl.program_id(1)
    nchunk = pl.num_programs(1)
    step = b * nchunk + c
    nsteps = pl.num_programs(0) * nchunk
    page = cache_ref.shape[1]

    def copies(st, slot):
        bb = st // nchunk
        cc = st % nchunk
        out = []
        for u in range(pages_per_step):
            phys = pt_ref[bb * n_pages + cc * pages_per_step + u]
            for hh in range(2 * kvh):
                out.append(pltpu.make_async_copy(
                    cache_ref.at[phys, :, hh // kvh, hh % kvh, :],
                    r_ref.at[slot, hh, pl.ds(u * page, page), :],
                    sem.at[slot]))
        return out

    slot = step % 2

    @pl.when(step == 0)
    def _():
        for cp in copies(step, slot):
            cp.start()

    @pl.when(step + 1 < nsteps)
    def _():
        for cp in copies(step + 1, 1 - slot):
            cp.start()

    for cp in copies(step, slot):
        cp.wait()
    _cmp_project(r_ref.at[slot], w_ref, p_ref, kvh, hd)


def _cmp_w1_pairs(cmp_w1, hd):
    r = CMP_LEN // CMP_STRIDE
    e = cmp_w1.shape[-1]
    w = cmp_w1.reshape(2, r, CMP_STRIDE // 2, 2, hd, e)
    w = w.transpose(0, 2, 3, 4, 1, 5)
    return w.reshape(2, CMP_STRIDE // 2, 2 * hd, r * e).astype(BF16)


def cmp_proj_rows(rows2d, nb, t, w_pairs, kvh, hd):
    n_seg = t // CMP_STRIDE
    width = 2 * kvh * hd
    return pl.pallas_call(
        functools.partial(_cmp_proj_rows_kernel, kvh=kvh, hd=hd),
        grid=(nb,),
        in_specs=[pl.BlockSpec((t, width), lambda b: (b, 0)),
                  pl.BlockSpec(w_pairs.shape, lambda b: (0, 0, 0, 0))],
        out_specs=pl.BlockSpec((1, 2 * kvh, n_seg, 2 * hd), lambda b: (b, 0, 0, 0)),
        out_shape=jax.ShapeDtypeStruct((nb, 2 * kvh, n_seg, 2 * hd), F32),
        scratch_shapes=[pltpu.VMEM((2 * kvh, t, hd), F32)],
        compiler_params=_params(("parallel",)),
        name="cmp_proj_prompt",
    )(rows2d, w_pairs)


def cmp_proj_pages(cache, page_table, w_pairs, kvh, hd, pages_per_step):
    nb, n_pages = page_table.shape
    page = cache.shape[1]
    n_seg = pages_per_step * page // CMP_STRIDE
    nchunk = n_pages // pages_per_step
    grid_spec = pltpu.PrefetchScalarGridSpec(
        num_scalar_prefetch=1,
        grid=(nb, nchunk),
        in_specs=[pl.BlockSpec(memory_space=pl.ANY),
                  pl.BlockSpec(w_pairs.shape, lambda b, c, pt: (0, 0, 0, 0))],
        out_specs=pl.BlockSpec((1, 2 * kvh, n_seg, 2 * hd), lambda b, c, pt: (b, 0, c, 0)),
        scratch_shapes=[pltpu.VMEM((2, 2 * kvh, pages_per_step * page, hd), F32),
                        pltpu.SemaphoreType.DMA((2,))],
    )
    return pl.pallas_call(
        functools.partial(_cmp_proj_pages_kernel, kvh=kvh, hd=hd, pages_per_step=pages_per_step,
                          n_pages=n_pages),
        grid_spec=grid_spec,
        out_shape=jax.ShapeDtypeStruct((nb, 2 * kvh, nchunk * n_seg, 2 * hd), F32),
        compiler_params=_params(("arbitrary", "arbitrary")),
        name="cmp_proj_sample",
    )(page_table.reshape(-1), cache, w_pairs)


def _cmp_pe_kernel(pe_ref, w_ref, o_ref):
    o_ref[0] = _dot(pe_ref[0], w_ref[0])


def cmp_pe_term(cmp_pe, cmp_w1):
    _, n, hd = cmp_pe.shape
    e = cmp_w1.shape[-1]
    pe_flat = jnp.zeros((2, SUBLANES, n * hd), BF16).at[:, 0].set(cmp_pe.reshape(2, n * hd).astype(BF16))
    w_flat = cmp_w1.reshape(2, n * hd, e).astype(BF16)
    return pl.pallas_call(
        _cmp_pe_kernel,
        grid=(2,),
        in_specs=[pl.BlockSpec((1, SUBLANES, n * hd), lambda b: (b, 0, 0)),
                  pl.BlockSpec((1, n * hd, e), lambda b: (b, 0, 0))],
        out_specs=pl.BlockSpec((1, SUBLANES, e), lambda b: (b, 0, 0)),
        out_shape=jax.ShapeDtypeStruct((2, SUBLANES, e), F32),
        compiler_params=_params(("parallel",)),
        name="cmp_pe",
    )(pe_flat, w_flat)


def _cmp_finish_kernel(p_ref, pe_ref, w2_ref, o_ref):
    p = p_ref[0, 0]
    n_seg = p.shape[0]
    e = p.shape[1] // 2
    hid = p[:, :e] + pltpu.roll(p[:, e:], n_seg - 1, 0) + pe_ref[0, 0:1, :]
    c = math.sqrt(2.0 / math.pi)
    act = 0.5 * hid * (1.0 + jnp.tanh(c * (hid + 0.044715 * (hid * hid * hid))))
    o_ref[0, 0] = _dot(act.astype(BF16), w2_ref[0]).astype(o_ref.dtype)


def cmp_finish(p, pe_term, cmp_w2, kvh):
    nb, nu, n_seg, e2 = p.shape
    e = e2 // 2
    hd = cmp_w2.shape[-1]
    return pl.pallas_call(
        _cmp_finish_kernel,
        grid=(nb, nu),
        in_specs=[pl.BlockSpec((1, 1, n_seg, e2), lambda b, u: (b, u, 0, 0)),
                  pl.BlockSpec((1, SUBLANES, e), lambda b, u: (u // kvh, 0, 0)),
                  pl.BlockSpec((1, e, hd), lambda b, u: (u // kvh, 0, 0))],
        out_specs=pl.BlockSpec((1, 1, n_seg, hd), lambda b, u: (b, u, 0, 0)),
        out_shape=jax.ShapeDtypeStruct((nb, nu, n_seg, hd), BF16),
        compiler_params=_params(("parallel", "parallel")),
        name="cmp_finish",
    )(p, pe_term, cmp_w2.astype(BF16))


def _cmp_attn_kernel(q_ref, kc_ref, vc_ref, agg_ref, o_ref, imp_ref, *, group, hd, n_c, pos0):
    qi = pl.program_id(2)
    tq = q_ref.shape[0]
    kc = kc_ref[0, 0]
    vc = vc_ref[0, 0]
    agg = agg_ref[...]
    n_cp = kc.shape[0]
    qs = jnp.concatenate([q_ref[:, h * hd:(h + 1) * hd] for h in range(group)], axis=0)
    qpos1 = pos0 + qi * tq + lax.broadcasted_iota(jnp.int32, (tq, 1), 0)
    qpos = jnp.concatenate([qpos1] * group, axis=0)
    ci = lax.broadcasted_iota(jnp.int32, (1, n_cp), 1)
    cm = ((ci * CMP_STRIDE + (CMP_LEN - 1)) <= qpos) & (ci < n_c)
    s = jnp.where(cm, _dot_nt(qs, kc) * (hd ** -0.5), NEG)
    e = jnp.where(cm, jnp.exp(s - jnp.max(s, axis=-1, keepdims=True)), 0.0)
    p = e / jnp.maximum(jnp.sum(e, axis=-1, keepdims=True), TINY)
    pb = p.astype(BF16)
    o = _dot(pb, vc)
    psum = jnp.zeros((tq, n_cp), F32)
    for h in range(group):
        o_ref[:, h * hd:(h + 1) * hd] = o[h * tq:(h + 1) * tq, :].astype(o_ref.dtype)
        psum = psum + pb[h * tq:(h + 1) * tq, :].astype(F32)
    hi = psum.astype(BF16)
    lo = (psum - hi.astype(F32)).astype(BF16)
    imp_ref[0, 0] = _dot_nt(agg, hi) + _dot_nt(agg, lo)


def _slc_aggregation_t(n_cp, n_slc, rows):
    rs = SLC_BLOCK // CMP_STRIDE
    rc = CMP_LEN // CMP_STRIDE
    i = jnp.arange(n_cp)[None, :]
    j = jnp.arange(rows)[:, None]
    w = sum((i == j * rs + m - n).astype(F32) for m in range(rs) for n in range(rc))
    return jnp.where(j < n_slc, w, 0.0).astype(BF16)


def cmp_attn(q, kvc, *, nb, kvh, group, hd, n_c, n_slc, pos0, tq):
    m = q.shape[0]
    t = m // nb
    tq = _tile(t, tq)
    nq = t // tq
    n_cp = kvc.shape[2]
    rows = -(-n_slc // SUBLANES) * SUBLANES
    agg_t = _slc_aggregation_t(n_cp, n_slc, rows)
    gw = group * hd
    return pl.pallas_call(
        functools.partial(_cmp_attn_kernel, group=group, hd=hd, n_c=n_c, pos0=pos0),
        grid=(nb, kvh, nq),
        in_specs=[pl.BlockSpec((tq, gw), lambda b, g, i: (b * nq + i, g)),
                  pl.BlockSpec((1, 1, n_cp, hd), lambda b, g, i: (b, g, 0, 0)),
                  pl.BlockSpec((1, 1, n_cp, hd), lambda b, g, i: (b, kvh + g, 0, 0)),
                  pl.BlockSpec((rows, n_cp), lambda b, g, i: (0, 0))],
        out_specs=[pl.BlockSpec((tq, gw), lambda b, g, i: (b * nq + i, g)),
                   pl.BlockSpec((1, 1, rows, tq), lambda b, g, i: (b, g, 0, i))],
        out_shape=[jax.ShapeDtypeStruct((m, kvh * gw), BF16),
                   jax.ShapeDtypeStruct((nb, kvh, rows, t), F32)],
        compiler_params=_params(("parallel", "parallel", "parallel")),
        name="cmp_attn",
    )(q, kvc, kvc, agg_t)


def _select_kernel(imp_ref, *rest, n_slc, pos0, t_len, emit_idx):
    if emit_idx:
        idx_ref, ok_ref, sc_ref = rest
    else:
        bias_ref, sc_ref = rest
    rows, width = imp_ref.shape
    lane = pl.program_id(0) * width + lax.broadcasted_iota(jnp.int32, (1, width), 1)
    qpos = pos0 + jnp.bitwise_and(lane, t_len - 1)
    blk = lax.broadcasted_iota(jnp.int32, (rows, width), 0)
    cur = jnp.right_shift(qpos, SLC_BLOCK.bit_length() - 1)
    visible = blk * SLC_BLOCK <= qpos
    forced = (blk == 0) | (blk == cur) | (blk == cur - 1)
    score = jnp.where(visible, jnp.where(forced, FORCE, imp_ref[...]), NEG)
    sc_ref[...] = score

    def count(i, cnt):
        si = sc_ref[pl.ds(i, 1), :]
        beats = (si > score) | ((si == score) & (i < blk))
        return cnt + jnp.where(beats, 1, 0)

    rank = lax.fori_loop(0, n_slc, count, jnp.zeros((rows, width), jnp.int32))
    in_range = blk < n_slc
    if emit_idx:
        blk_f = blk.astype(F32)
        for r in range(idx_ref.shape[0]):
            hit = (rank == r) & in_range
            idx_r = jnp.sum(jnp.where(hit, blk_f, 0.0), axis=0, keepdims=True)
            idx_ref[r:r + 1, :] = idx_r.astype(jnp.int32)
            ok_ref[r:r + 1, :] = jnp.sum(jnp.where(hit & visible, 1.0, 0.0), axis=0, keepdims=True)
    else:
        chosen = (rank < N_SELECT) & visible & in_range
        bias_ref[...] = jnp.where(chosen, 0.0, NEG).astype(bias_ref.dtype)


def select_blocks(imp, *, n_slc, pos0, emit_idx):
    nb, kvh, rows, t = imp.shape
    assert t & (t - 1) == 0
    lanes = nb * kvh * t
    imp2 = imp.transpose(2, 0, 1, 3).reshape(rows, lanes)
    width = _tile(lanes, 2048)
    spec = pl.BlockSpec((rows, width), lambda i: (0, i))
    if emit_idx:
        nsel = min(N_SELECT, n_slc)
        o_spec = pl.BlockSpec((nsel, width), lambda i: (0, i))
        out_specs = [o_spec, o_spec]
        out_shape = [jax.ShapeDtypeStruct((nsel, lanes), jnp.int32), jax.ShapeDtypeStruct((nsel, lanes), F32)]
    else:
        out_specs = [spec]
        out_shape = [jax.ShapeDtypeStruct((rows, lanes), BF16)]
    outs = pl.pallas_call(
        functools.partial(_select_kernel, n_slc=n_slc, pos0=pos0, t_len=t, emit_idx=emit_idx),
        grid=(lanes // width,),
        in_specs=[spec], out_specs=out_specs, out_shape=out_shape,
        scratch_shapes=[pltpu.VMEM((rows, width), F32)],
        compiler_params=_params(("parallel",)),
        name="select_blocks",
    )(imp2)
    if emit_idx:
        return tuple(o.reshape(-1, nb, kvh, t).transpose(1, 2, 0, 3) for o in outs)
    return outs[0].reshape(rows, nb, kvh, t).transpose(1, 2, 3, 0)


V_ROWS_PAD = 2 * SUBLANES


def _prompt_attn_kernel(q_ref, bias_ref, ks_in, vs_in, kw_ref, vw_in, hot_ref, oc_ref, gc_ref, gs_ref, gw_ref,
                        o_ref, ks_ref, vts_ref, vtw_ref, *, nq, hd):
    qi = pl.program_id(2)
    hs = pl.program_id(3)
    tq = q_ref.shape[0]
    n_heads = q_ref.shape[1] // hd
    kl = lax.broadcasted_iota(jnp.int32, (tq, tq), 0)
    ql = lax.broadcasted_iota(jnp.int32, (tq, tq), 1)

    @pl.when((qi == 0) & (hs == 0))
    def _():
        ks_ref[:, :hd] = ks_in[...]
        ks_ref[:, hd:] = hot_ref[...]
        pad = vts_ref.shape[0] - hd
        ones_row = jnp.where(lax.broadcasted_iota(jnp.int32, (pad, vts_ref.shape[1]), 0) == 0, 1.0, 0.0)
        for src, dst in ((vs_in, vts_ref), (vw_in, vtw_ref)):
            dst[:hd, :] = src[...].astype(F32).T.astype(BF16)
            dst[hd:, :] = ones_row.astype(BF16)

    def branch(k_ref, vt_ref, q, k0, nkeys, band):
        st = _dot_nt(k_ref[k0:k0 + nkeys, :], q)
        parts = []
        if nkeys > tq:
            top = st[:nkeys - tq, :]
            if band:
                top = jnp.where(kl > ql, top, NEG)
            parts.append(top)
        parts.append(jnp.where(kl <= ql, st[nkeys - tq:, :], NEG))
        st = jnp.concatenate(parts, axis=0) if len(parts) > 1 else parts[0]
        p = jnp.exp2(st - jnp.max(st, axis=0, keepdims=True)).astype(BF16)
        ot = _dot(vt_ref[:, k0:k0 + nkeys], p)
        return ot[:hd, :] / ot[hd:hd + 1, :]

    def run(v):
        for hp in range(n_heads):
            sl = slice(hp * hd, (hp + 1) * hd)
            row = pl.ds(hs * n_heads + hp, 1)
            q = (q_ref[:, sl].astype(F32) * (hd ** -0.5 * math.log2(math.e))).astype(BF16)
            o_s = branch(ks_ref, vts_ref, jnp.concatenate([q, bias_ref[0, 0]], axis=1), 0, (v + 1) * tq, False)
            o_w = branch(kw_ref, vtw_ref, q, max(v - 1, 0) * tq, min(v + 1, 2) * tq, v > 0)
            o_c = oc_ref[:, sl].astype(F32).T
            o_t = o_c * gc_ref[row, :] + o_s * gs_ref[row, :] + o_w * gw_ref[row, :]
            o_ref[:, sl] = o_t.T.astype(o_ref.dtype)

    for v in range(nq):
        pl.when(qi == v)(functools.partial(run, v))


def prompt_attn(q, o_cmp, gates_t, bias, kv16, *, nb, kvh, group, hd, tq, heads_per_step=4):
    m = q.shape[0]
    t = m // nb
    tq = _tile(t, tq)
    nq = t // tq
    assert tq == WINDOW or nq == 1
    hw = heads_per_step * hd
    hsteps = group // heads_per_step
    head_spec = pl.BlockSpec((tq, hw), lambda b, g, i, h: (b * nq + i, g * hsteps + h))
    onehot = (jnp.arange(t)[:, None] // SLC_BLOCK == jnp.arange(hd)[None, :]).astype(BF16)

    def kv_spec(sec):
        return pl.BlockSpec((t, hd), lambda b, g, i, h: (b, sec * kvh + g))

    def gate_spec(branch):
        return pl.BlockSpec((group, tq), lambda b, g, i, h: (branch * kvh + g, b * nq + i))

    return pl.pallas_call(
        functools.partial(_prompt_attn_kernel, nq=nq, hd=hd),
        grid=(nb, kvh, nq, hsteps),
        in_specs=[head_spec,
                  pl.BlockSpec((1, 1, tq, hd), lambda b, g, i, h: (b, g, i, 0)),
                  kv_spec(2), kv_spec(3), kv_spec(4), kv_spec(5),
                  pl.BlockSpec((t, hd), lambda b, g, i, h: (0, 0)),
                  head_spec, gate_spec(0), gate_spec(1), gate_spec(2)],
        out_specs=head_spec,
        out_shape=jax.ShapeDtypeStruct((m, kvh * group * hd), BF16),
        scratch_shapes=[pltpu.VMEM((t, 2 * hd), BF16),
                        pltpu.VMEM((hd + V_ROWS_PAD, t), BF16),
                        pltpu.VMEM((hd + V_ROWS_PAD, t), BF16)],
        compiler_params=_params(("arbitrary", "arbitrary", "arbitrary", "arbitrary")),
        name="prompt_attn",
    )(q, bias, kv16, kv16, kv16, kv16, onehot, o_cmp, gates_t, gates_t, gates_t)


def _sample_attn_kernel(idx_ref, pt_ref, q_ref, kpos_ref, okx_ref, kn_ref, vn_ref, wk_ref, wv_ref,
                        wkn_ref, wvn_ref, cache_ref, oslc_ref, owin_ref, kbuf, vbuf, sem,
                        *, kvh, group, hd, t_valid, past, nsel, ks_sec, vs_sec):
    b = pl.program_id(0)
    g = pl.program_id(1)
    tp = q_ref.shape[2] // group
    n_blocks = past // SLC_BLOCK
    per_page = PAGE_SIZE // SLC_BLOCK
    n_fetch = t_valid * nsel

    def copies(f):
        t = f // nsel
        r = f % nsel
        blk = idx_ref[((b * kvh + g) * nsel + r) * tp + t]
        blk = jnp.minimum(blk, n_blocks - 1)
        page = pt_ref[b * (past // PAGE_SIZE) + blk // per_page]
        row0 = (blk % per_page) * SLC_BLOCK
        dst = pl.ds(f * SLC_BLOCK, SLC_BLOCK)
        ck = pltpu.make_async_copy(cache_ref.at[page, pl.ds(row0, SLC_BLOCK), ks_sec, g, :],
                                   kbuf.at[dst, :], sem.at[0])
        cv = pltpu.make_async_copy(cache_ref.at[page, pl.ds(row0, SLC_BLOCK), vs_sec, g, :],
                                   vbuf.at[dst, :], sem.at[1])
        return ck, cv

    def start(f, c):
        ck, cv = copies(f)
        ck.start()
        cv.start()
        return c

    def wait(f, c):
        ck, cv = copies(f)
        ck.wait()
        cv.wait()
        return c

    lax.fori_loop(0, n_fetch, start, 0)

    scale = hd ** -0.5
    qa = q_ref[0, 0].astype(BF16)
    trow = jnp.right_shift(lax.broadcasted_iota(jnp.int32, (tp * group, 1), 0), group.bit_length() - 1)
    wk = wk_ref[0].astype(BF16)
    wrows = wk.shape[0]
    s_c = _dot_nt(qa, wk) * scale
    s_n = _dot_nt(qa, wkn_ref[...].astype(BF16)) * scale
    d_c = (past + trow) - (past - wrows + lax.broadcasted_iota(jnp.int32, (1, wrows), 1))
    jn = lax.broadcasted_iota(jnp.int32, (1, tp), 1)
    d_n = trow - jn
    m_c = (d_c >= 0) & (d_c < WINDOW)
    m_n = (d_n >= 0) & (d_n < WINDOW) & (jn < t_valid)
    s_c = jnp.where(m_c, s_c, NEG)
    s_n = jnp.where(m_n, s_n, NEG)
    mx = jnp.maximum(jnp.max(s_c, axis=-1, keepdims=True), jnp.max(s_n, axis=-1, keepdims=True))
    e_c = jnp.where(m_c, jnp.exp(s_c - mx), 0.0)
    e_n = jnp.where(m_n, jnp.exp(s_n - mx), 0.0)
    den = jnp.maximum(jnp.sum(e_c, axis=-1, keepdims=True) + jnp.sum(e_n, axis=-1, keepdims=True), TINY)
    o_w = _dot((e_c / den).astype(BF16), wv_ref[0].astype(BF16)) + \
        _dot((e_n / den).astype(BF16), wvn_ref[...].astype(BF16))
    owin_ref[0, 0] = o_w.astype(owin_ref.dtype)

    lax.fori_loop(0, n_fetch, wait, 0)

    kn = kn_ref[...].astype(BF16)
    vn = vn_ref[...].astype(BF16)
    span = nsel * SLC_BLOCK
    oslc_ref[...] = jnp.zeros(oslc_ref.shape, oslc_ref.dtype)
    for t in range(t_valid):
        qt = q_ref[0, 0, t * group:(t + 1) * group, :].astype(BF16)
        kt = kbuf[t * span:(t + 1) * span, :].astype(BF16)
        vt = vbuf[t * span:(t + 1) * span, :].astype(BF16)
        kp = kpos_ref[0, 0, t:t + 1, :]
        okv = okx_ref[0, 0, t:t + 1, :] > 0.5
        qpos = past + t
        m_g = (kp <= qpos) & okv & (kp < past)
        s_g = jnp.where(m_g, _dot_nt(qt, kt) * scale, NEG)
        selrow = jnp.zeros((1, tp), F32)
        for j in range(t_valid):
            hit = jnp.max(jnp.where((kp == past + j) & okv, 1.0, 0.0), axis=-1, keepdims=True)
            selrow = selrow + jnp.where(jn == j, hit, 0.0)
        m_w = (selrow > 0.5) & (jn <= t)
        s_w = jnp.where(m_w, _dot_nt(qt, kn) * scale, NEG)
        mx = jnp.maximum(jnp.max(s_g, axis=-1, keepdims=True), jnp.max(s_w, axis=-1, keepdims=True))
        e_g = jnp.where(m_g, jnp.exp(s_g - mx), 0.0)
        e_w = jnp.where(m_w, jnp.exp(s_w - mx), 0.0)
        den = jnp.sum(e_g, axis=-1, keepdims=True) + jnp.sum(e_w, axis=-1, keepdims=True)
        o_t = _dot((e_g / den).astype(BF16), vt) + _dot((e_w / den).astype(BF16), vn)
        oslc_ref[0, 0, t * group:(t + 1) * group, :] = o_t.astype(oslc_ref.dtype)


def sample_attn(q4, idx, okf, kv_new, cache_kv4, cache_win3, page_table, *, kvh, group, hd, t_valid, past):
    nb = q4.shape[0]
    tp = q4.shape[2] // group
    nsel = idx.shape[2]
    span = nsel * SLC_BLOCK
    wrows = cache_win3.shape[1]
    idx_t = idx.transpose(0, 1, 3, 2)
    kpos = (idx_t[..., None] * SLC_BLOCK + jnp.arange(SLC_BLOCK, dtype=jnp.int32)).reshape(nb, kvh, tp, span)
    okx = jnp.broadcast_to(okf.transpose(0, 1, 3, 2)[..., None], (nb, kvh, tp, nsel, SLC_BLOCK)).reshape(
        nb, kvh, tp, span)
    grid_spec = pltpu.PrefetchScalarGridSpec(
        num_scalar_prefetch=2,
        grid=(nb, kvh),
        in_specs=[
            pl.BlockSpec((1, 1, tp * group, hd), lambda b, g, *_: (b, g, 0, 0)),
            pl.BlockSpec((1, 1, tp, span), lambda b, g, *_: (b, g, 0, 0)),
            pl.BlockSpec((1, 1, tp, span), lambda b, g, *_: (b, g, 0, 0)),
            pl.BlockSpec((tp, hd), lambda b, g, *_: (b, 2 * kvh + g)),
            pl.BlockSpec((tp, hd), lambda b, g, *_: (b, 3 * kvh + g)),
            pl.BlockSpec((1, wrows, hd), lambda b, g, *_: (b, 0, g)),
            pl.BlockSpec((1, wrows, hd), lambda b, g, *_: (b, 0, kvh + g)),
            pl.BlockSpec((tp, hd), lambda b, g, *_: (b, 4 * kvh + g)),
            pl.BlockSpec((tp, hd), lambda b, g, *_: (b, 5 * kvh + g)),
            pl.BlockSpec(memory_space=pl.ANY),
        ],
        out_specs=[pl.BlockSpec((1, 1, tp * group, hd), lambda b, g, *_: (b, g, 0, 0)),
                   pl.BlockSpec((1, 1, tp * group, hd), lambda b, g, *_: (b, g, 0, 0))],
        scratch_shapes=[pltpu.VMEM((t_valid * span, hd), F32),
                        pltpu.VMEM((t_valid * span, hd), F32),
                        pltpu.SemaphoreType.DMA((2,))],
    )
    return pl.pallas_call(
        functools.partial(_sample_attn_kernel, kvh=kvh, group=group, hd=hd, t_valid=t_valid, past=past,
                          nsel=nsel, ks_sec=2, vs_sec=3),
        grid_spec=grid_spec,
        out_shape=[jax.ShapeDtypeStruct(q4.shape, F32), jax.ShapeDtypeStruct(q4.shape, F32)],
        compiler_params=_params(("arbitrary", "arbitrary")),
        name="sample_attn",
    )(idx.reshape(-1), page_table.reshape(-1), q4, kpos, okx, kv_new, kv_new, cache_win3, cache_win3,
      kv_new, kv_new, cache_kv4)


def _combine_kernel(oc_ref, os_ref, ow_ref, g_ref, o_ref, *, heads, hd):
    gates = g_ref[...]
    for h in range(heads):
        sl = slice(h * hd, (h + 1) * hd)
        o = (gates[:, h:h + 1] * oc_ref[:, sl].astype(F32)
             + gates[:, heads + h:heads + h + 1] * os_ref[:, sl].astype(F32)
             + gates[:, 2 * heads + h:2 * heads + h + 1] * ow_ref[:, sl].astype(F32))
        o_ref[:, sl] = o.astype(o_ref.dtype)


def combine(o_cmp, o_slc, o_win, gates, heads, hd):
    m, d = o_cmp.shape
    tm = _tile(m, 256)
    spec = pl.BlockSpec((tm, d), lambda i: (i, 0))
    return pl.pallas_call(
        functools.partial(_combine_kernel, heads=heads, hd=hd),
        grid=(m // tm,),
        in_specs=[spec, spec, spec, pl.BlockSpec((tm, 3 * heads), lambda i: (i, 0))],
        out_specs=spec,
        out_shape=jax.ShapeDtypeStruct((m, d), BF16),
        compiler_params=_params(("parallel",)),
        name="combine",
    )(o_cmp, o_slc, o_win, gates)


def _rope_tables_half(pos, half, theta):
    inv = jnp.power(theta, -jnp.arange(half, dtype=F32) / half)
    ang = pos.astype(F32)[:, None] * inv[None, :]
    return jnp.cos(ang), jnp.sin(ang)


def _nsa_rope_tables(pos, hd):
    half = hd // 8
    cos, sin = _rope_tables_half(pos, half, ROPE_THETA)
    n = pos.shape[0]
    ones = jnp.ones((n, hd - 2 * half), F32)
    zeros = jnp.zeros((n, hd - 2 * half), F32)
    zh = jnp.zeros((n, half), F32)
    c = jnp.concatenate([cos, cos, ones], axis=1)
    s_up = jnp.concatenate([-sin, zh, zeros], axis=1)
    s_dn = jnp.concatenate([zh, sin, zeros], axis=1)
    return c, s_up, s_dn


class _Stream:
    def __init__(self, x, nb, t_rows, t_valid, pos0, ret_s0, conv_state, ctx):
        self.x, self.nb, self.t_rows, self.t_valid, self.pos0 = x, nb, t_rows, t_valid, pos0
        self.ret_s0, self.conv_state, self.ctx = ret_s0, conv_state, ctx
        self.m = nb * t_rows
        self.pos = pos0 + jnp.arange(t_rows)


def _nsa_branches(st, q, gates, kv32, kv16, w_pairs, pe_term, w, *, hd, kvh, group):
    nb, t_rows, t_valid, pos0, m = st.nb, st.t_rows, st.t_valid, st.pos0, st.m
    qw = kvh * group * hd
    if st.ctx is None:
        p = cmp_proj_rows(kv32, nb, t_rows, w_pairs, kvh, hd)
        kvc = cmp_finish(p, pe_term, w["cmp_w2"], kvh)
        n_seg = t_rows // CMP_STRIDE
        n_c = n_seg - CMP_LEN // CMP_STRIDE + 1
        n_slc = -(-t_rows // SLC_BLOCK)
        o_cmp, imp = cmp_attn(q, kvc, nb=nb, kvh=kvh, group=group, hd=hd, n_c=n_c, n_slc=n_slc,
                              pos0=pos0, tq=512)
        bias = select_blocks(imp, n_slc=n_slc, pos0=pos0, emit_idx=False)
        bias = jnp.pad(bias, ((0, 0), (0, 0), (0, 0), (0, hd - bias.shape[-1])))
        o = prompt_attn(q, o_cmp, gates.T, bias, kv16, nb=nb, kvh=kvh, group=group, hd=hd, tq=WINDOW)
        keep = min(WINDOW, t_valid)
        win = kv32.reshape(nb, t_rows, 6 * kvh * hd)[:, t_valid - keep:t_valid, 4 * kvh * hd:].reshape(
            nb, keep, 2, kvh, hd)
    else:
        cache_kv, cache_win, page_table = st.ctx
        past = pos0
        p = cmp_proj_pages(cache_kv, page_table, w_pairs, kvh, hd, pages_per_step=min(16, page_table.shape[1]))
        kvc = cmp_finish(p, pe_term, w["cmp_w2"], kvh)
        n_seg = (past + t_valid) // CMP_STRIDE
        n_c = n_seg - CMP_LEN // CMP_STRIDE + 1
        n_slc = -(-(past + t_valid) // SLC_BLOCK)
        o_cmp, imp = cmp_attn(q, kvc, nb=nb, kvh=kvh, group=group, hd=hd, n_c=n_c, n_slc=n_slc,
                              pos0=pos0, tq=t_rows)
        idx, okf = select_blocks(imp, n_slc=n_slc, pos0=pos0, emit_idx=True)
        q4 = q.astype(F32).reshape(nb, t_rows, kvh, group, hd).transpose(0, 2, 1, 3, 4).reshape(
            nb, kvh, t_rows * group, hd)
        wrows = cache_win.shape[1]
        o_slc4, o_win4 = sample_attn(q4, idx, okf, kv32, cache_kv,
                                     cache_win.reshape(nb, wrows, 2 * kvh * hd), page_table,
                                     kvh=kvh, group=group, hd=hd, t_valid=t_valid, past=past)

        def rows(o4):
            return o4.reshape(nb, kvh, t_rows, group, hd).transpose(0, 2, 1, 3, 4).reshape(m, qw).astype(BF16)

        win_new = kv32[:, 4 * kvh * hd:].reshape(nb, t_rows, 2, kvh, hd)[:, :t_valid]
        win = jnp.concatenate([cache_win, win_new], axis=1)[:, t_valid:]
        o = combine(o_cmp, rows(o_slc4), rows(o_win4), gates, kvh * group, hd)
    return o, win


def _trunk(sp, ss, w):
    d = sp.x.shape[1]
    streams = (sp, ss)
    tm = 1024
    d_ff = w["ffn_conv_w"].shape[-1]
    heads_r = RET_HEADS
    dk = d // heads_r
    dv = 2 * dk
    hd = d // NSA_HEADS
    kvh = NSA_KV_HEADS
    group = NSA_HEADS // kvh
    qw = NSA_HEADS * hd
    rope_half = hd // 8

    def norm(xs, g, dt=BF16):
        return [rmsnorm(x, g, dt) for x in xs]

    def ffn(xs, layer):
        hs = norm(xs, w["norm_ffn"][layer])
        st = ss.conv_state[layer]
        f1 = jnp.zeros((ss.nb, ss.t_rows, d_ff), F32).at[:, 0].set(st[:, 1])
        f2 = jnp.zeros((ss.nb, ss.t_rows, d_ff), F32).at[:, 0].set(st[:, 0]).at[:, 1].set(st[:, 1])
        (a_p, tail_p), (a_s, tail_s), w_down16 = ffn_up(
            hs[0], hs[1], w["ffn_w_up"], w["ffn_w_down"], layer, d_ff, w["ffn_conv_w"][layer],
            w["ffn_conv_b"][layer], tm=tm,
            state=sp.conv_state[layer], fill=(f1.reshape(ss.m, d_ff), f2.reshape(ss.m, d_ff)), seg=ss.t_rows)
        cs = [tail_p[:, SUBLANES - 2:],
              tail_s.reshape(ss.nb, ss.t_rows, d_ff)[:, ss.t_valid - 2:ss.t_valid]]
        ys = [matmul(a, w_down16, d, tm=512, tn=512, tk=d_ff, out_dtypes=[F32], res=x)
              for a, x in zip((a_p, a_s), xs)]
        return ys, cs

    xs = [sp.x, ss.x]
    hs = norm(xs, w["norm_mix"][0])
    ret_w_out16 = w["ret_w_out"].astype(BF16)
    (proj_p,), (proj_s,) = matmul_ws(hs[0], hs[1], w["ret_w_in"], w["ret_w_in"].shape[-1], tm=512, tn=1024,
                                     out_dtypes=[BF16])
    o_ret, s_ret = [], []
    for st, proj in zip(streams, (proj_p, proj_s)):
        chunk = math.gcd(st.t_valid, RET_CHUNK)
        cpad = chunk if st.ctx is None else st.t_rows
        cos_r, sin_r = _rope_tables_half(st.pos, dk // 2, RET_THETA)
        o, s = retention(proj, st.ret_s0, cos_r, sin_r, nb=st.nb, heads=heads_r, dk=dk, dv=dv,
                         chunk=cpad, c_valid=chunk)
        o_ret.append(o)
        s_ret.append(s)
    xs = [matmul(o, ret_w_out16, d, tm=512, tn=512, tk=heads_r * dv, out_dtypes=[F32], res=x)
          for o, x in zip(o_ret, xs)]
    xs, conv0 = ffn(xs, 0)

    hk = norm(xs, w["kv_norm"])
    rope_p = _nsa_rope_tables(sp.pos, hd)
    rope_s = tuple(jnp.tile(tb, (ss.nb, 1)) for tb in _nsa_rope_tables(ss.pos, hd))
    (kv32_p, kv16_p), (kv32_s, kv16_s) = matmul_ws(
        hk[0], hk[1], w["kv_w"], 6 * kvh * hd, tm=tm, tn=kvh * hd, out_dtypes=[F32, BF16],
        rope=(rope_p, rope_s), rope_every=2, rope_half=rope_half)
    kv32, kv16 = (kv32_p, kv32_s), (kv16_p, kv16_s)
    kv_rows = [kv[:, :4 * kvh * hd].reshape(st.nb, st.t_rows, 4, kvh, hd)[:, :st.t_valid]
               for st, kv in zip(streams, kv32)]

    hs = norm(xs, w["norm_mix"][1])
    (q_p,), (q_s,) = matmul_ws(hs[0], hs[1], w["nsa_w_in"], qw, tm=512, tn=1024, out_dtypes=[BF16],
                               rope=(rope_p, rope_s), rope_every=1, rope_half=rope_half)
    w_pairs = _cmp_w1_pairs(w["cmp_w1"], hd)
    pe_term = cmp_pe_term(w["cmp_pe"], w["cmp_w1"])
    os, wins = [], []
    for st, h, q, k32, k16 in zip(streams, hs, (q_p, q_s), kv32, kv16):
        gates = matmul(h, w["nsa_w_gate16"], 3 * NSA_HEADS, tm=tm, tn=3 * NSA_HEADS, tk=d, out_dtypes=[F32],
                       act="sigmoid")
        o, win = _nsa_branches(st, q, gates, k32, k16, w_pairs, pe_term, w, hd=hd, kvh=kvh, group=group)
        os.append(o)
        wins.append(win)
    (x_p,), (x_s,) = matmul_ws(os[0], os[1], w["nsa_w_out"], d, tm=tm, tn=512, out_dtypes=[F32], res=xs)
    xs, conv1 = ffn([x_p, x_s], 1)
    ys = norm(xs, w["norm_final"], F32)
    convs = [jnp.stack([c0, c1]) for c0, c1 in zip(conv0, conv1)]
    return [(y, s[None], cv, kvr, win) for y, s, cv, kvr, win in zip(ys, s_ret, convs, kv_rows, wins)]


def kernel(x_prompt, x_sample, cache_kv, cache_win, state_ret, state_conv, page_table, norm_mix, norm_ffn,
           ret_w_in, ret_w_out, kv_norm, kv_w, cmp_pe, cmp_w1, cmp_w2, nsa_w_in, nsa_w_out, ffn_w_up, ffn_conv_w,
           ffn_conv_b, ffn_w_down, norm_final):
    b, t, d = x_prompt.shape
    db, dt, _ = x_sample.shape
    past = page_table.shape[1] * cache_kv.shape[1]
    qw = nsa_w_out.shape[1]
    w = dict(
        norm_mix=norm_mix, norm_ffn=norm_ffn, kv_norm=kv_norm, norm_final=norm_final,
        cmp_pe=cmp_pe, cmp_w1=cmp_w1, cmp_w2=cmp_w2, ffn_conv_w=ffn_conv_w, ffn_conv_b=ffn_conv_b,
        ret_w_in=ret_w_in[0], kv_w=kv_w, nsa_w_in=nsa_w_in[0], nsa_w_out=nsa_w_out[0], ffn_w_up=ffn_w_up,
        ret_w_out=ret_w_out[0], ffn_w_down=ffn_w_down, nsa_w_gate16=nsa_w_in[0][:, qw:].astype(BF16),
    )
    heads_r = RET_HEADS
    dk = d // heads_r
    ret0 = jnp.zeros((b, heads_r, dk, 2 * dk), F32)
    conv0 = jnp.zeros((state_conv.shape[0], b, 2, ffn_conv_w.shape[-1]), F32)
    tp = 2 * SUBLANES
    xs = jnp.zeros((db, tp, d), F32).at[:, :dt].set(x_sample).reshape(db * tp, d)
    sp = _Stream(x_prompt.reshape(b * t, d), b, t, t, 0, ret0, conv0, None)
    ss = _Stream(xs, db, tp, dt, past, state_ret[0], state_conv, (cache_kv, cache_win, page_table))
    (y_p, ret_p, conv_p, kv_p, win_p), (y_s, ret_s, conv_s, kv_s, win_s) = _trunk(sp, ss, w)
    y_s = y_s.reshape(db, tp, d)[:, :dt]
    return (y_p.reshape(b, t, d), y_s, kv_p, kv_s, win_p, win_s, ret_p, ret_s, conv_p, conv_s)
```

```python
import functools
import math

import jax
import jax.numpy as jnp
from jax import lax
from jax.experimental import pallas as pl
from jax.experimental.pallas import tpu as pltpu

F32 = jnp.float32
BF16 = jnp.bfloat16

RET_HEADS = 16
RET_CHUNK = 512
RET_THETA = 10000.0
NSA_HEADS = 32
NSA_KV_HEADS = 4
CMP_LEN = 32
CMP_STRIDE = 16
SLC_BLOCK = 64
N_SELECT = 16
WINDOW = 512
ROPE_THETA = 500000.0
PAGE_SIZE = 128
EPS = 1e-6
NEG = -1e30
FORCE = 1e9
TINY = 1e-20

LANES = 128
SUBLANES = 8
VMEM_LIMIT = 60 * 1024 * 1024


def _params(sem):
    return pltpu.CompilerParams(dimension_semantics=sem, vmem_limit_bytes=VMEM_LIMIT)


def _tile(dim, pref):
    if dim <= pref:
        return dim
    t = pref
    while dim % t:
        t //= 2
    return t


def _sigmoid(x):
    return 1.0 / (1.0 + jnp.exp(-x))


def _dot(a, b):
    return jnp.dot(a, b, preferred_element_type=F32)


def _dot_nt(a, b):
    return lax.dot_general(a, b, (((1,), (1,)), ((), ())), preferred_element_type=F32)


def _rmsnorm_kernel(x_ref, g_ref, o_ref):
    x = x_ref[...]
    y = x * lax.rsqrt(jnp.mean(x * x, axis=-1, keepdims=True) + EPS)
    o_ref[...] = (y * g_ref[...]).astype(o_ref.dtype)


def rmsnorm(x, g, out_dtype):
    m, d = x.shape
    tm = _tile(m, 512)
    return pl.pallas_call(
        _rmsnorm_kernel,
        grid=(m // tm,),
        in_specs=[pl.BlockSpec((tm, d), lambda i: (i, 0)),
                  pl.BlockSpec((1, d), lambda i: (0, 0))],
        out_specs=pl.BlockSpec((tm, d), lambda i: (i, 0)),
        out_shape=jax.ShapeDtypeStruct((m, d), out_dtype),
        compiler_params=_params(("parallel",)),
        name="rmsnorm",
    )(x, g.reshape(1, d).astype(F32))


def _rope_lanes(x, c, s_up, s_dn, half):
    return x * c + pltpu.roll(x, LANES - half, 1) * s_up + pltpu.roll(x, half, 1) * s_dn


def _mm_epilogue(acc, j, res_ref, rope_refs, out_refs, *, rope_every, rope_half, act):
    def store(val):
        for o in out_refs:
            o[...] = val.astype(o.dtype)

    if res_ref is not None:
        acc = acc + res_ref[...]
    if act == "sigmoid":
        acc = _sigmoid(acc)
    if not rope_every:
        store(acc)
        return
    tn = acc.shape[1]

    def roped():
        c, su, sd = (r[...] for r in rope_refs)
        pieces = [_rope_lanes(acc[:, a:a + LANES], c, su, sd, rope_half) for a in range(0, tn, LANES)]
        store(jnp.concatenate(pieces, axis=1))

    if rope_every == 1:
        roped()
    else:
        pl.when(j % rope_every == 0)(roped)
        pl.when(j % rope_every != 0)(lambda: store(acc))


def _mm_kernel(*refs, nk, has_res, rope_every, rope_half, act, n_out):
    x_ref, w_ref = refs[0], refs[1]
    pos = 2
    res_ref = None
    if has_res:
        res_ref = refs[pos]
        pos += 1
    rope_refs = None
    if rope_every:
        rope_refs = refs[pos:pos + 3]
        pos += 3
    out_refs = refs[pos:pos + n_out]
    acc_ref = refs[pos + n_out] if nk > 1 else None
    j = pl.program_id(1)
    k = pl.program_id(2)

    part = _dot(x_ref[...], w_ref[...].astype(BF16))
    finish = functools.partial(_mm_epilogue, j=j, res_ref=res_ref, rope_refs=rope_refs, out_refs=out_refs,
                               rope_every=rope_every, rope_half=rope_half, act=act)

    if nk == 1:
        finish(part)
    else:
        @pl.when(k == 0)
        def _():
            acc_ref[...] = part

        @pl.when(k > 0)
        def _():
            acc_ref[...] += part

        @pl.when(k == nk - 1)
        def _():
            finish(acc_ref[...])


def matmul(x, w, n, *, tm, tn, tk, out_dtypes, res=None, rope=None, rope_every=0, rope_half=0, act=None):
    m, kdim = x.shape
    tm, tn, tk = _tile(m, tm), _tile(n, tn), _tile(kdim, tk)
    nk = kdim // tk
    in_specs = [pl.BlockSpec((tm, tk), lambda i, j, k: (i, k)),
                pl.BlockSpec((tk, tn), lambda i, j, k: (k, j))]
    args = [x, w]
    if res is not None:
        in_specs.append(pl.BlockSpec((tm, tn), lambda i, j, k: (i, j)))
        args.append(res)
    if rope is not None:
        period = rope[0].shape[0]
        nper = period // tm
        for t in rope:
            in_specs.append(pl.BlockSpec((tm, LANES), lambda i, j, k: (i % nper, 0)))
            args.append(t)
    out_specs = [pl.BlockSpec((tm, tn), lambda i, j, k: (i, j)) for _ in out_dtypes]
    out_shape = [jax.ShapeDtypeStruct((m, n), dt) for dt in out_dtypes]
    scratch = [pltpu.VMEM((tm, tn), F32)] if nk > 1 else []
    outs = pl.pallas_call(
        functools.partial(_mm_kernel, nk=nk, has_res=res is not None,
                          rope_every=rope_every if rope is not None else 0, rope_half=rope_half,
                          act=act, n_out=len(out_dtypes)),
        grid=(m // tm, n // tn, nk),
        in_specs=in_specs, out_specs=out_specs, out_shape=out_shape,
        scratch_shapes=scratch,
        compiler_params=_params(("parallel", "parallel", "arbitrary")),
        name="matmul",
    )(*args)
    return outs if len(outs) > 1 else outs[0]


def _round_rows_job(step, first, src_ref, dst_ref, nblk):
    @pl.when((step >= first) & (step < first + nblk))
    def _():
        dst_ref[...] = src_ref[...].astype(BF16)


def _round_block_count(rows, max_blocks):
    rb = 64
    while rb < rows and (rows % rb or rows // rb > max_blocks):
        rb *= 2
    return rows // rb if rows % rb == 0 else 1


def _round_rows_specs(src, layer, nblk, first, step_of):
    rows, cols = src.shape[-2:]
    rb = rows // nblk

    def blk(*ids):
        return jnp.clip(step_of(*ids) - first, 0, nblk - 1)

    in_spec = pl.BlockSpec((None, rb, cols), lambda *ids: (layer, blk(*ids), 0))
    out_spec = pl.BlockSpec((rb, cols), lambda *ids: (blk(*ids), 0))
    return in_spec, out_spec, jax.ShapeDtypeStruct((rows, cols), BF16)


def _mm_ws_kernel(*refs, has_res, rope_every, rope_half, n_out):
    x_ref, xs_ref, w_ref = refs[:3]
    pos = 3
    res_ref = ress_ref = rope_refs = ropes_refs = None
    if has_res:
        res_ref, ress_ref = refs[pos:pos + 2]
        pos += 2
    if rope_every:
        rope_refs, ropes_refs = refs[pos:pos + 3], refs[pos + 3:pos + 6]
        pos += 6
    out_refs = refs[pos:pos + n_out]
    outs_refs = refs[pos + n_out:pos + 2 * n_out]
    w16 = refs[pos + 2 * n_out]
    j = pl.program_id(0)
    i = pl.program_id(1)

    @pl.when(i == 0)
    def _():
        w16[...] = w_ref[...].astype(BF16)

    kw = dict(rope_every=rope_every, rope_half=rope_half, act=None)
    _mm_epilogue(_dot(x_ref[...], w16[...]), j, res_ref, rope_refs, out_refs, **kw)

    @pl.when(i == pl.num_programs(1) - 1)
    def _():
        _mm_epilogue(_dot(xs_ref[...], w16[...]), j, ress_ref, ropes_refs, outs_refs, **kw)


def matmul_ws(x, xs, w, n, *, tm, tn, out_dtypes, res=None, rope=None, rope_every=0, rope_half=0):
    m, kdim = x.shape
    ms = xs.shape[0]
    tm, tn = _tile(m, tm), _tile(n, tn)
    in_specs = [pl.BlockSpec((tm, kdim), lambda j, i: (i, 0)),
                pl.BlockSpec((ms, kdim), lambda j, i: (0, 0)),
                pl.BlockSpec((kdim, tn), lambda j, i: (0, j))]
    args = [x, xs, w]
    if res is not None:
        in_specs += [pl.BlockSpec((tm, tn), lambda j, i: (i, j)), pl.BlockSpec((ms, tn), lambda j, i: (0, j))]
        args += list(res)
    if rope is not None:
        nper = rope[0][0].shape[0] // tm
        in_specs += [pl.BlockSpec((tm, LANES), lambda j, i: (i % nper, 0))] * 3
        in_specs += [pl.BlockSpec((ms, LANES), lambda j, i: (0, 0))] * 3
        args += list(rope[0]) + list(rope[1])
    out_specs = ([pl.BlockSpec((tm, tn), lambda j, i: (i, j)) for _ in out_dtypes]
                 + [pl.BlockSpec((ms, tn), lambda j, i: (0, j)) for _ in out_dtypes])
    out_shape = ([jax.ShapeDtypeStruct((m, n), dt) for dt in out_dtypes]
                 + [jax.ShapeDtypeStruct((ms, n), dt) for dt in out_dtypes])
    outs = pl.pallas_call(
        functools.partial(_mm_ws_kernel, has_res=res is not None,
                          rope_every=rope_every if rope is not None else 0, rope_half=rope_half,
                          n_out=len(out_dtypes)),
        grid=(n // tn, m // tm),
        in_specs=in_specs, out_specs=out_specs, out_shape=out_shape,
        scratch_shapes=[pltpu.VMEM((kdim, tn), BF16)],
        compiler_params=_params(("parallel", "arbitrary")),
        name="matmul_ws",
    )(*args)
    k = len(out_dtypes)
    return outs[:k], outs[k:]


def _ffn_up_kernel(x_ref, xs_ref, wv_ref, wg_ref, cw_ref, cb_ref, st_ref, f1_ref, f2_ref,
                   a_ref, tail_ref, as_ref, tails_ref, w16, carry_ref, *, tiles_per_seq, seg):
    i = pl.program_id(1)
    tn = wv_ref.shape[1]

    @pl.when(i == 0)
    def _():
        w16[:, :tn] = wv_ref[...].astype(BF16)
        w16[:, tn:] = wg_ref[...].astype(BF16)

    def val_gate(x):
        r = _dot(x, w16[...])
        return r[:, :tn], r[:, tn:]

    cb = cb_ref[...]
    cw0, cw1, cw2 = cw_ref[0:1, :], cw_ref[1:2, :], cw_ref[2:3, :]

    def gated(val, gate, g1, g2):
        conv = cb + cw0 * g2 + cw1 * g1 + cw2 * gate
        return (val * (conv * _sigmoid(conv))).astype(BF16)

    @pl.when(i % tiles_per_seq == 0)
    def _():
        carry_ref[...] = st_ref[0]

    tm = x_ref.shape[0]
    row = lax.broadcasted_iota(jnp.int32, (SUBLANES, 1), 0)
    val, gate = val_gate(x_ref[...])
    c0 = carry_ref[0:1, :]
    c1 = carry_ref[1:2, :]
    g1 = pltpu.roll(gate, 1, 0)
    g2 = pltpu.roll(gate, 2, 0)
    g1 = jnp.concatenate([jnp.where(row == 0, c1, g1[:SUBLANES]), g1[SUBLANES:]], axis=0)
    g2 = jnp.concatenate([jnp.where(row == 0, c0, jnp.where(row == 1, c1, g2[:SUBLANES])), g2[SUBLANES:]], axis=0)
    a_ref[...] = gated(val, gate, g1, g2)
    carry_ref[...] = gate[tm - 2:tm, :]
    nt = tail_ref.shape[1]
    tail_ref[0] = gate[tm - nt:tm, :]

    @pl.when(i == pl.num_programs(1) - 1)
    def _():
        val, gate = val_gate(xs_ref[...])
        t = jnp.bitwise_and(lax.broadcasted_iota(jnp.int32, (val.shape[0], 1), 0), seg - 1)
        g1 = jnp.where(t == 0, f1_ref[...], pltpu.roll(gate, 1, 0))
        g2 = jnp.where(t < 2, f2_ref[...], pltpu.roll(gate, 2, 0))
        as_ref[...] = gated(val, gate, g1, g2)
        tails_ref[...] = gate


def ffn_up(h, hs, w_up, layer, d_ff, conv_w, conv_b, *, tm, state, fill, seg):
    m, kdim = h.shape
    ms = hs.shape[0]
    assert seg & (seg - 1) == 0
    tn = _tile(d_ff, 256)
    nj = d_ff // tn
    tm = _tile(m, tm)
    nseq = state.shape[0]
    tps = (m // nseq) // tm
    a, tail, a_s, tail_s = pl.pallas_call(
        functools.partial(_ffn_up_kernel, tiles_per_seq=tps, seg=seg),
        grid=(nj, m // tm),
        in_specs=[pl.BlockSpec((tm, kdim), lambda j, i: (i, 0)),
                  pl.BlockSpec((ms, kdim), lambda j, i: (0, 0)),
                  pl.BlockSpec((None, kdim, tn), lambda j, i: (layer, 0, j)),
                  pl.BlockSpec((None, kdim, tn), lambda j, i: (layer, 0, nj + j)),
                  pl.BlockSpec((3, tn), lambda j, i: (0, j)),
                  pl.BlockSpec((1, tn), lambda j, i: (0, j)),
                  pl.BlockSpec((1, 2, tn), lambda j, i: (i // tps, 0, j)),
                  pl.BlockSpec((ms, tn), lambda j, i: (0, j)),
                  pl.BlockSpec((ms, tn), lambda j, i: (0, j))],
        out_specs=[pl.BlockSpec((tm, tn), lambda j, i: (i, j)),
                   pl.BlockSpec((1, SUBLANES, tn), lambda j, i: (i // tps, 0, j)),
                   pl.BlockSpec((ms, tn), lambda j, i: (0, j)),
                   pl.BlockSpec((ms, tn), lambda j, i: (0, j))],
        out_shape=[jax.ShapeDtypeStruct((m, d_ff), BF16),
                   jax.ShapeDtypeStruct((nseq, SUBLANES, d_ff), F32),
                   jax.ShapeDtypeStruct((ms, d_ff), BF16),
                   jax.ShapeDtypeStruct((ms, d_ff), F32)],
        scratch_shapes=[pltpu.VMEM((kdim, 2 * tn), BF16), pltpu.VMEM((2, tn), F32)],
        compiler_params=_params(("parallel", "arbitrary")),
        name="ffn_up",
    )(h, hs, w_up, w_up, conv_w, conv_b.reshape(1, d_ff), state, fill[0], fill[1])
    return (a, tail), (a_s, tail_s)


def _ret_kernel(lg_ref, q_ref, k_ref, v_ref, g_ref, cos_ref, sin_ref, s0_ref, *rest, c_valid, dk, round_blocks):
    nr = len(round_blocks)
    o_ref, s_out_ref = rest[nr:nr + 2]
    s_scr, intra_scr = rest[2 * nr + 2:]
    step = (pl.program_id(0) * pl.num_programs(1) + pl.program_id(1)) * pl.num_programs(2) + pl.program_id(2)
    for r, (first, nblk) in enumerate(round_blocks):
        _round_rows_job(step, first, rest[r], rest[nr + 2 + r], nblk)
    h = pl.program_id(1)
    c = pl.program_id(2)
    nc = pl.num_programs(2)
    half = dk // 2
    lg = lg_ref[h]
    cp = q_ref.shape[0]

    @pl.when(c == 0)
    def _():
        s_scr[...] = s0_ref[0, 0]
        di = lax.broadcasted_iota(jnp.int32, (cp, cp), 0)
        dj = lax.broadcasted_iota(jnp.int32, (cp, cp), 1)
        diff = di - dj
        intra_scr[...] = jnp.where((diff >= 0) & (dj < c_valid),
                                   jnp.exp(lg * jnp.maximum(diff, 0).astype(F32)), 0.0)

    cos = cos_ref[...]
    sin = sin_ref[...]

    def rope(x):
        x1 = x[:, :half]
        x2 = x[:, half:]
        return jnp.concatenate([x1 * cos - x2 * sin, x2 * cos + x1 * sin], axis=1)

    q = rope(q_ref[...].astype(F32))
    k = rope(k_ref[...].astype(F32)) * (dk ** -0.5)
    v = v_ref[...]
    ri = lax.broadcasted_iota(jnp.int32, (cp, 1), 0)
    rif = ri.astype(F32)
    read_decay = jnp.exp(lg * (rif + 1.0))
    write_decay = jnp.where(ri < c_valid, jnp.exp(lg * jnp.maximum(c_valid - 1.0 - rif, 0.0)), 0.0)

    s = s_scr[...]
    att = _dot_nt(q.astype(BF16), k.astype(BF16)) * intra_scr[...]
    o = _dot(att.astype(BF16), v) + _dot((q * read_decay).astype(BF16), s.astype(BF16))
    kw = (k * write_decay).astype(BF16)
    chunk_decay = jnp.exp(lg * jnp.full((1, 1), float(c_valid), F32))
    s_new = s * chunk_decay + lax.dot_general(kw, v, (((0,), (0,)), ((), ())), preferred_element_type=F32)
    s_scr[...] = s_new

    @pl.when(c == nc - 1)
    def _():
        s_out_ref[0, 0] = s_new

    on = o * lax.rsqrt(jnp.mean(o * o, axis=-1, keepdims=True) + EPS)
    gate = g_ref[...].astype(F32)
    o_ref[...] = (on * (gate * _sigmoid(gate))).astype(o_ref.dtype)


def retention(proj, s0, cos, sin, *, nb, heads, dk, dv, chunk, c_valid, round_rows=()):
    m = proj.shape[0]
    t = m // nb
    nc = t // chunk
    qb = heads
    vb = 2 * heads * dk // dv
    log_g = jnp.log1p(-jnp.exp2(-5.0 - jnp.arange(heads, dtype=F32)))
    in_specs = [
        pl.BlockSpec((chunk, dk), lambda b, h, c, lg: (b * nc + c, h)),
        pl.BlockSpec((chunk, dk), lambda b, h, c, lg: (b * nc + c, qb + h)),
        pl.BlockSpec((chunk, dv), lambda b, h, c, lg: (b * nc + c, vb + h)),
        pl.BlockSpec((chunk, dv), lambda b, h, c, lg: (b * nc + c, vb + heads + h)),
        pl.BlockSpec((chunk, dk // 2), lambda b, h, c, lg: (c, 0)),
        pl.BlockSpec((chunk, dk // 2), lambda b, h, c, lg: (c, 0)),
        pl.BlockSpec((1, 1, dk, dv), lambda b, h, c, lg: (b, h, 0, 0)),
    ]
    out_specs = [
        pl.BlockSpec((chunk, dv), lambda b, h, c, lg: (b * nc + c, h)),
        pl.BlockSpec((1, 1, dk, dv), lambda b, h, c, lg: (b, h, 0, 0)),
    ]
    out_shape = [jax.ShapeDtypeStruct((m, heads * dv), BF16),
                 jax.ShapeDtypeStruct((nb, heads, dk, dv), F32)]
    args = [log_g, proj, proj, proj, proj, cos, sin, s0]
    round_blocks = []
    for src, layer, nblk in round_rows:
        first = sum(n for _, n in round_blocks)
        assert first + nblk <= nb * heads * nc
        r_in, r_out, r_shape = _round_rows_specs(src, layer, nblk, first,
                                                 lambda b, h, c, lg: (b * heads + h) * nc + c)
        in_specs.append(r_in)
        out_specs.append(r_out)
        out_shape.append(r_shape)
        args.append(src)
        round_blocks.append((first, nblk))
    grid_spec = pltpu.PrefetchScalarGridSpec(
        num_scalar_prefetch=1,
        grid=(nb, heads, nc),
        in_specs=in_specs,
        out_specs=out_specs,
        scratch_shapes=[pltpu.VMEM((dk, dv), F32), pltpu.VMEM((chunk, chunk), F32)],
    )
    sem = ("arbitrary",) * 3 if round_blocks else ("parallel", "parallel", "arbitrary")
    return pl.pallas_call(
        functools.partial(_ret_kernel, c_valid=c_valid, dk=dk, round_blocks=tuple(round_blocks)),
        grid_spec=grid_spec,
        out_shape=out_shape,
        compiler_params=_params(sem),
        name="retention",
    )(*args)


def _cmp_project(planes, w_ref, p_ref, kvh, hd):
    n_seg = planes.shape[1] // CMP_STRIDE
    for br in range(2):
        acc = jnp.zeros((kvh * n_seg, 2 * hd), F32)
        for sp in range(CMP_STRIDE // 2):
            pieces = []
            for g in range(kvh):
                plane = planes.at[br * kvh + g]
                a = plane[pl.ds(2 * sp, n_seg, stride=CMP_STRIDE), :]
                b = plane[pl.ds(2 * sp + 1, n_seg, stride=CMP_STRIDE), :]
                pieces.append(jnp.concatenate([a, b], axis=1))
            lhs = jnp.concatenate(pieces, axis=0).astype(BF16)
            acc = acc + _dot(lhs, w_ref[br, sp])
        for g in range(kvh):
            p_ref[0, br * kvh + g] = acc[g * n_seg:(g + 1) * n_seg, :]


def _cmp_proj_rows_kernel(x_ref, w_ref, p_ref, r_ref, *, kvh, hd):
    for hh in range(2 * kvh):
        r_ref[hh] = x_ref[:, hh * hd:(hh + 1) * hd]
    _cmp_project(r_ref, w_ref, p_ref, kvh, hd)


def _cmp_proj_pages_kernel(pt_ref, cache_ref, w_ref, p_ref, r_ref, sem, *, kvh, hd, pages_per_step, n_pages):
    b = pl.program_id(0)
    c = pl.program_id(1)
    nchunk = pl.num_programs(1)
    step = b * nchunk + c
    nsteps = pl.num_programs(0) * nchunk
    page = cache_ref.shape[1]

    def copies(st, slot):
        bb = st // nchunk
        cc = st % nchunk
        out = []
        for u in range(pages_per_step):
            phys = pt_ref[bb * n_pages + cc * pages_per_step + u]
            for hh in range(2 * kvh):
                out.append(pltpu.make_async_copy(
                    cache_ref.at[phys, :, hh // kvh, hh % kvh, :],
                    r_ref.at[slot, hh, pl.ds(u * page, page), :],
                    sem.at[slot]))
        return out

    slot = step % 2

    @pl.when(step == 0)
    def _():
        for cp in copies(step, slot):
            cp.start()

    @pl.when(step + 1 < nsteps)
    def _():
        for cp in copies(step + 1, 1 - slot):
            cp.start()

    for cp in copies(step, slot):
        cp.wait()
    _cmp_project(r_ref.at[slot], w_ref, p_ref, kvh, hd)


def _cmp_w1_pairs(cmp_w1, hd):
    r = CMP_LEN // CMP_STRIDE
    e = cmp_w1.shape[-1]
    w = cmp_w1.reshape(2, r, CMP_STRIDE // 2, 2, hd, e)
    w = w.transpose(0, 2, 3, 4, 1, 5)
    return w.reshape(2, CMP_STRIDE // 2, 2 * hd, r * e).astype(BF16)


def cmp_proj_rows(rows2d, nb, t, w_pairs, kvh, hd):
    n_seg = t // CMP_STRIDE
    width = 2 * kvh * hd
    return pl.pallas_call(
        functools.partial(_cmp_proj_rows_kernel, kvh=kvh, hd=hd),
        grid=(nb,),
        in_specs=[pl.BlockSpec((t, width), lambda b: (b, 0)),
                  pl.BlockSpec(w_pairs.shape, lambda b: (0, 0, 0, 0))],
        out_specs=pl.BlockSpec((1, 2 * kvh, n_seg, 2 * hd), lambda b: (b, 0, 0, 0)),
        out_shape=jax.ShapeDtypeStruct((nb, 2 * kvh, n_seg, 2 * hd), F32),
        scratch_shapes=[pltpu.VMEM((2 * kvh, t, hd), F32)],
        compiler_params=_params(("parallel",)),
        name="cmp_proj_prompt",
    )(rows2d, w_pairs)


def cmp_proj_pages(cache, page_table, w_pairs, kvh, hd, pages_per_step):
    nb, n_pages = page_table.shape
    page = cache.shape[1]
    n_seg = pages_per_step * page // CMP_STRIDE
    nchunk = n_pages // pages_per_step
    grid_spec = pltpu.PrefetchScalarGridSpec(
        num_scalar_prefetch=1,
        grid=(nb, nchunk),
        in_specs=[pl.BlockSpec(memory_space=pl.ANY),
                  pl.BlockSpec(w_pairs.shape, lambda b, c, pt: (0, 0, 0, 0))],
        out_specs=pl.BlockSpec((1, 2 * kvh, n_seg, 2 * hd), lambda b, c, pt: (b, 0, c, 0)),
        scratch_shapes=[pltpu.VMEM((2, 2 * kvh, pages_per_step * page, hd), F32),
                        pltpu.SemaphoreType.DMA((2,))],
    )
    return pl.pallas_call(
        functools.partial(_cmp_proj_pages_kernel, kvh=kvh, hd=hd, pages_per_step=pages_per_step,
                          n_pages=n_pages),
        grid_spec=grid_spec,
        out_shape=jax.ShapeDtypeStruct((nb, 2 * kvh, nchunk * n_seg, 2 * hd), F32),
        compiler_params=_params(("arbitrary", "arbitrary")),
        name="cmp_proj_sample",
    )(page_table.reshape(-1), cache, w_pairs)


def _cmp_pe_kernel(pe_ref, w_ref, o_ref):
    o_ref[0] = _dot(pe_ref[0], w_ref[0])


def cmp_pe_term(cmp_pe, cmp_w1):
    _, n, hd = cmp_pe.shape
    e = cmp_w1.shape[-1]
    pe_flat = jnp.zeros((2, SUBLANES, n * hd), BF16).at[:, 0].set(cmp_pe.reshape(2, n * hd).astype(BF16))
    w_flat = cmp_w1.reshape(2, n * hd, e).astype(BF16)
    return pl.pallas_call(
        _cmp_pe_kernel,
        grid=(2,),
        in_specs=[pl.BlockSpec((1, SUBLANES, n * hd), lambda b: (b, 0, 0)),
                  pl.BlockSpec((1, n * hd, e), lambda b: (b, 0, 0))],
        out_specs=pl.BlockSpec((1, SUBLANES, e), lambda b: (b, 0, 0)),
        out_shape=jax.ShapeDtypeStruct((2, SUBLANES, e), F32),
        compiler_params=_params(("parallel",)),
        name="cmp_pe",
    )(pe_flat, w_flat)


def _cmp_finish_kernel(p_ref, pe_ref, w2_ref, o_ref):
    p = p_ref[0, 0]
    n_seg = p.shape[0]
    e = p.shape[1] // 2
    hid = p[:, :e] + pltpu.roll(p[:, e:], n_seg - 1, 0) + pe_ref[0, 0:1, :]
    c = math.sqrt(2.0 / math.pi)
    act = 0.5 * hid * (1.0 + jnp.tanh(c * (hid + 0.044715 * (hid * hid * hid))))
    o_ref[0, 0] = _dot(act.astype(BF16), w2_ref[0]).astype(o_ref.dtype)


def cmp_finish(p, pe_term, cmp_w2, kvh):
    nb, nu, n_seg, e2 = p.shape
    e = e2 // 2
    hd = cmp_w2.shape[-1]
    return pl.pallas_call(
        _cmp_finish_kernel,
        grid=(nb, nu),
        in_specs=[pl.BlockSpec((1, 1, n_seg, e2), lambda b, u: (b, u, 0, 0)),
                  pl.BlockSpec((1, SUBLANES, e), lambda b, u: (u // kvh, 0, 0)),
                  pl.BlockSpec((1, e, hd), lambda b, u: (u // kvh, 0, 0))],
        out_specs=pl.BlockSpec((1, 1, n_seg, hd), lambda b, u: (b, u, 0, 0)),
        out_shape=jax.ShapeDtypeStruct((nb, nu, n_seg, hd), BF16),
        compiler_params=_params(("parallel", "parallel")),
        name="cmp_finish",
    )(p, pe_term, cmp_w2.astype(BF16))


def _cmp_attn_kernel(q_ref, kc_ref, vc_ref, agg_ref, o_ref, imp_ref, *, group, hd, n_c, pos0):
    qi = pl.program_id(2)
    tq = q_ref.shape[0]
    kc = kc_ref[0, 0]
    vc = vc_ref[0, 0]
    agg = agg_ref[...]
    n_cp = kc.shape[0]
    qs = jnp.concatenate([q_ref[:, h * hd:(h + 1) * hd] for h in range(group)], axis=0)
    qpos1 = pos0 + qi * tq + lax.broadcasted_iota(jnp.int32, (tq, 1), 0)
    qpos = jnp.concatenate([qpos1] * group, axis=0)
    ci = lax.broadcasted_iota(jnp.int32, (1, n_cp), 1)
    cm = ((ci * CMP_STRIDE + (CMP_LEN - 1)) <= qpos) & (ci < n_c)
    s = jnp.where(cm, _dot_nt(qs, kc) * (hd ** -0.5), NEG)
    e = jnp.where(cm, jnp.exp(s - jnp.max(s, axis=-1, keepdims=True)), 0.0)
    p = e / jnp.maximum(jnp.sum(e, axis=-1, keepdims=True), TINY)
    pb = p.astype(BF16)
    o = _dot(pb, vc)
    psum = jnp.zeros((tq, n_cp), F32)
    for h in range(group):
        o_ref[:, h * hd:(h + 1) * hd] = o[h * tq:(h + 1) * tq, :].astype(o_ref.dtype)
        psum = psum + pb[h * tq:(h + 1) * tq, :].astype(F32)
    hi = psum.astype(BF16)
    lo = (psum - hi.astype(F32)).astype(BF16)
    imp_ref[0, 0] = _dot_nt(agg, hi) + _dot_nt(agg, lo)


def _slc_aggregation_t(n_cp, n_slc, rows):
    rs = SLC_BLOCK // CMP_STRIDE
    rc = CMP_LEN // CMP_STRIDE
    i = jnp.arange(n_cp)[None, :]
    j = jnp.arange(rows)[:, None]
    w = sum((i == j * rs + m - n).astype(F32) for m in range(rs) for n in range(rc))
    return jnp.where(j < n_slc, w, 0.0).astype(BF16)


def cmp_attn(q, kvc, *, nb, kvh, group, hd, n_c, n_slc, pos0, tq):
    m = q.shape[0]
    t = m // nb
    tq = _tile(t, tq)
    nq = t // tq
    n_cp = kvc.shape[2]
    rows = -(-n_slc // SUBLANES) * SUBLANES
    agg_t = _slc_aggregation_t(n_cp, n_slc, rows)
    gw = group * hd
    return pl.pallas_call(
        functools.partial(_cmp_attn_kernel, group=group, hd=hd, n_c=n_c, pos0=pos0),
        grid=(nb, kvh, nq),
        in_specs=[pl.BlockSpec((tq, gw), lambda b, g, i: (b * nq + i, g)),
                  pl.BlockSpec((1, 1, n_cp, hd), lambda b, g, i: (b, g, 0, 0)),
                  pl.BlockSpec((1, 1, n_cp, hd), lambda b, g, i: (b, kvh + g, 0, 0)),
                  pl.BlockSpec((rows, n_cp), lambda b, g, i: (0, 0))],
        out_specs=[pl.BlockSpec((tq, gw), lambda b, g, i: (b * nq + i, g)),
                   pl.BlockSpec((1, 1, rows, tq), lambda b, g, i: (b, g, 0, i))],
        out_shape=[jax.ShapeDtypeStruct((m, kvh * gw), BF16),
                   jax.ShapeDtypeStruct((nb, kvh, rows, t), F32)],
        compiler_params=_params(("parallel", "parallel", "parallel")),
        name="cmp_attn",
    )(q, kvc, kvc, agg_t)


def _select_kernel(imp_ref, *rest, n_slc, pos0, t_len, emit_idx):
    if emit_idx:
        idx_ref, ok_ref, sc_ref = rest
    else:
        bias_ref, sc_ref = rest
    rows, width = imp_ref.shape
    lane = pl.program_id(0) * width + lax.broadcasted_iota(jnp.int32, (1, width), 1)
    qpos = pos0 + jnp.bitwise_and(lane, t_len - 1)
    blk = lax.broadcasted_iota(jnp.int32, (rows, width), 0)
    cur = jnp.right_shift(qpos, SLC_BLOCK.bit_length() - 1)
    visible = blk * SLC_BLOCK <= qpos
    forced = (blk == 0) | (blk == cur) | (blk == cur - 1)
    score = jnp.where(visible, jnp.where(forced, FORCE, imp_ref[...]), NEG)
    sc_ref[...] = score

    def count(i, cnt):
        si = sc_ref[pl.ds(i, 1), :]
        beats = (si > score) | ((si == score) & (i < blk))
        return cnt + jnp.where(beats, 1, 0)

    rank = lax.fori_loop(0, n_slc, count, jnp.zeros((rows, width), jnp.int32))
    in_range = blk < n_slc
    if emit_idx:
        blk_f = blk.astype(F32)
        for r in range(idx_ref.shape[0]):
            hit = (rank == r) & in_range
            idx_r = jnp.sum(jnp.where(hit, blk_f, 0.0), axis=0, keepdims=True)
            idx_ref[r:r + 1, :] = idx_r.astype(jnp.int32)
            ok_ref[r:r + 1, :] = jnp.sum(jnp.where(hit & visible, 1.0, 0.0), axis=0, keepdims=True)
    else:
        chosen = (rank < N_SELECT) & visible & in_range
        bias_ref[...] = jnp.where(chosen, 0.0, NEG).astype(bias_ref.dtype)


def select_blocks(imp, *, n_slc, pos0, emit_idx):
    nb, kvh, rows, t = imp.shape
    assert t & (t - 1) == 0
    lanes = nb * kvh * t
    imp2 = imp.transpose(2, 0, 1, 3).reshape(rows, lanes)
    width = _tile(lanes, 2048)
    spec = pl.BlockSpec((rows, width), lambda i: (0, i))
    if emit_idx:
        nsel = min(N_SELECT, n_slc)
        o_spec = pl.BlockSpec((nsel, width), lambda i: (0, i))
        out_specs = [o_spec, o_spec]
        out_shape = [jax.ShapeDtypeStruct((nsel, lanes), jnp.int32), jax.ShapeDtypeStruct((nsel, lanes), F32)]
    else:
        out_specs = [spec]
        out_shape = [jax.ShapeDtypeStruct((rows, lanes), BF16)]
    outs = pl.pallas_call(
        functools.partial(_select_kernel, n_slc=n_slc, pos0=pos0, t_len=t, emit_idx=emit_idx),
        grid=(lanes // width,),
        in_specs=[spec], out_specs=out_specs, out_shape=out_shape,
        scratch_shapes=[pltpu.VMEM((rows, width), F32)],
        compiler_params=_params(("parallel",)),
        name="select_blocks",
    )(imp2)
    if emit_idx:
        return tuple(o.reshape(-1, nb, kvh, t).transpose(1, 2, 0, 3) for o in outs)
    return outs[0].reshape(rows, nb, kvh, t).transpose(1, 2, 3, 0)


V_ROWS_PAD = 2 * SUBLANES


def _prompt_attn_kernel(q_ref, bias_ref, ks_in, vs_in, kw_ref, vw_in, hot_ref, oc_ref, gc_ref, gs_ref, gw_ref,
                        rsrc_ref, o_ref, rdst_ref, ks_ref, vts_ref, vtw_ref, *, nq, hd, round_blocks):
    qi = pl.program_id(2)
    hs = pl.program_id(3)
    step = ((pl.program_id(0) * pl.num_programs(1) + pl.program_id(1)) * nq + qi) * pl.num_programs(3) + hs
    _round_rows_job(step, 0, rsrc_ref, rdst_ref, round_blocks)
    tq = q_ref.shape[0]
    n_heads = q_ref.shape[1] // hd
    kl = lax.broadcasted_iota(jnp.int32, (tq, tq), 0)
    ql = lax.broadcasted_iota(jnp.int32, (tq, tq), 1)

    @pl.when((qi == 0) & (hs == 0))
    def _():
        ks_ref[:, :hd] = ks_in[...]
        ks_ref[:, hd:] = hot_ref[...]
        pad = vts_ref.shape[0] - hd
        ones_row = jnp.where(lax.broadcasted_iota(jnp.int32, (pad, vts_ref.shape[1]), 0) == 0, 1.0, 0.0)
        for src, dst in ((vs_in, vts_ref), (vw_in, vtw_ref)):
            dst[:hd, :] = src[...].astype(F32).T.astype(BF16)
            dst[hd:, :] = ones_row.astype(BF16)

    def branch(k_ref, vt_ref, q, k0, nkeys, band):
        st = _dot_nt(k_ref[k0:k0 + nkeys, :], q)
        parts = []
        if nkeys > tq:
            top = st[:nkeys - tq, :]
            if band:
                top = jnp.where(kl > ql, top, NEG)
            parts.append(top)
        parts.append(jnp.where(kl <= ql, st[nkeys - tq:, :], NEG))
        st = jnp.concatenate(parts, axis=0) if len(parts) > 1 else parts[0]
        p = jnp.exp2(st - jnp.max(st, axis=0, keepdims=True)).astype(BF16)
        ot = _dot(vt_ref[:, k0:k0 + nkeys], p)
        return ot[:hd, :] / ot[hd:hd + 1, :]

    def run(v):
        for hp in range(n_heads):
            sl = slice(hp * hd, (hp + 1) * hd)
            row = pl.ds(hs * n_heads + hp, 1)
            q = (q_ref[:, sl].astype(F32) * (hd ** -0.5 * math.log2(math.e))).astype(BF16)
            o_s = branch(ks_ref, vts_ref, jnp.concatenate([q, bias_ref[0, 0]], axis=1), 0, (v + 1) * tq, False)
            o_w = branch(kw_ref, vtw_ref, q, max(v - 1, 0) * tq, min(v + 1, 2) * tq, v > 0)
            o_c = oc_ref[:, sl].astype(F32).T
            o_t = o_c * gc_ref[row, :] + o_s * gs_ref[row, :] + o_w * gw_ref[row, :]
            o_ref[:, sl] = o_t.T.astype(o_ref.dtype)

    for v in range(nq):
        pl.when(qi == v)(functools.partial(run, v))


def prompt_attn(q, o_cmp, gates_t, bias, kv16, round_rows, *, nb, kvh, group, hd, tq, heads_per_step=4):
    m = q.shape[0]
    t = m // nb
    tq = _tile(t, tq)
    nq = t // tq
    assert tq == WINDOW or nq == 1
    hw = heads_per_step * hd
    hsteps = group // heads_per_step
    head_spec = pl.BlockSpec((tq, hw), lambda b, g, i, h: (b * nq + i, g * hsteps + h))
    onehot = (jnp.arange(t)[:, None] // SLC_BLOCK == jnp.arange(hd)[None, :]).astype(BF16)

    def kv_spec(sec):
        return pl.BlockSpec((t, hd), lambda b, g, i, h: (b, sec * kvh + g))

    def gate_spec(branch):
        return pl.BlockSpec((group, tq), lambda b, g, i, h: (branch * kvh + g, b * nq + i))

    src, layer = round_rows
    round_blocks = _round_block_count(src.shape[-2], nb * kvh * nq * hsteps)
    r_in, r_out, r_shape = _round_rows_specs(src, layer, round_blocks, 0,
                                             lambda b, g, i, h: ((b * kvh + g) * nq + i) * hsteps + h)
    return pl.pallas_call(
        functools.partial(_prompt_attn_kernel, nq=nq, hd=hd, round_blocks=round_blocks),
        grid=(nb, kvh, nq, hsteps),
        in_specs=[head_spec,
                  pl.BlockSpec((1, 1, tq, hd), lambda b, g, i, h: (b, g, i, 0)),
                  kv_spec(2), kv_spec(3), kv_spec(4), kv_spec(5),
                  pl.BlockSpec((t, hd), lambda b, g, i, h: (0, 0)),
                  head_spec, gate_spec(0), gate_spec(1), gate_spec(2), r_in],
        out_specs=[head_spec, r_out],
        out_shape=[jax.ShapeDtypeStruct((m, kvh * group * hd), BF16), r_shape],
        scratch_shapes=[pltpu.VMEM((t, 2 * hd), BF16),
                        pltpu.VMEM((hd + V_ROWS_PAD, t), BF16),
                        pltpu.VMEM((hd + V_ROWS_PAD, t), BF16)],
        compiler_params=_params(("arbitrary", "arbitrary", "arbitrary", "arbitrary")),
        name="prompt_attn",
    )(q, bias, kv16, kv16, kv16, kv16, onehot, o_cmp, gates_t, gates_t, gates_t, src)


def _sample_attn_kernel(idx_ref, pt_ref, q_ref, kpos_ref, okx_ref, kn_ref, vn_ref, wk_ref, wv_ref,
                        wkn_ref, wvn_ref, cache_ref, oslc_ref, owin_ref, kbuf, vbuf, sem,
                        *, kvh, group, hd, t_valid, past, nsel, ks_sec, vs_sec):
    b = pl.program_id(0)
    g = pl.program_id(1)
    tp = q_ref.shape[2] // group
    n_blocks = past // SLC_BLOCK
    per_page = PAGE_SIZE // SLC_BLOCK
    n_fetch = t_valid * nsel

    step = b * kvh + g
    nsteps = pl.num_programs(0) * kvh
    slot = step % 2

    def copies(f, bb, gg, sl):
        t = f // nsel
        r = f % nsel
        blk = idx_ref[((bb * kvh + gg) * nsel + r) * tp + t]
        blk = jnp.minimum(blk, n_blocks - 1)
        page = pt_ref[bb * (past // PAGE_SIZE) + blk // per_page]
        row0 = (blk % per_page) * SLC_BLOCK
        dst = pl.ds(f * SLC_BLOCK, SLC_BLOCK)
        ck = pltpu.make_async_copy(cache_ref.at[page, pl.ds(row0, SLC_BLOCK), ks_sec, gg, :],
                                   kbuf.at[sl, dst, :], sem.at[sl, 0])
        cv = pltpu.make_async_copy(cache_ref.at[page, pl.ds(row0, SLC_BLOCK), vs_sec, gg, :],
                                   vbuf.at[sl, dst, :], sem.at[sl, 1])
        return ck, cv

    def start_all(bb, gg, sl):
        def body(f, c):
            ck, cv = copies(f, bb, gg, sl)
            ck.start()
            cv.start()
            return c

        lax.fori_loop(0, n_fetch, body, 0)

    def wait(f, c):
        ck, cv = copies(f, b, g, slot)
        ck.wait()
        cv.wait()
        return c

    @pl.when(step == 0)
    def _():
        start_all(b, g, slot)

    @pl.when(step + 1 < nsteps)
    def _():
        start_all((step + 1) // kvh, (step + 1) % kvh, 1 - slot)

    scale = hd ** -0.5
    qa = q_ref[0, 0].astype(BF16)
    trow = jnp.right_shift(lax.broadcasted_iota(jnp.int32, (tp * group, 1), 0), group.bit_length() - 1)
    wk = wk_ref[0].astype(BF16)
    wrows = wk.shape[0]
    s_c = _dot_nt(qa, wk) * scale
    s_n = _dot_nt(qa, wkn_ref[...].astype(BF16)) * scale
    d_c = (past + trow) - (past - wrows + lax.broadcasted_iota(jnp.int32, (1, wrows), 1))
    jn = lax.broadcasted_iota(jnp.int32, (1, tp), 1)
    d_n = trow - jn
    m_c = (d_c >= 0) & (d_c < WINDOW)
    m_n = (d_n >= 0) & (d_n < WINDOW) & (jn < t_valid)
    s_c = jnp.where(m_c, s_c, NEG)
    s_n = jnp.where(m_n, s_n, NEG)
    mx = jnp.maximum(jnp.max(s_c, axis=-1, keepdims=True), jnp.max(s_n, axis=-1, keepdims=True))
    e_c = jnp.where(m_c, jnp.exp(s_c - mx), 0.0)
    e_n = jnp.where(m_n, jnp.exp(s_n - mx), 0.0)
    den = jnp.maximum(jnp.sum(e_c, axis=-1, keepdims=True) + jnp.sum(e_n, axis=-1, keepdims=True), TINY)
    o_w = _dot((e_c / den).astype(BF16), wv_ref[0].astype(BF16)) + \
        _dot((e_n / den).astype(BF16), wvn_ref[...].astype(BF16))
    owin_ref[0, 0] = o_w.astype(owin_ref.dtype)

    lax.fori_loop(0, n_fetch, wait, 0)

    kn = kn_ref[...].astype(BF16)
    vn = vn_ref[...].astype(BF16)
    span = nsel * SLC_BLOCK
    oslc_ref[...] = jnp.zeros(oslc_ref.shape, oslc_ref.dtype)
    for t in range(t_valid):
        qt = q_ref[0, 0, t * group:(t + 1) * group, :].astype(BF16)
        kt = kbuf[slot, t * span:(t + 1) * span, :].astype(BF16)
        vt = vbuf[slot, t * span:(t + 1) * span, :].astype(BF16)
        kp = kpos_ref[0, 0, t:t + 1, :]
        okv = okx_ref[0, 0, t:t + 1, :] > 0.5
        qpos = past + t
        m_g = (kp <= qpos) & okv & (kp < past)
        s_g = jnp.where(m_g, _dot_nt(qt, kt) * scale, NEG)
        selrow = jnp.zeros((1, tp), F32)
        for j in range(t_valid):
            hit = jnp.max(jnp.where((kp == past + j) & okv, 1.0, 0.0), axis=-1, keepdims=True)
            selrow = selrow + jnp.where(jn == j, hit, 0.0)
        m_w = (selrow > 0.5) & (jn <= t)
        s_w = jnp.where(m_w, _dot_nt(qt, kn) * scale, NEG)
        mx = jnp.maximum(jnp.max(s_g, axis=-1, keepdims=True), jnp.max(s_w, axis=-1, keepdims=True))
        e_g = jnp.where(m_g, jnp.exp(s_g - mx), 0.0)
        e_w = jnp.where(m_w, jnp.exp(s_w - mx), 0.0)
        den = jnp.sum(e_g, axis=-1, keepdims=True) + jnp.sum(e_w, axis=-1, keepdims=True)
        o_t = _dot((e_g / den).astype(BF16), vt) + _dot((e_w / den).astype(BF16), vn)
        oslc_ref[0, 0, t * group:(t + 1) * group, :] = o_t.astype(oslc_ref.dtype)


def sample_attn(q4, idx, okf, kv_new, cache_kv4, cache_win3, page_table, *, kvh, group, hd, t_valid, past):
    nb = q4.shape[0]
    tp = q4.shape[2] // group
    nsel = idx.shape[2]
    span = nsel * SLC_BLOCK
    wrows = cache_win3.shape[1]
    idx_t = idx.transpose(0, 1, 3, 2)
    kpos = (idx_t[..., None] * SLC_BLOCK + jnp.arange(SLC_BLOCK, dtype=jnp.int32)).reshape(nb, kvh, tp, span)
    okx = jnp.broadcast_to(okf.transpose(0, 1, 3, 2)[..., None], (nb, kvh, tp, nsel, SLC_BLOCK)).reshape(
        nb, kvh, tp, span)
    grid_spec = pltpu.PrefetchScalarGridSpec(
        num_scalar_prefetch=2,
        grid=(nb, kvh),
        in_specs=[
            pl.BlockSpec((1, 1, tp * group, hd), lambda b, g, *_: (b, g, 0, 0)),
            pl.BlockSpec((1, 1, tp, span), lambda b, g, *_: (b, g, 0, 0)),
            pl.BlockSpec((1, 1, tp, span), lambda b, g, *_: (b, g, 0, 0)),
            pl.BlockSpec((tp, hd), lambda b, g, *_: (b, 2 * kvh + g)),
            pl.BlockSpec((tp, hd), lambda b, g, *_: (b, 3 * kvh + g)),
            pl.BlockSpec((1, wrows, hd), lambda b, g, *_: (b, 0, g)),
            pl.BlockSpec((1, wrows, hd), lambda b, g, *_: (b, 0, kvh + g)),
            pl.BlockSpec((tp, hd), lambda b, g, *_: (b, 4 * kvh + g)),
            pl.BlockSpec((tp, hd), lambda b, g, *_: (b, 5 * kvh + g)),
            pl.BlockSpec(memory_space=pl.ANY),
        ],
        out_specs=[pl.BlockSpec((1, 1, tp * group, hd), lambda b, g, *_: (b, g, 0, 0)),
                   pl.BlockSpec((1, 1, tp * group, hd), lambda b, g, *_: (b, g, 0, 0))],
        scratch_shapes=[pltpu.VMEM((2, t_valid * span, hd), F32),
                        pltpu.VMEM((2, t_valid * span, hd), F32),
                        pltpu.SemaphoreType.DMA((2, 2))],
    )
    return pl.pallas_call(
        functools.partial(_sample_attn_kernel, kvh=kvh, group=group, hd=hd, t_valid=t_valid, past=past,
                          nsel=nsel, ks_sec=2, vs_sec=3),
        grid_spec=grid_spec,
        out_shape=[jax.ShapeDtypeStruct(q4.shape, F32), jax.ShapeDtypeStruct(q4.shape, F32)],
        compiler_params=_params(("arbitrary", "arbitrary")),
        name="sample_attn",
    )(idx.reshape(-1), page_table.reshape(-1), q4, kpos, okx, kv_new, kv_new, cache_win3, cache_win3,
      kv_new, kv_new, cache_kv4)


def _combine_kernel(oc_ref, os_ref, ow_ref, g_ref, o_ref, *, heads, hd):
    gates = g_ref[...]
    for h in range(heads):
        sl = slice(h * hd, (h + 1) * hd)
        o = (gates[:, h:h + 1] * oc_ref[:, sl].astype(F32)
             + gates[:, heads + h:heads + h + 1] * os_ref[:, sl].astype(F32)
             + gates[:, 2 * heads + h:2 * heads + h + 1] * ow_ref[:, sl].astype(F32))
        o_ref[:, sl] = o.astype(o_ref.dtype)


def combine(o_cmp, o_slc, o_win, gates, heads, hd):
    m, d = o_cmp.shape
    tm = _tile(m, 256)
    spec = pl.BlockSpec((tm, d), lambda i: (i, 0))
    return pl.pallas_call(
        functools.partial(_combine_kernel, heads=heads, hd=hd),
        grid=(m // tm,),
        in_specs=[spec, spec, spec, pl.BlockSpec((tm, 3 * heads), lambda i: (i, 0))],
        out_specs=spec,
        out_shape=jax.ShapeDtypeStruct((m, d), BF16),
        compiler_params=_params(("parallel",)),
        name="combine",
    )(o_cmp, o_slc, o_win, gates)


def _rope_tables_half(pos, half, theta):
    inv = jnp.power(theta, -jnp.arange(half, dtype=F32) / half)
    ang = pos.astype(F32)[:, None] * inv[None, :]
    return jnp.cos(ang), jnp.sin(ang)


def _nsa_rope_tables(pos, hd):
    half = hd // 8
    cos, sin = _rope_tables_half(pos, half, ROPE_THETA)
    n = pos.shape[0]
    ones = jnp.ones((n, hd - 2 * half), F32)
    zeros = jnp.zeros((n, hd - 2 * half), F32)
    zh = jnp.zeros((n, half), F32)
    c = jnp.concatenate([cos, cos, ones], axis=1)
    s_up = jnp.concatenate([-sin, zh, zeros], axis=1)
    s_dn = jnp.concatenate([zh, sin, zeros], axis=1)
    return c, s_up, s_dn


class _Stream:
    def __init__(self, x, nb, t_rows, t_valid, pos0, ret_s0, conv_state, ctx):
        self.x, self.nb, self.t_rows, self.t_valid, self.pos0 = x, nb, t_rows, t_valid, pos0
        self.ret_s0, self.conv_state, self.ctx = ret_s0, conv_state, ctx
        self.m = nb * t_rows
        self.pos = pos0 + jnp.arange(t_rows)


def _nsa_branches(st, q, gates, kv32, kv16, w_pairs, pe_term, w, round_rows, *, hd, kvh, group):
    rounded = None
    nb, t_rows, t_valid, pos0, m = st.nb, st.t_rows, st.t_valid, st.pos0, st.m
    qw = kvh * group * hd
    if st.ctx is None:
        p = cmp_proj_rows(kv32, nb, t_rows, w_pairs, kvh, hd)
        kvc = cmp_finish(p, pe_term, w["cmp_w2"], kvh)
        n_seg = t_rows // CMP_STRIDE
        n_c = n_seg - CMP_LEN // CMP_STRIDE + 1
        n_slc = -(-t_rows // SLC_BLOCK)
        o_cmp, imp = cmp_attn(q, kvc, nb=nb, kvh=kvh, group=group, hd=hd, n_c=n_c, n_slc=n_slc,
                              pos0=pos0, tq=512)
        bias = select_blocks(imp, n_slc=n_slc, pos0=pos0, emit_idx=False)
        bias = jnp.pad(bias, ((0, 0), (0, 0), (0, 0), (0, hd - bias.shape[-1])))
        o, rounded = prompt_attn(q, o_cmp, gates.T, bias, kv16, round_rows, nb=nb, kvh=kvh, group=group, hd=hd,
                                 tq=WINDOW)
        keep = min(WINDOW, t_valid)
        win = kv32.reshape(nb, t_rows, 6 * kvh * hd)[:, t_valid - keep:t_valid, 4 * kvh * hd:].reshape(
            nb, keep, 2, kvh, hd)
    else:
        cache_kv, cache_win, page_table = st.ctx
        past = pos0
        p = cmp_proj_pages(cache_kv, page_table, w_pairs, kvh, hd, pages_per_step=min(16, page_table.shape[1]))
        kvc = cmp_finish(p, pe_term, w["cmp_w2"], kvh)
        n_seg = (past + t_valid) // CMP_STRIDE
        n_c = n_seg - CMP_LEN // CMP_STRIDE + 1
        n_slc = -(-(past + t_valid) // SLC_BLOCK)
        o_cmp, imp = cmp_attn(q, kvc, nb=nb, kvh=kvh, group=group, hd=hd, n_c=n_c, n_slc=n_slc,
                              pos0=pos0, tq=t_rows)
        idx, okf = select_blocks(imp, n_slc=n_slc, pos0=pos0, emit_idx=True)
        q4 = q.astype(F32).reshape(nb, t_rows, kvh, group, hd).transpose(0, 2, 1, 3, 4).reshape(
            nb, kvh, t_rows * group, hd)
        wrows = cache_win.shape[1]
        o_slc4, o_win4 = sample_attn(q4, idx, okf, kv32, cache_kv,
                                     cache_win.reshape(nb, wrows, 2 * kvh * hd), page_table,
                                     kvh=kvh, group=group, hd=hd, t_valid=t_valid, past=past)

        def rows(o4):
            return o4.reshape(nb, kvh, t_rows, group, hd).transpose(0, 2, 1, 3, 4).reshape(m, qw).astype(BF16)

        win_new = kv32[:, 4 * kvh * hd:].reshape(nb, t_rows, 2, kvh, hd)[:, :t_valid]
        win = jnp.concatenate([cache_win, win_new], axis=1)[:, t_valid:]
        o = combine(o_cmp, rows(o_slc4), rows(o_win4), gates, kvh * group, hd)
    return o, win, rounded


def _trunk(sp, ss, w):
    d = sp.x.shape[1]
    streams = (sp, ss)
    tm = 1024
    d_ff = w["ffn_conv_w"].shape[-1]
    heads_r = RET_HEADS
    dk = d // heads_r
    dv = 2 * dk
    hd = d // NSA_HEADS
    kvh = NSA_KV_HEADS
    group = NSA_HEADS // kvh
    qw = NSA_HEADS * hd
    rope_half = hd // 8

    def norm(xs, g, dt=BF16):
        return [rmsnorm(x, g, dt) for x in xs]

    def ffn(xs, layer, w_down16):
        hs = norm(xs, w["norm_ffn"][layer])
        st = ss.conv_state[layer]
        f1 = jnp.zeros((ss.nb, ss.t_rows, d_ff), F32).at[:, 0].set(st[:, 1])
        f2 = jnp.zeros((ss.nb, ss.t_rows, d_ff), F32).at[:, 0].set(st[:, 0]).at[:, 1].set(st[:, 1])
        (a_p, tail_p), (a_s, tail_s) = ffn_up(
            hs[0], hs[1], w["ffn_w_up"], layer, d_ff, w["ffn_conv_w"][layer], w["ffn_conv_b"][layer], tm=tm,
            state=sp.conv_state[layer], fill=(f1.reshape(ss.m, d_ff), f2.reshape(ss.m, d_ff)), seg=ss.t_rows)
        cs = [tail_p[:, SUBLANES - 2:],
              tail_s.reshape(ss.nb, ss.t_rows, d_ff)[:, ss.t_valid - 2:ss.t_valid]]
        ys = [matmul(a, w_down16, d, tm=512, tn=512, tk=d_ff, out_dtypes=[F32], res=x)
              for a, x in zip((a_p, a_s), xs)]
        return ys, cs

    xs = [sp.x, ss.x]
    hs = norm(xs, w["norm_mix"][0])
    (proj_p,), (proj_s,) = matmul_ws(hs[0], hs[1], w["ret_w_in"], w["ret_w_in"].shape[-1], tm=512, tn=1024,
                                     out_dtypes=[BF16])
    o_ret, s_ret = [], []
    for st, proj in zip(streams, (proj_p, proj_s)):
        chunk = math.gcd(st.t_valid, RET_CHUNK)
        cpad = chunk if st.ctx is None else st.t_rows
        cos_r, sin_r = _rope_tables_half(st.pos, dk // 2, RET_THETA)
        jobs = []
        if st.ctx is None:
            steps = st.nb * heads_r * (st.t_rows // cpad)
            n_down = _round_block_count(d_ff, steps * 2 // 3)
            jobs = [(w["ffn_w_down"], 0, n_down),
                    (w["ret_w_out"], 0, _round_block_count(heads_r * dv, steps - n_down))]
        outs = retention(proj, st.ret_s0, cos_r, sin_r, nb=st.nb, heads=heads_r, dk=dk, dv=dv,
                         chunk=cpad, c_valid=chunk, round_rows=jobs)
        o_ret.append(outs[0])
        s_ret.append(outs[1])
        if st.ctx is None:
            w_down16, ret_w_out16 = outs[2:]
    xs = [matmul(o, ret_w_out16, d, tm=512, tn=512, tk=heads_r * dv, out_dtypes=[F32], res=x)
          for o, x in zip(o_ret, xs)]
    xs, conv0 = ffn(xs, 0, w_down16)

    hk = norm(xs, w["kv_norm"])
    rope_p = _nsa_rope_tables(sp.pos, hd)
    rope_s = tuple(jnp.tile(tb, (ss.nb, 1)) for tb in _nsa_rope_tables(ss.pos, hd))
    (kv32_p, kv16_p), (kv32_s, kv16_s) = matmul_ws(
        hk[0], hk[1], w["kv_w"], 6 * kvh * hd, tm=tm, tn=kvh * hd, out_dtypes=[F32, BF16],
        rope=(rope_p, rope_s), rope_every=2, rope_half=rope_half)
    kv32, kv16 = (kv32_p, kv32_s), (kv16_p, kv16_s)
    kv_rows = [kv[:, :4 * kvh * hd].reshape(st.nb, st.t_rows, 4, kvh, hd)[:, :st.t_valid]
               for st, kv in zip(streams, kv32)]

    hs = norm(xs, w["norm_mix"][1])
    (q_p,), (q_s,) = matmul_ws(hs[0], hs[1], w["nsa_w_in"], qw, tm=512, tn=1024, out_dtypes=[BF16],
                               rope=(rope_p, rope_s), rope_every=1, rope_half=rope_half)
    w_pairs = _cmp_w1_pairs(w["cmp_w1"], hd)
    pe_term = cmp_pe_term(w["cmp_pe"], w["cmp_w1"])
    os, wins = [], []
    for st, h, q, k32, k16 in zip(streams, hs, (q_p, q_s), kv32, kv16):
        gates = matmul(h, w["nsa_w_gate16"], 3 * NSA_HEADS, tm=tm, tn=3 * NSA_HEADS, tk=d, out_dtypes=[F32],
                       act="sigmoid")
        o, win, rounded = _nsa_branches(st, q, gates, k32, k16, w_pairs, pe_term, w,
                                        (w["ffn_w_down"], 1), hd=hd, kvh=kvh, group=group)
        os.append(o)
        wins.append(win)
        if rounded is not None:
            w_down16 = rounded
    (x_p,), (x_s,) = matmul_ws(os[0], os[1], w["nsa_w_out"], d, tm=tm, tn=512, out_dtypes=[F32], res=xs)
    xs, conv1 = ffn([x_p, x_s], 1, w_down16)
    ys = norm(xs, w["norm_final"], F32)
    convs = [jnp.stack([c0, c1]) for c0, c1 in zip(conv0, conv1)]
    return [(y, s[None], cv, kvr, win) for y, s, cv, kvr, win in zip(ys, s_ret, convs, kv_rows, wins)]


def kernel(x_prompt, x_sample, cache_kv, cache_win, state_ret, state_conv, page_table, norm_mix, norm_ffn,
           ret_w_in, ret_w_out, kv_norm, kv_w, cmp_pe, cmp_w1, cmp_w2, nsa_w_in, nsa_w_out, ffn_w_up, ffn_conv_w,
           ffn_conv_b, ffn_w_down, norm_final):
    b, t, d = x_prompt.shape
    db, dt, _ = x_sample.shape
    past = page_table.shape[1] * cache_kv.shape[1]
    qw = nsa_w_out.shape[1]
    w = dict(
        norm_mix=norm_mix, norm_ffn=norm_ffn, kv_norm=kv_norm, norm_final=norm_final,
        cmp_pe=cmp_pe, cmp_w1=cmp_w1, cmp_w2=cmp_w2, ffn_conv_w=ffn_conv_w, ffn_conv_b=ffn_conv_b,
        ret_w_in=ret_w_in[0], kv_w=kv_w, nsa_w_in=nsa_w_in[0], nsa_w_out=nsa_w_out[0], ffn_w_up=ffn_w_up,
        ret_w_out=ret_w_out, ffn_w_down=ffn_w_down, nsa_w_gate16=nsa_w_in[0][:, qw:].astype(BF16),
    )
    heads_r = RET_HEADS
    dk = d // heads_r
    ret0 = jnp.zeros((b, heads_r, dk, 2 * dk), F32)
    conv0 = jnp.zeros((state_conv.shape[0], b, 2, ffn_conv_w.shape[-1]), F32)
    tp = 2 * SUBLANES
    xs = jnp.zeros((db, tp, d), F32).at[:, :dt].set(x_sample).reshape(db * tp, d)
    sp = _Stream(x_prompt.reshape(b * t, d), b, t, t, 0, ret0, conv0, None)
    ss = _Stream(xs, db, tp, dt, past, state_ret[0], state_conv, (cache_kv, cache_win, page_table))
    (y_p, ret_p, conv_p, kv_p, win_p), (y_s, ret_s, conv_s, kv_s, win_s) = _trunk(sp, ss, w)
    y_s = y_s.reshape(db, tp, d)[:, :dt]
    return (y_p.reshape(b, t, d), y_s, kv_p, kv_s, win_p, win_s, ret_p, ret_s, conv_p, conv_s)
```

```python
import functools
import math

import jax
import jax.numpy as jnp
from jax import lax
from jax.experimental import pallas as pl
from jax.experimental.pallas import tpu as pltpu

F32 = jnp.float32
BF16 = jnp.bfloat16

RET_HEADS = 16
RET_CHUNK = 512
RET_THETA = 10000.0
NSA_HEADS = 32
NSA_KV_HEADS = 4
CMP_LEN = 32
CMP_STRIDE = 16
SLC_BLOCK = 64
N_SELECT = 16
WINDOW = 512
ROPE_THETA = 500000.0
PAGE_SIZE = 128
EPS = 1e-6
NEG = -1e30
FORCE = 1e9
TINY = 1e-20

LANES = 128
SUBLANES = 8
VMEM_LIMIT = 60 * 1024 * 1024


def _params(sem):
    return pltpu.CompilerParams(dimension_semantics=sem, vmem_limit_bytes=VMEM_LIMIT)


def _tile(dim, pref):
    if dim <= pref:
        return dim
    t = pref
    while dim % t:
        t //= 2
    return t


def _sigmoid(x):
    return 1.0 / (1.0 + jnp.exp(-x))


def _dot(a, b):
    return jnp.dot(a, b, preferred_element_type=F32)


def _dot_nt(a, b):
    return lax.dot_general(a, b, (((1,), (1,)), ((), ())), preferred_element_type=F32)


def _rmsnorm_kernel(x_ref, g_ref, o_ref):
    x = x_ref[...]
    y = x * lax.rsqrt(jnp.mean(x * x, axis=-1, keepdims=True) + EPS)
    o_ref[...] = (y * g_ref[...]).astype(o_ref.dtype)


def rmsnorm(x, g, out_dtype):
    m, d = x.shape
    tm = _tile(m, 512)
    return pl.pallas_call(
        _rmsnorm_kernel,
        grid=(m // tm,),
        in_specs=[pl.BlockSpec((tm, d), lambda i: (i, 0)),
                  pl.BlockSpec((1, d), lambda i: (0, 0))],
        out_specs=pl.BlockSpec((tm, d), lambda i: (i, 0)),
        out_shape=jax.ShapeDtypeStruct((m, d), out_dtype),
        compiler_params=_params(("parallel",)),
        name="rmsnorm",
    )(x, g.reshape(1, d).astype(F32))


def _rope_lanes(x, c, s_up, s_dn, half):
    return x * c + pltpu.roll(x, LANES - half, 1) * s_up + pltpu.roll(x, half, 1) * s_dn


def _mm_epilogue(acc, j, res_ref, rope_refs, out_refs, *, rope_every, rope_half, act):
    def store(val):
        for o in out_refs:
            o[...] = val.astype(o.dtype)

    if res_ref is not None:
        acc = acc + res_ref[...]
    if act == "sigmoid":
        acc = _sigmoid(acc)
    if not rope_every:
        store(acc)
        return
    tn = acc.shape[1]

    def roped():
        c, su, sd = (r[...] for r in rope_refs)
        pieces = [_rope_lanes(acc[:, a:a + LANES], c, su, sd, rope_half) for a in range(0, tn, LANES)]
        store(jnp.concatenate(pieces, axis=1))

    if rope_every == 1:
        roped()
    else:
        pl.when(j % rope_every == 0)(roped)
        pl.when(j % rope_every != 0)(lambda: store(acc))


def _mm_kernel(*refs, nk, has_res, rope_every, rope_half, act, n_out):
    x_ref, w_ref = refs[0], refs[1]
    pos = 2
    res_ref = None
    if has_res:
        res_ref = refs[pos]
        pos += 1
    rope_refs = None
    if rope_every:
        rope_refs = refs[pos:pos + 3]
        pos += 3
    out_refs = refs[pos:pos + n_out]
    acc_ref = refs[pos + n_out] if nk > 1 else None
    j = pl.program_id(1)
    k = pl.program_id(2)

    part = _dot(x_ref[...], w_ref[...].astype(BF16))
    finish = functools.partial(_mm_epilogue, j=j, res_ref=res_ref, rope_refs=rope_refs, out_refs=out_refs,
                               rope_every=rope_every, rope_half=rope_half, act=act)

    if nk == 1:
        finish(part)
    else:
        @pl.when(k == 0)
        def _():
            acc_ref[...] = part

        @pl.when(k > 0)
        def _():
            acc_ref[...] += part

        @pl.when(k == nk - 1)
        def _():
            finish(acc_ref[...])


def matmul(x, w, n, *, tm, tn, tk, out_dtypes, res=None, rope=None, rope_every=0, rope_half=0, act=None):
    m, kdim = x.shape
    tm, tn, tk = _tile(m, tm), _tile(n, tn), _tile(kdim, tk)
    nk = kdim // tk
    in_specs = [pl.BlockSpec((tm, tk), lambda i, j, k: (i, k)),
                pl.BlockSpec((tk, tn), lambda i, j, k: (k, j))]
    args = [x, w]
    if res is not None:
        in_specs.append(pl.BlockSpec((tm, tn), lambda i, j, k: (i, j)))
        args.append(res)
    if rope is not None:
        period = rope[0].shape[0]
        nper = period // tm
        for t in rope:
            in_specs.append(pl.BlockSpec((tm, LANES), lambda i, j, k: (i % nper, 0)))
            args.append(t)
    out_specs = [pl.BlockSpec((tm, tn), lambda i, j, k: (i, j)) for _ in out_dtypes]
    out_shape = [jax.ShapeDtypeStruct((m, n), dt) for dt in out_dtypes]
    scratch = [pltpu.VMEM((tm, tn), F32)] if nk > 1 else []
    outs = pl.pallas_call(
        functools.partial(_mm_kernel, nk=nk, has_res=res is not None,
                          rope_every=rope_every if rope is not None else 0, rope_half=rope_half,
                          act=act, n_out=len(out_dtypes)),
        grid=(m // tm, n // tn, nk),
        in_specs=in_specs, out_specs=out_specs, out_shape=out_shape,
        scratch_shapes=scratch,
        compiler_params=_params(("parallel", "parallel", "arbitrary")),
        name="matmul",
    )(*args)
    return outs if len(outs) > 1 else outs[0]


def _round_rows_job(step, first, src_ref, dst_ref, nblk):
    @pl.when((step >= first) & (step < first + nblk))
    def _():
        dst_ref[...] = src_ref[...].astype(BF16)


def _round_block_count(rows, max_blocks):
    rb = 64
    while rb < rows and (rows % rb or rows // rb > max_blocks):
        rb *= 2
    return rows // rb if rows % rb == 0 else 1


def _round_rows_specs(src, layer, nblk, first, step_of):
    rows, cols = src.shape[-2:]
    rb = rows // nblk

    def blk(*ids):
        return jnp.clip(step_of(*ids) - first, 0, nblk - 1)

    in_spec = pl.BlockSpec((None, rb, cols), lambda *ids: (layer, blk(*ids), 0))
    out_spec = pl.BlockSpec((rb, cols), lambda *ids: (blk(*ids), 0))
    return in_spec, out_spec, jax.ShapeDtypeStruct((rows, cols), BF16)


def _mm_ws_kernel(*refs, has_res, rope_every, rope_half, n_out):
    x_ref, xs_ref, w_ref = refs[:3]
    pos = 3
    res_ref = ress_ref = rope_refs = ropes_refs = None
    if has_res:
        res_ref, ress_ref = refs[pos:pos + 2]
        pos += 2
    if rope_every:
        rope_refs, ropes_refs = refs[pos:pos + 3], refs[pos + 3:pos + 6]
        pos += 6
    out_refs = refs[pos:pos + n_out]
    outs_refs = refs[pos + n_out:pos + 2 * n_out]
    w16 = refs[pos + 2 * n_out]
    j = pl.program_id(0)
    i = pl.program_id(1)

    @pl.when(i == 0)
    def _():
        w16[...] = w_ref[...].astype(BF16)

    kw = dict(rope_every=rope_every, rope_half=rope_half, act=None)
    _mm_epilogue(_dot(x_ref[...], w16[...]), j, res_ref, rope_refs, out_refs, **kw)

    @pl.when(i == pl.num_programs(1) - 1)
    def _():
        _mm_epilogue(_dot(xs_ref[...], w16[...]), j, ress_ref, ropes_refs, outs_refs, **kw)


def matmul_ws(x, xs, w, n, *, tm, tn, out_dtypes, res=None, rope=None, rope_every=0, rope_half=0):
    m, kdim = x.shape
    ms = xs.shape[0]
    tm, tn = _tile(m, tm), _tile(n, tn)
    in_specs = [pl.BlockSpec((tm, kdim), lambda j, i: (i, 0)),
                pl.BlockSpec((ms, kdim), lambda j, i: (0, 0)),
                pl.BlockSpec((kdim, tn), lambda j, i: (0, j))]
    args = [x, xs, w]
    if res is not None:
        in_specs += [pl.BlockSpec((tm, tn), lambda j, i: (i, j)), pl.BlockSpec((ms, tn), lambda j, i: (0, j))]
        args += list(res)
    if rope is not None:
        nper = rope[0][0].shape[0] // tm
        in_specs += [pl.BlockSpec((tm, LANES), lambda j, i: (i % nper, 0))] * 3
        in_specs += [pl.BlockSpec((ms, LANES), lambda j, i: (0, 0))] * 3
        args += list(rope[0]) + list(rope[1])
    out_specs = ([pl.BlockSpec((tm, tn), lambda j, i: (i, j)) for _ in out_dtypes]
                 + [pl.BlockSpec((ms, tn), lambda j, i: (0, j)) for _ in out_dtypes])
    out_shape = ([jax.ShapeDtypeStruct((m, n), dt) for dt in out_dtypes]
                 + [jax.ShapeDtypeStruct((ms, n), dt) for dt in out_dtypes])
    outs = pl.pallas_call(
        functools.partial(_mm_ws_kernel, has_res=res is not None,
                          rope_every=rope_every if rope is not None else 0, rope_half=rope_half,
                          n_out=len(out_dtypes)),
        grid=(n // tn, m // tm),
        in_specs=in_specs, out_specs=out_specs, out_shape=out_shape,
        scratch_shapes=[pltpu.VMEM((kdim, tn), BF16)],
        compiler_params=_params(("parallel", "arbitrary")),
        name="matmul_ws",
    )(*args)
    k = len(out_dtypes)
    return outs[:k], outs[k:]


def _ffn_up_kernel(x_ref, xs_ref, wv_ref, wg_ref, cw_ref, cb_ref, st_ref, f1_ref, f2_ref,
                   a_ref, tail_ref, as_ref, tails_ref, w16, carry_ref, *, tiles_per_seq, seg):
    i = pl.program_id(1)
    tn = wv_ref.shape[1]

    @pl.when(i == 0)
    def _():
        w16[:, :tn] = wv_ref[...].astype(BF16)
        w16[:, tn:] = wg_ref[...].astype(BF16)

    def val_gate(x):
        r = _dot(x, w16[...])
        return r[:, :tn], r[:, tn:]

    cb = cb_ref[...]
    cw0, cw1, cw2 = cw_ref[0:1, :], cw_ref[1:2, :], cw_ref[2:3, :]

    def gated(val, gate, g1, g2):
        conv = cb + cw0 * g2 + cw1 * g1 + cw2 * gate
        return (val * (conv * _sigmoid(conv))).astype(BF16)

    @pl.when(i % tiles_per_seq == 0)
    def _():
        carry_ref[...] = st_ref[0]

    tm = x_ref.shape[0]
    row = lax.broadcasted_iota(jnp.int32, (SUBLANES, 1), 0)
    val, gate = val_gate(x_ref[...])
    c0 = carry_ref[0:1, :]
    c1 = carry_ref[1:2, :]
    g1 = pltpu.roll(gate, 1, 0)
    g2 = pltpu.roll(gate, 2, 0)
    g1 = jnp.concatenate([jnp.where(row == 0, c1, g1[:SUBLANES]), g1[SUBLANES:]], axis=0)
    g2 = jnp.concatenate([jnp.where(row == 0, c0, jnp.where(row == 1, c1, g2[:SUBLANES])), g2[SUBLANES:]], axis=0)
    a_ref[...] = gated(val, gate, g1, g2)
    carry_ref[...] = gate[tm - 2:tm, :]
    nt = tail_ref.shape[1]
    tail_ref[0] = gate[tm - nt:tm, :]

    @pl.when(i == pl.num_programs(1) - 1)
    def _():
        val, gate = val_gate(xs_ref[...])
        t = jnp.bitwise_and(lax.broadcasted_iota(jnp.int32, (val.shape[0], 1), 0), seg - 1)
        g1 = jnp.where(t == 0, f1_ref[...], pltpu.roll(gate, 1, 0))
        g2 = jnp.where(t < 2, f2_ref[...], pltpu.roll(gate, 2, 0))
        as_ref[...] = gated(val, gate, g1, g2)
        tails_ref[...] = gate


def ffn_up(h, hs, w_up, layer, d_ff, conv_w, conv_b, *, tm, state, fill, seg):
    m, kdim = h.shape
    ms = hs.shape[0]
    assert seg & (seg - 1) == 0
    tn = _tile(d_ff, 256)
    nj = d_ff // tn
    tm = _tile(m, tm)
    nseq = state.shape[0]
    tps = (m // nseq) // tm
    a, tail, a_s, tail_s = pl.pallas_call(
        functools.partial(_ffn_up_kernel, tiles_per_seq=tps, seg=seg),
        grid=(nj, m // tm),
        in_specs=[pl.BlockSpec((tm, kdim), lambda j, i: (i, 0)),
                  pl.BlockSpec((ms, kdim), lambda j, i: (0, 0)),
                  pl.BlockSpec((None, kdim, tn), lambda j, i: (layer, 0, j)),
                  pl.BlockSpec((None, kdim, tn), lambda j, i: (layer, 0, nj + j)),
                  pl.BlockSpec((3, tn), lambda j, i: (0, j)),
                  pl.BlockSpec((1, tn), lambda j, i: (0, j)),
                  pl.BlockSpec((1, 2, tn), lambda j, i: (i // tps, 0, j)),
                  pl.BlockSpec((ms, tn), lambda j, i: (0, j)),
                  pl.BlockSpec((ms, tn), lambda j, i: (0, j))],
        out_specs=[pl.BlockSpec((tm, tn), lambda j, i: (i, j)),
                   pl.BlockSpec((1, SUBLANES, tn), lambda j, i: (i // tps, 0, j)),
                   pl.BlockSpec((ms, tn), lambda j, i: (0, j)),
                   pl.BlockSpec((ms, tn), lambda j, i: (0, j))],
        out_shape=[jax.ShapeDtypeStruct((m, d_ff), BF16),
                   jax.ShapeDtypeStruct((nseq, SUBLANES, d_ff), F32),
                   jax.ShapeDtypeStruct((ms, d_ff), BF16),
                   jax.ShapeDtypeStruct((ms, d_ff), F32)],
        scratch_shapes=[pltpu.VMEM((kdim, 2 * tn), BF16), pltpu.VMEM((2, tn), F32)],
        compiler_params=_params(("parallel", "arbitrary")),
        name="ffn_up",
    )(h, hs, w_up, w_up, conv_w, conv_b.reshape(1, d_ff), state, fill[0], fill[1])
    return (a, tail), (a_s, tail_s)


def _ret_kernel(lg_ref, q_ref, k_ref, v_ref, g_ref, cos_ref, sin_ref, s0_ref, *rest, c_valid, dk, round_blocks):
    nr = len(round_blocks)
    o_ref, s_out_ref = rest[nr:nr + 2]
    s_scr, intra_scr = rest[2 * nr + 2:]
    step = (pl.program_id(0) * pl.num_programs(1) + pl.program_id(1)) * pl.num_programs(2) + pl.program_id(2)
    for r, (first, nblk) in enumerate(round_blocks):
        _round_rows_job(step, first, rest[r], rest[nr + 2 + r], nblk)
    h = pl.program_id(1)
    c = pl.program_id(2)
    nc = pl.num_programs(2)
    half = dk // 2
    lg = lg_ref[h]
    cp = q_ref.shape[0]

    @pl.when(c == 0)
    def _():
        s_scr[...] = s0_ref[0, 0]
        di = lax.broadcasted_iota(jnp.int32, (cp, cp), 0)
        dj = lax.broadcasted_iota(jnp.int32, (cp, cp), 1)
        diff = di - dj
        intra_scr[...] = jnp.where((diff >= 0) & (dj < c_valid),
                                   jnp.exp(lg * jnp.maximum(diff, 0).astype(F32)), 0.0)

    cos = cos_ref[...]
    sin = sin_ref[...]

    def rope(x):
        x1 = x[:, :half]
        x2 = x[:, half:]
        return jnp.concatenate([x1 * cos - x2 * sin, x2 * cos + x1 * sin], axis=1)

    q = rope(q_ref[...].astype(F32))
    k = rope(k_ref[...].astype(F32)) * (dk ** -0.5)
    v = v_ref[...]
    ri = lax.broadcasted_iota(jnp.int32, (cp, 1), 0)
    rif = ri.astype(F32)
    read_decay = jnp.exp(lg * (rif + 1.0))
    write_decay = jnp.where(ri < c_valid, jnp.exp(lg * jnp.maximum(c_valid - 1.0 - rif, 0.0)), 0.0)

    s = s_scr[...]
    att = _dot_nt(q.astype(BF16), k.astype(BF16)) * intra_scr[...]
    o = _dot(att.astype(BF16), v) + _dot((q * read_decay).astype(BF16), s.astype(BF16))
    kw = (k * write_decay).astype(BF16)
    chunk_decay = jnp.exp(lg * jnp.full((1, 1), float(c_valid), F32))
    s_new = s * chunk_decay + lax.dot_general(kw, v, (((0,), (0,)), ((), ())), preferred_element_type=F32)
    s_scr[...] = s_new

    @pl.when(c == nc - 1)
    def _():
        s_out_ref[0, 0] = s_new

    on = o * lax.rsqrt(jnp.mean(o * o, axis=-1, keepdims=True) + EPS)
    gate = g_ref[...].astype(F32)
    o_ref[...] = (on * (gate * _sigmoid(gate))).astype(o_ref.dtype)


def retention(proj, s0, cos, sin, *, nb, heads, dk, dv, chunk, c_valid, round_rows=()):
    m = proj.shape[0]
    t = m // nb
    nc = t // chunk
    qb = heads
    vb = 2 * heads * dk // dv
    log_g = jnp.log1p(-jnp.exp2(-5.0 - jnp.arange(heads, dtype=F32)))
    in_specs = [
        pl.BlockSpec((chunk, dk), lambda b, h, c, lg: (b * nc + c, h)),
        pl.BlockSpec((chunk, dk), lambda b, h, c, lg: (b * nc + c, qb + h)),
        pl.BlockSpec((chunk, dv), lambda b, h, c, lg: (b * nc + c, vb + h)),
        pl.BlockSpec((chunk, dv), lambda b, h, c, lg: (b * nc + c, vb + heads + h)),
        pl.BlockSpec((chunk, dk // 2), lambda b, h, c, lg: (c, 0)),
        pl.BlockSpec((chunk, dk // 2), lambda b, h, c, lg: (c, 0)),
        pl.BlockSpec((1, 1, dk, dv), lambda b, h, c, lg: (b, h, 0, 0)),
    ]
    out_specs = [
        pl.BlockSpec((chunk, dv), lambda b, h, c, lg: (b * nc + c, h)),
        pl.BlockSpec((1, 1, dk, dv), lambda b, h, c, lg: (b, h, 0, 0)),
    ]
    out_shape = [jax.ShapeDtypeStruct((m, heads * dv), BF16),
                 jax.ShapeDtypeStruct((nb, heads, dk, dv), F32)]
    args = [log_g, proj, proj, proj, proj, cos, sin, s0]
    round_blocks = []
    for src, layer, nblk in round_rows:
        first = sum(n for _, n in round_blocks)
        assert first + nblk <= nb * heads * nc
        r_in, r_out, r_shape = _round_rows_specs(src, layer, nblk, first,
                                                 lambda b, h, c, lg: (b * heads + h) * nc + c)
        in_specs.append(r_in)
        out_specs.append(r_out)
        out_shape.append(r_shape)
        args.append(src)
        round_blocks.append((first, nblk))
    grid_spec = pltpu.PrefetchScalarGridSpec(
        num_scalar_prefetch=1,
        grid=(nb, heads, nc),
        in_specs=in_specs,
        out_specs=out_specs,
        scratch_shapes=[pltpu.VMEM((dk, dv), F32), pltpu.VMEM((chunk, chunk), F32)],
    )
    sem = ("arbitrary",) * 3 if round_blocks else ("parallel", "parallel", "arbitrary")
    return pl.pallas_call(
        functools.partial(_ret_kernel, c_valid=c_valid, dk=dk, round_blocks=tuple(round_blocks)),
        grid_spec=grid_spec,
        out_shape=out_shape,
        compiler_params=_params(sem),
        name="retention",
    )(*args)


def _cmp_project(planes, w_ref, p_ref, kvh, hd):
    n_seg = planes.shape[1] // CMP_STRIDE
    for br in range(2):
        acc = jnp.zeros((kvh * n_seg, 2 * hd), F32)
        for sp in range(CMP_STRIDE // 2):
            pieces = []
            for g in range(kvh):
                plane = planes.at[br * kvh + g]
                a = plane[pl.ds(2 * sp, n_seg, stride=CMP_STRIDE), :]
                b = plane[pl.ds(2 * sp + 1, n_seg, stride=CMP_STRIDE), :]
                pieces.append(jnp.concatenate([a, b], axis=1))
            lhs = jnp.concatenate(pieces, axis=0).astype(BF16)
            acc = acc + _dot(lhs, w_ref[br, sp])
        for g in range(kvh):
            p_ref[0, br * kvh + g] = acc[g * n_seg:(g + 1) * n_seg, :]


def _cmp_proj_rows_kernel(x_ref, w_ref, p_ref, r_ref, *, kvh, hd):
    for hh in range(2 * kvh):
        r_ref[hh] = x_ref[:, hh * hd:(hh + 1) * hd]
    _cmp_project(r_ref, w_ref, p_ref, kvh, hd)


def _cmp_proj_pages_kernel(pt_ref, cache_ref, w_ref, p_ref, r_ref, sem, *, kvh, hd, pages_per_step, n_pages):
    b = pl.program_id(0)
    c = pl.program_id(1)
    nchunk = pl.num_programs(1)
    step = b * nchunk + c
    nsteps = pl.num_programs(0) * nchunk
    page = cache_ref.shape[1]

    def copies(st, slot):
        bb = st // nchunk
        cc = st % nchunk
        out = []
        for u in range(pages_per_step):
            phys = pt_ref[bb * n_pages + cc * pages_per_step + u]
            for hh in range(2 * kvh):
                out.append(pltpu.make_async_copy(
                    cache_ref.at[phys, :, hh // kvh, hh % kvh, :],
                    r_ref.at[slot, hh, pl.ds(u * page, page), :],
                    sem.at[slot]))
        return out

    slot = step % 2

    @pl.when(step == 0)
    def _():
        for cp in copies(step, slot):
            cp.start()

    @pl.when(step + 1 < nsteps)
    def _():
        for cp in copies(step + 1, 1 - slot):
            cp.start()

    for cp in copies(step, slot):
        cp.wait()
    _cmp_project(r_ref.at[slot], w_ref, p_ref, kvh, hd)


def _cmp_w1_pairs(cmp_w1, hd):
    r = CMP_LEN // CMP_STRIDE
    e = cmp_w1.shape[-1]
    w = cmp_w1.reshape(2, r, CMP_STRIDE // 2, 2, hd, e)
    w = w.transpose(0, 2, 3, 4, 1, 5)
    return w.reshape(2, CMP_STRIDE // 2, 2 * hd, r * e).astype(BF16)


def cmp_proj_rows(rows2d, nb, t, w_pairs, kvh, hd):
    n_seg = t // CMP_STRIDE
    width = 2 * kvh * hd
    return pl.pallas_call(
        functools.partial(_cmp_proj_rows_kernel, kvh=kvh, hd=hd),
        grid=(nb,),
        in_specs=[pl.BlockSpec((t, width), lambda b: (b, 0)),
                  pl.BlockSpec(w_pairs.shape, lambda b: (0, 0, 0, 0))],
        out_specs=pl.BlockSpec((1, 2 * kvh, n_seg, 2 * hd), lambda b: (b, 0, 0, 0)),
        out_shape=jax.ShapeDtypeStruct((nb, 2 * kvh, n_seg, 2 * hd), F32),
        scratch_shapes=[pltpu.VMEM((2 * kvh, t, hd), F32)],
        compiler_params=_params(("parallel",)),
        name="cmp_proj_prompt",
    )(rows2d, w_pairs)


def cmp_proj_pages(cache, page_table, w_pairs, kvh, hd, pages_per_step):
    nb, n_pages = page_table.shape
    page = cache.shape[1]
    n_seg = pages_per_step * page // CMP_STRIDE
    nchunk = n_pages // pages_per_step
    grid_spec = pltpu.PrefetchScalarGridSpec(
        num_scalar_prefetch=1,
        grid=(nb, nchunk),
        in_specs=[pl.BlockSpec(memory_space=pl.ANY),
                  pl.BlockSpec(w_pairs.shape, lambda b, c, pt: (0, 0, 0, 0))],
        out_specs=pl.BlockSpec((1, 2 * kvh, n_seg, 2 * hd), lambda b, c, pt: (b, 0, c, 0)),
        scratch_shapes=[pltpu.VMEM((2, 2 * kvh, pages_per_step * page, hd), F32),
                        pltpu.SemaphoreType.DMA((2,))],
    )
    return pl.pallas_call(
        functools.partial(_cmp_proj_pages_kernel, kvh=kvh, hd=hd, pages_per_step=pages_per_step,
                          n_pages=n_pages),
        grid_spec=grid_spec,
        out_shape=jax.ShapeDtypeStruct((nb, 2 * kvh, nchunk * n_seg, 2 * hd), F32),
        compiler_params=_params(("arbitrary", "arbitrary")),
        name="cmp_proj_sample",
    )(page_table.reshape(-1), cache, w_pairs)


def _cmp_pe_kernel(pe_ref, w_ref, o_ref):
    o_ref[0] = _dot(pe_ref[0], w_ref[0])


def cmp_pe_term(cmp_pe, cmp_w1):
    _, n, hd = cmp_pe.shape
    e = cmp_w1.shape[-1]
    pe_flat = jnp.zeros((2, SUBLANES, n * hd), BF16).at[:, 0].set(cmp_pe.reshape(2, n * hd).astype(BF16))
    w_flat = cmp_w1.reshape(2, n * hd, e).astype(BF16)
    return pl.pallas_call(
        _cmp_pe_kernel,
        grid=(2,),
        in_specs=[pl.BlockSpec((1, SUBLANES, n * hd), lambda b: (b, 0, 0)),
                  pl.BlockSpec((1, n * hd, e), lambda b: (b, 0, 0))],
        out_specs=pl.BlockSpec((1, SUBLANES, e), lambda b: (b, 0, 0)),
        out_shape=jax.ShapeDtypeStruct((2, SUBLANES, e), F32),
        compiler_params=_params(("parallel",)),
        name="cmp_pe",
    )(pe_flat, w_flat)


def _cmp_finish_kernel(p_ref, pe_ref, w2_ref, o_ref):
    p = p_ref[0, 0]
    n_seg = p.shape[0]
    e = p.shape[1] // 2
    hid = p[:, :e] + pltpu.roll(p[:, e:], n_seg - 1, 0) + pe_ref[0, 0:1, :]
    c = math.sqrt(2.0 / math.pi)
    act = 0.5 * hid * (1.0 + jnp.tanh(c * (hid + 0.044715 * (hid * hid * hid))))
    o_ref[0, 0] = _dot(act.astype(BF16), w2_ref[0]).astype(o_ref.dtype)


def cmp_finish(p, pe_term, cmp_w2, kvh):
    nb, nu, n_seg, e2 = p.shape
    e = e2 // 2
    hd = cmp_w2.shape[-1]
    return pl.pallas_call(
        _cmp_finish_kernel,
        grid=(nb, nu),
        in_specs=[pl.BlockSpec((1, 1, n_seg, e2), lambda b, u: (b, u, 0, 0)),
                  pl.BlockSpec((1, SUBLANES, e), lambda b, u: (u // kvh, 0, 0)),
                  pl.BlockSpec((1, e, hd), lambda b, u: (u // kvh, 0, 0))],
        out_specs=pl.BlockSpec((1, 1, n_seg, hd), lambda b, u: (b, u, 0, 0)),
        out_shape=jax.ShapeDtypeStruct((nb, nu, n_seg, hd), BF16),
        compiler_params=_params(("parallel", "parallel")),
        name="cmp_finish",
    )(p, pe_term, cmp_w2.astype(BF16))


def _cmp_attn_kernel(q_ref, kc_ref, vc_ref, agg_ref, o_ref, imp_ref, *, group, hd, n_c, pos0):
    qi = pl.program_id(2)
    tq = q_ref.shape[0]
    kc = kc_ref[0, 0]
    vc = vc_ref[0, 0]
    agg = agg_ref[...]
    n_cp = kc.shape[0]
    qs = jnp.concatenate([q_ref[:, h * hd:(h + 1) * hd] for h in range(group)], axis=0)
    qpos1 = pos0 + qi * tq + lax.broadcasted_iota(jnp.int32, (tq, 1), 0)
    qpos = jnp.concatenate([qpos1] * group, axis=0)
    ci = lax.broadcasted_iota(jnp.int32, (1, n_cp), 1)
    cm = ((ci * CMP_STRIDE + (CMP_LEN - 1)) <= qpos) & (ci < n_c)
    s = jnp.where(cm, _dot_nt(qs, kc) * (hd ** -0.5), NEG)
    e = jnp.where(cm, jnp.exp(s - jnp.max(s, axis=-1, keepdims=True)), 0.0)
    p = e / jnp.maximum(jnp.sum(e, axis=-1, keepdims=True), TINY)
    pb = p.astype(BF16)
    o = _dot(pb, vc)
    psum = jnp.zeros((tq, n_cp), F32)
    for h in range(group):
        o_ref[:, h * hd:(h + 1) * hd] = o[h * tq:(h + 1) * tq, :].astype(o_ref.dtype)
        psum = psum + pb[h * tq:(h + 1) * tq, :].astype(F32)
    hi = psum.astype(BF16)
    lo = (psum - hi.astype(F32)).astype(BF16)
    imp_ref[0, 0] = _dot_nt(agg, hi) + _dot_nt(agg, lo)


def _slc_aggregation_t(n_cp, n_slc, rows):
    rs = SLC_BLOCK // CMP_STRIDE
    rc = CMP_LEN // CMP_STRIDE
    i = jnp.arange(n_cp)[None, :]
    j = jnp.arange(rows)[:, None]
    w = sum((i == j * rs + m - n).astype(F32) for m in range(rs) for n in range(rc))
    return jnp.where(j < n_slc, w, 0.0).astype(BF16)


def cmp_attn(q, kvc, *, nb, kvh, group, hd, n_c, n_slc, pos0, tq):
    m = q.shape[0]
    t = m // nb
    tq = _tile(t, tq)
    nq = t // tq
    n_cp = kvc.shape[2]
    rows = -(-n_slc // SUBLANES) * SUBLANES
    agg_t = _slc_aggregation_t(n_cp, n_slc, rows)
    gw = group * hd
    return pl.pallas_call(
        functools.partial(_cmp_attn_kernel, group=group, hd=hd, n_c=n_c, pos0=pos0),
        grid=(nb, kvh, nq),
        in_specs=[pl.BlockSpec((tq, gw), lambda b, g, i: (b * nq + i, g)),
                  pl.BlockSpec((1, 1, n_cp, hd), lambda b, g, i: (b, g, 0, 0)),
                  pl.BlockSpec((1, 1, n_cp, hd), lambda b, g, i: (b, kvh + g, 0, 0)),
                  pl.BlockSpec((rows, n_cp), lambda b, g, i: (0, 0))],
        out_specs=[pl.BlockSpec((tq, gw), lambda b, g, i: (b * nq + i, g)),
                   pl.BlockSpec((1, 1, rows, tq), lambda b, g, i: (b, g, 0, i))],
        out_shape=[jax.ShapeDtypeStruct((m, kvh * gw), BF16),
                   jax.ShapeDtypeStruct((nb, kvh, rows, t), F32)],
        compiler_params=_params(("parallel", "parallel", "parallel")),
        name="cmp_attn",
    )(q, kvc, kvc, agg_t)


def _select_kernel(imp_ref, *rest, n_slc, pos0, t_len, emit_idx):
    if emit_idx:
        idx_ref, ok_ref, sc_ref = rest
    else:
        bias_ref, sc_ref = rest
    rows, width = imp_ref.shape
    lane = pl.program_id(0) * width + lax.broadcasted_iota(jnp.int32, (1, width), 1)
    qpos = pos0 + jnp.bitwise_and(lane, t_len - 1)
    blk = lax.broadcasted_iota(jnp.int32, (rows, width), 0)
    cur = jnp.right_shift(qpos, SLC_BLOCK.bit_length() - 1)
    visible = blk * SLC_BLOCK <= qpos
    forced = (blk == 0) | (blk == cur) | (blk == cur - 1)
    score = jnp.where(visible, jnp.where(forced, FORCE, imp_ref[...]), NEG)
    sc_ref[...] = score

    def count(i, cnt):
        si = sc_ref[pl.ds(i, 1), :]
        beats = (si > score) | ((si == score) & (i < blk))
        return cnt + jnp.where(beats, 1, 0)

    rank = lax.fori_loop(0, n_slc, count, jnp.zeros((rows, width), jnp.int32))
    in_range = blk < n_slc
    if emit_idx:
        blk_f = blk.astype(F32)
        for r in range(idx_ref.shape[0]):
            hit = (rank == r) & in_range
            idx_r = jnp.sum(jnp.where(hit, blk_f, 0.0), axis=0, keepdims=True)
            idx_ref[r:r + 1, :] = idx_r.astype(jnp.int32)
            ok_ref[r:r + 1, :] = jnp.sum(jnp.where(hit & visible, 1.0, 0.0), axis=0, keepdims=True)
    else:
        chosen = (rank < N_SELECT) & visible & in_range
        bias_ref[...] = jnp.where(chosen, 0.0, NEG).astype(bias_ref.dtype)


def select_blocks(imp, *, n_slc, pos0, emit_idx):
    nb, kvh, rows, t = imp.shape
    assert t & (t - 1) == 0
    lanes = nb * kvh * t
    imp2 = imp.transpose(2, 0, 1, 3).reshape(rows, lanes)
    width = _tile(lanes, 2048)
    spec = pl.BlockSpec((rows, width), lambda i: (0, i))
    if emit_idx:
        nsel = min(N_SELECT, n_slc)
        o_spec = pl.BlockSpec((nsel, width), lambda i: (0, i))
        out_specs = [o_spec, o_spec]
        out_shape = [jax.ShapeDtypeStruct((nsel, lanes), jnp.int32), jax.ShapeDtypeStruct((nsel, lanes), F32)]
    else:
        out_specs = [spec]
        out_shape = [jax.ShapeDtypeStruct((rows, lanes), BF16)]
    outs = pl.pallas_call(
        functools.partial(_select_kernel, n_slc=n_slc, pos0=pos0, t_len=t, emit_idx=emit_idx),
        grid=(lanes // width,),
        in_specs=[spec], out_specs=out_specs, out_shape=out_shape,
        scratch_shapes=[pltpu.VMEM((rows, width), F32)],
        compiler_params=_params(("parallel",)),
        name="select_blocks",
    )(imp2)
    if emit_idx:
        return tuple(o.reshape(-1, nb, kvh, t).transpose(1, 2, 0, 3) for o in outs)
    return outs[0].reshape(rows, nb, kvh, t).transpose(1, 2, 3, 0)


V_ROWS_PAD = 2 * SUBLANES


def _prompt_attn_kernel(q_ref, bias_ref, ks_in, vs_in, kw_ref, vw_in, hot_ref, oc_ref, gc_ref, gs_ref, gw_ref,
                        rsrc_ref, o_ref, rdst_ref, ks_ref, vts_ref, vtw_ref, *, nq, hd, round_blocks):
    qi = pl.program_id(2)
    hs = pl.program_id(3)
    step = ((pl.program_id(0) * pl.num_programs(1) + pl.program_id(1)) * nq + qi) * pl.num_programs(3) + hs
    _round_rows_job(step, 0, rsrc_ref, rdst_ref, round_blocks)
    tq = q_ref.shape[0]
    n_heads = q_ref.shape[1] // hd
    kl = lax.broadcasted_iota(jnp.int32, (tq, tq), 0)
    ql = lax.broadcasted_iota(jnp.int32, (tq, tq), 1)

    @pl.when((qi == 0) & (hs == 0))
    def _():
        ks_ref[:, :hd] = ks_in[...]
        ks_ref[:, hd:] = hot_ref[...]
        pad = vts_ref.shape[0] - hd
        ones_row = jnp.where(lax.broadcasted_iota(jnp.int32, (pad, vts_ref.shape[1]), 0) == 0, 1.0, 0.0)
        for src, dst in ((vs_in, vts_ref), (vw_in, vtw_ref)):
            dst[:hd, :] = src[...].astype(F32).T.astype(BF16)
            dst[hd:, :] = ones_row.astype(BF16)

    def branch(k_ref, vt_ref, q, k0, nkeys, band):
        st = _dot_nt(k_ref[k0:k0 + nkeys, :], q)
        parts = []
        if nkeys > tq:
            top = st[:nkeys - tq, :]
            if band:
                top = jnp.where(kl > ql, top, NEG)
            parts.append(top)
        parts.append(jnp.where(kl <= ql, st[nkeys - tq:, :], NEG))
        st = jnp.concatenate(parts, axis=0) if len(parts) > 1 else parts[0]
        p = jnp.exp2(st - jnp.max(st, axis=0, keepdims=True)).astype(BF16)
        ot = _dot(vt_ref[:, k0:k0 + nkeys], p)
        return ot[:hd, :] / ot[hd:hd + 1, :]

    def run(v):
        for hp in range(n_heads):
            sl = slice(hp * hd, (hp + 1) * hd)
            row = pl.ds(hs * n_heads + hp, 1)
            q = (q_ref[:, sl].astype(F32) * (hd ** -0.5 * math.log2(math.e))).astype(BF16)
            o_s = branch(ks_ref, vts_ref, jnp.concatenate([q, bias_ref[0, 0]], axis=1), 0, (v + 1) * tq, False)
            o_w = branch(kw_ref, vtw_ref, q, max(v - 1, 0) * tq, min(v + 1, 2) * tq, v > 0)
            o_c = oc_ref[:, sl].astype(F32).T
            o_t = o_c * gc_ref[row, :] + o_s * gs_ref[row, :] + o_w * gw_ref[row, :]
            o_ref[:, sl] = o_t.T.astype(o_ref.dtype)

    for v in range(nq):
        pl.when(qi == v)(functools.partial(run, v))


def prompt_attn(q, o_cmp, gates_t, bias, kv16, round_rows, *, nb, kvh, group, hd, tq, heads_per_step=4):
    m = q.shape[0]
    t = m // nb
    tq = _tile(t, tq)
    nq = t // tq
    assert tq == WINDOW or nq == 1
    hw = heads_per_step * hd
    hsteps = group // heads_per_step
    head_spec = pl.BlockSpec((tq, hw), lambda b, g, i, h: (b * nq + i, g * hsteps + h))
    onehot = (jnp.arange(t)[:, None] // SLC_BLOCK == jnp.arange(hd)[None, :]).astype(BF16)

    def kv_spec(sec):
        return pl.BlockSpec((t, hd), lambda b, g, i, h: (b, sec * kvh + g))

    def gate_spec(branch):
        return pl.BlockSpec((group, tq), lambda b, g, i, h: (branch * kvh + g, b * nq + i))

    src, layer = round_rows
    round_blocks = _round_block_count(src.shape[-2], nb * kvh * nq * hsteps)
    r_in, r_out, r_shape = _round_rows_specs(src, layer, round_blocks, 0,
                                             lambda b, g, i, h: ((b * kvh + g) * nq + i) * hsteps + h)
    return pl.pallas_call(
        functools.partial(_prompt_attn_kernel, nq=nq, hd=hd, round_blocks=round_blocks),
        grid=(nb, kvh, nq, hsteps),
        in_specs=[head_spec,
                  pl.BlockSpec((1, 1, tq, hd), lambda b, g, i, h: (b, g, i, 0)),
                  kv_spec(2), kv_spec(3), kv_spec(4), kv_spec(5),
                  pl.BlockSpec((t, hd), lambda b, g, i, h: (0, 0)),
                  head_spec, gate_spec(0), gate_spec(1), gate_spec(2), r_in],
        out_specs=[head_spec, r_out],
        out_shape=[jax.ShapeDtypeStruct((m, kvh * group * hd), BF16), r_shape],
        scratch_shapes=[pltpu.VMEM((t, 2 * hd), BF16),
                        pltpu.VMEM((hd + V_ROWS_PAD, t), BF16),
                        pltpu.VMEM((hd + V_ROWS_PAD, t), BF16)],
        compiler_params=_params(("arbitrary", "arbitrary", "arbitrary", "arbitrary")),
        name="prompt_attn",
    )(q, bias, kv16, kv16, kv16, kv16, onehot, o_cmp, gates_t, gates_t, gates_t, src)


def _sample_attn_kernel(idx_ref, pt_ref, q_ref, kpos_ref, okx_ref, kn_ref, vn_ref, wk_ref, wv_ref,
                        wkn_ref, wvn_ref, cache_ref, oslc_ref, owin_ref, kbuf, vbuf, sem,
                        *, kvh, group, hd, t_valid, past, nsel, ks_sec, vs_sec):
    b = pl.program_id(0)
    g = pl.program_id(1)
    tp = q_ref.shape[2] // group
    n_blocks = past // SLC_BLOCK
    per_page = PAGE_SIZE // SLC_BLOCK
    n_fetch = t_valid * nsel

    step = b * kvh + g
    nsteps = pl.num_programs(0) * kvh
    slot = step % 2

    def copies(f, bb, gg, sl):
        t = f // nsel
        r = f % nsel
        blk = idx_ref[((bb * kvh + gg) * nsel + r) * tp + t]
        blk = jnp.minimum(blk, n_blocks - 1)
        page = pt_ref[bb * (past // PAGE_SIZE) + blk // per_page]
        row0 = (blk % per_page) * SLC_BLOCK
        dst = pl.ds(f * SLC_BLOCK, SLC_BLOCK)
        ck = pltpu.make_async_copy(cache_ref.at[page, pl.ds(row0, SLC_BLOCK), ks_sec, gg, :],
                                   kbuf.at[sl, dst, :], sem.at[sl, 0])
        cv = pltpu.make_async_copy(cache_ref.at[page, pl.ds(row0, SLC_BLOCK), vs_sec, gg, :],
                                   vbuf.at[sl, dst, :], sem.at[sl, 1])
        return ck, cv

    def start_all(bb, gg, sl):
        def body(f, c):
            ck, cv = copies(f, bb, gg, sl)
            ck.start(priority=0)
            cv.start(priority=1)
            return c

        lax.fori_loop(0, n_fetch, body, 0)

    def wait(f, c):
        ck, cv = copies(f, b, g, slot)
        ck.wait()
        cv.wait()
        return c

    @pl.when(step == 0)
    def _():
        start_all(b, g, slot)

    @pl.when(step + 1 < nsteps)
    def _():
        start_all((step + 1) // kvh, (step + 1) % kvh, 1 - slot)

    scale = hd ** -0.5
    qa = q_ref[0, 0].astype(BF16)
    trow = jnp.right_shift(lax.broadcasted_iota(jnp.int32, (tp * group, 1), 0), group.bit_length() - 1)
    wk = wk_ref[0].astype(BF16)
    wrows = wk.shape[0]
    s_c = _dot_nt(qa, wk) * scale
    s_n = _dot_nt(qa, wkn_ref[...].astype(BF16)) * scale
    d_c = (past + trow) - (past - wrows + lax.broadcasted_iota(jnp.int32, (1, wrows), 1))
    jn = lax.broadcasted_iota(jnp.int32, (1, tp), 1)
    d_n = trow - jn
    m_c = (d_c >= 0) & (d_c < WINDOW)
    m_n = (d_n >= 0) & (d_n < WINDOW) & (jn < t_valid)
    s_c = jnp.where(m_c, s_c, NEG)
    s_n = jnp.where(m_n, s_n, NEG)
    mx = jnp.maximum(jnp.max(s_c, axis=-1, keepdims=True), jnp.max(s_n, axis=-1, keepdims=True))
    e_c = jnp.where(m_c, jnp.exp(s_c - mx), 0.0)
    e_n = jnp.where(m_n, jnp.exp(s_n - mx), 0.0)
    den = jnp.maximum(jnp.sum(e_c, axis=-1, keepdims=True) + jnp.sum(e_n, axis=-1, keepdims=True), TINY)
    o_w = _dot((e_c / den).astype(BF16), wv_ref[0].astype(BF16)) + \
        _dot((e_n / den).astype(BF16), wvn_ref[...].astype(BF16))
    owin_ref[0, 0] = o_w.astype(owin_ref.dtype)

    lax.fori_loop(0, n_fetch, wait, 0)

    kn = kn_ref[...].astype(BF16)
    vn = vn_ref[...].astype(BF16)
    span = nsel * SLC_BLOCK
    oslc_ref[...] = jnp.zeros(oslc_ref.shape, oslc_ref.dtype)
    for t in range(t_valid):
        qt = q_ref[0, 0, t * group:(t + 1) * group, :].astype(BF16)
        kt = kbuf[slot, t * span:(t + 1) * span, :].astype(BF16)
        vt = vbuf[slot, t * span:(t + 1) * span, :].astype(BF16)
        kp = kpos_ref[0, 0, t:t + 1, :]
        okv = okx_ref[0, 0, t:t + 1, :] > 0.5
        qpos = past + t
        m_g = (kp <= qpos) & okv & (kp < past)
        s_g = jnp.where(m_g, _dot_nt(qt, kt) * scale, NEG)
        selrow = jnp.zeros((1, tp), F32)
        for j in range(t_valid):
            hit = jnp.max(jnp.where((kp == past + j) & okv, 1.0, 0.0), axis=-1, keepdims=True)
            selrow = selrow + jnp.where(jn == j, hit, 0.0)
        m_w = (selrow > 0.5) & (jn <= t)
        s_w = jnp.where(m_w, _dot_nt(qt, kn) * scale, NEG)
        mx = jnp.maximum(jnp.max(s_g, axis=-1, keepdims=True), jnp.max(s_w, axis=-1, keepdims=True))
        e_g = jnp.where(m_g, jnp.exp(s_g - mx), 0.0)
        e_w = jnp.where(m_w, jnp.exp(s_w - mx), 0.0)
        den = jnp.sum(e_g, axis=-1, keepdims=True) + jnp.sum(e_w, axis=-1, keepdims=True)
        o_t = _dot((e_g / den).astype(BF16), vt) + _dot((e_w / den).astype(BF16), vn)
        oslc_ref[0, 0, t * group:(t + 1) * group, :] = o_t.astype(oslc_ref.dtype)


def sample_attn(q4, idx, okf, kv_new, cache_kv4, cache_win3, page_table, *, kvh, group, hd, t_valid, past):
    nb = q4.shape[0]
    tp = q4.shape[2] // group
    nsel = idx.shape[2]
    span = nsel * SLC_BLOCK
    wrows = cache_win3.shape[1]
    idx_t = idx.transpose(0, 1, 3, 2)
    kpos = (idx_t[..., None] * SLC_BLOCK + jnp.arange(SLC_BLOCK, dtype=jnp.int32)).reshape(nb, kvh, tp, span)
    okx = jnp.broadcast_to(okf.transpose(0, 1, 3, 2)[..., None], (nb, kvh, tp, nsel, SLC_BLOCK)).reshape(
        nb, kvh, tp, span)
    grid_spec = pltpu.PrefetchScalarGridSpec(
        num_scalar_prefetch=2,
        grid=(nb, kvh),
        in_specs=[
            pl.BlockSpec((1, 1, tp * group, hd), lambda b, g, *_: (b, g, 0, 0)),
            pl.BlockSpec((1, 1, tp, span), lambda b, g, *_: (b, g, 0, 0)),
            pl.BlockSpec((1, 1, tp, span), lambda b, g, *_: (b, g, 0, 0)),
            pl.BlockSpec((tp, hd), lambda b, g, *_: (b, 2 * kvh + g)),
            pl.BlockSpec((tp, hd), lambda b, g, *_: (b, 3 * kvh + g)),
            pl.BlockSpec((1, wrows, hd), lambda b, g, *_: (b, 0, g)),
            pl.BlockSpec((1, wrows, hd), lambda b, g, *_: (b, 0, kvh + g)),
            pl.BlockSpec((tp, hd), lambda b, g, *_: (b, 4 * kvh + g)),
            pl.BlockSpec((tp, hd), lambda b, g, *_: (b, 5 * kvh + g)),
            pl.BlockSpec(memory_space=pl.ANY),
        ],
        out_specs=[pl.BlockSpec((1, 1, tp * group, hd), lambda b, g, *_: (b, g, 0, 0)),
                   pl.BlockSpec((1, 1, tp * group, hd), lambda b, g, *_: (b, g, 0, 0))],
        scratch_shapes=[pltpu.VMEM((2, t_valid * span, hd), F32),
                        pltpu.VMEM((2, t_valid * span, hd), F32),
                        pltpu.SemaphoreType.DMA((2, 2))],
    )
    return pl.pallas_call(
        functools.partial(_sample_attn_kernel, kvh=kvh, group=group, hd=hd, t_valid=t_valid, past=past,
                          nsel=nsel, ks_sec=2, vs_sec=3),
        grid_spec=grid_spec,
        out_shape=[jax.ShapeDtypeStruct(q4.shape, F32), jax.ShapeDtypeStruct(q4.shape, F32)],
        compiler_params=_params(("arbitrary", "arbitrary")),
        name="sample_attn",
    )(idx.reshape(-1), page_table.reshape(-1), q4, kpos, okx, kv_new, kv_new, cache_win3, cache_win3,
      kv_new, kv_new, cache_kv4)


def _combine_kernel(oc_ref, os_ref, ow_ref, g_ref, o_ref, *, heads, hd):
    gates = g_ref[...]
    for h in range(heads):
        sl = slice(h * hd, (h + 1) * hd)
        o = (gates[:, h:h + 1] * oc_ref[:, sl].astype(F32)
             + gates[:, heads + h:heads + h + 1] * os_ref[:, sl].astype(F32)
             + gates[:, 2 * heads + h:2 * heads + h + 1] * ow_ref[:, sl].astype(F32))
        o_ref[:, sl] = o.astype(o_ref.dtype)


def combine(o_cmp, o_slc, o_win, gates, heads, hd):
    m, d = o_cmp.shape
    tm = _tile(m, 256)
    spec = pl.BlockSpec((tm, d), lambda i: (i, 0))
    return pl.pallas_call(
        functools.partial(_combine_kernel, heads=heads, hd=hd),
        grid=(m // tm,),
        in_specs=[spec, spec, spec, pl.BlockSpec((tm, 3 * heads), lambda i: (i, 0))],
        out_specs=spec,
        out_shape=jax.ShapeDtypeStruct((m, d), BF16),
        compiler_params=_params(("parallel",)),
        name="combine",
    )(o_cmp, o_slc, o_win, gates)


def _rope_tables_half(pos, half, theta):
    inv = jnp.power(theta, -jnp.arange(half, dtype=F32) / half)
    ang = pos.astype(F32)[:, None] * inv[None, :]
    return jnp.cos(ang), jnp.sin(ang)


def _nsa_rope_tables(pos, hd):
    half = hd // 8
    cos, sin = _rope_tables_half(pos, half, ROPE_THETA)
    n = pos.shape[0]
    ones = jnp.ones((n, hd - 2 * half), F32)
    zeros = jnp.zeros((n, hd - 2 * half), F32)
    zh = jnp.zeros((n, half), F32)
    c = jnp.concatenate([cos, cos, ones], axis=1)
    s_up = jnp.concatenate([-sin, zh, zeros], axis=1)
    s_dn = jnp.concatenate([zh, sin, zeros], axis=1)
    return c, s_up, s_dn


class _Stream:
    def __init__(self, x, nb, t_rows, t_valid, pos0, ret_s0, conv_state, ctx):
        self.x, self.nb, self.t_rows, self.t_valid, self.pos0 = x, nb, t_rows, t_valid, pos0
        self.ret_s0, self.conv_state, self.ctx = ret_s0, conv_state, ctx
        self.m = nb * t_rows
        self.pos = pos0 + jnp.arange(t_rows)


def _nsa_branches(st, q, gates, kv32, kv16, w_pairs, pe_term, w, round_rows, *, hd, kvh, group):
    rounded = None
    nb, t_rows, t_valid, pos0, m = st.nb, st.t_rows, st.t_valid, st.pos0, st.m
    qw = kvh * group * hd
    if st.ctx is None:
        p = cmp_proj_rows(kv32, nb, t_rows, w_pairs, kvh, hd)
        kvc = cmp_finish(p, pe_term, w["cmp_w2"], kvh)
        n_seg = t_rows // CMP_STRIDE
        n_c = n_seg - CMP_LEN // CMP_STRIDE + 1
        n_slc = -(-t_rows // SLC_BLOCK)
        o_cmp, imp = cmp_attn(q, kvc, nb=nb, kvh=kvh, group=group, hd=hd, n_c=n_c, n_slc=n_slc,
                              pos0=pos0, tq=512)
        bias = select_blocks(imp, n_slc=n_slc, pos0=pos0, emit_idx=False)
        bias = jnp.pad(bias, ((0, 0), (0, 0), (0, 0), (0, hd - bias.shape[-1])))
        o, rounded = prompt_attn(q, o_cmp, gates.T, bias, kv16, round_rows, nb=nb, kvh=kvh, group=group, hd=hd,
                                 tq=WINDOW)
        keep = min(WINDOW, t_valid)
        win = kv32.reshape(nb, t_rows, 6 * kvh * hd)[:, t_valid - keep:t_valid, 4 * kvh * hd:].reshape(
            nb, keep, 2, kvh, hd)
    else:
        cache_kv, cache_win, page_table = st.ctx
        past = pos0
        p = cmp_proj_pages(cache_kv, page_table, w_pairs, kvh, hd, pages_per_step=min(16, page_table.shape[1]))
        kvc = cmp_finish(p, pe_term, w["cmp_w2"], kvh)
        n_seg = (past + t_valid) // CMP_STRIDE
        n_c = n_seg - CMP_LEN // CMP_STRIDE + 1
        n_slc = -(-(past + t_valid) // SLC_BLOCK)
        o_cmp, imp = cmp_attn(q, kvc, nb=nb, kvh=kvh, group=group, hd=hd, n_c=n_c, n_slc=n_slc,
                              pos0=pos0, tq=t_rows)
        idx, okf = select_blocks(imp, n_slc=n_slc, pos0=pos0, emit_idx=True)
        q4 = q.astype(F32).reshape(nb, t_rows, kvh, group, hd).transpose(0, 2, 1, 3, 4).reshape(
            nb, kvh, t_rows * group, hd)
        wrows = cache_win.shape[1]
        o_slc4, o_win4 = sample_attn(q4, idx, okf, kv32, cache_kv,
                                     cache_win.reshape(nb, wrows, 2 * kvh * hd), page_table,
                                     kvh=kvh, group=group, hd=hd, t_valid=t_valid, past=past)

        def rows(o4):
            return o4.reshape(nb, kvh, t_rows, group, hd).transpose(0, 2, 1, 3, 4).reshape(m, qw).astype(BF16)

        win_new = kv32[:, 4 * kvh * hd:].reshape(nb, t_rows, 2, kvh, hd)[:, :t_valid]
        win = jnp.concatenate([cache_win, win_new], axis=1)[:, t_valid:]
        o = combine(o_cmp, rows(o_slc4), rows(o_win4), gates, kvh * group, hd)
    return o, win, rounded


def _trunk(sp, ss, w):
    d = sp.x.shape[1]
    streams = (sp, ss)
    tm = 1024
    d_ff = w["ffn_conv_w"].shape[-1]
    heads_r = RET_HEADS
    dk = d // heads_r
    dv = 2 * dk
    hd = d // NSA_HEADS
    kvh = NSA_KV_HEADS
    group = NSA_HEADS // kvh
    qw = NSA_HEADS * hd
    rope_half = hd // 8

    def norm(xs, g, dt=BF16):
        return [rmsnorm(x, g, dt) for x in xs]

    def ffn(xs, layer, w_down16):
        hs = norm(xs, w["norm_ffn"][layer])
        st = ss.conv_state[layer]
        f1 = jnp.zeros((ss.nb, ss.t_rows, d_ff), F32).at[:, 0].set(st[:, 1])
        f2 = jnp.zeros((ss.nb, ss.t_rows, d_ff), F32).at[:, 0].set(st[:, 0]).at[:, 1].set(st[:, 1])
        (a_p, tail_p), (a_s, tail_s) = ffn_up(
            hs[0], hs[1], w["ffn_w_up"], layer, d_ff, w["ffn_conv_w"][layer], w["ffn_conv_b"][layer], tm=tm,
            state=sp.conv_state[layer], fill=(f1.reshape(ss.m, d_ff), f2.reshape(ss.m, d_ff)), seg=ss.t_rows)
        cs = [tail_p[:, SUBLANES - 2:],
              tail_s.reshape(ss.nb, ss.t_rows, d_ff)[:, ss.t_valid - 2:ss.t_valid]]
        ys = [matmul(a, w_down16, d, tm=512, tn=512, tk=d_ff, out_dtypes=[F32], res=x)
              for a, x in zip((a_p, a_s), xs)]
        return ys, cs

    xs = [sp.x, ss.x]
    hs = norm(xs, w["norm_mix"][0])
    (proj_p,), (proj_s,) = matmul_ws(hs[0], hs[1], w["ret_w_in"], w["ret_w_in"].shape[-1], tm=512, tn=1024,
                                     out_dtypes=[BF16])
    o_ret, s_ret = [], []
    for st, proj in zip(streams, (proj_p, proj_s)):
        chunk = math.gcd(st.t_valid, RET_CHUNK)
        cpad = chunk if st.ctx is None else st.t_rows
        cos_r, sin_r = _rope_tables_half(st.pos, dk // 2, RET_THETA)
        jobs = []
        if st.ctx is None:
            steps = st.nb * heads_r * (st.t_rows // cpad)
            n_down = _round_block_count(d_ff, steps * 2 // 3)
            jobs = [(w["ffn_w_down"], 0, n_down),
                    (w["ret_w_out"], 0, _round_block_count(heads_r * dv, steps - n_down))]
        outs = retention(proj, st.ret_s0, cos_r, sin_r, nb=st.nb, heads=heads_r, dk=dk, dv=dv,
                         chunk=cpad, c_valid=chunk, round_rows=jobs)
        o_ret.append(outs[0])
        s_ret.append(outs[1])
        if st.ctx is None:
            w_down16, ret_w_out16 = outs[2:]
    xs = [matmul(o, ret_w_out16, d, tm=512, tn=512, tk=heads_r * dv, out_dtypes=[F32], res=x)
          for o, x in zip(o_ret, xs)]
    xs, conv0 = ffn(xs, 0, w_down16)

    hk = norm(xs, w["kv_norm"])
    rope_p = _nsa_rope_tables(sp.pos, hd)
    rope_s = tuple(jnp.tile(tb, (ss.nb, 1)) for tb in _nsa_rope_tables(ss.pos, hd))
    (kv32_p, kv16_p), (kv32_s, kv16_s) = matmul_ws(
        hk[0], hk[1], w["kv_w"], 6 * kvh * hd, tm=tm, tn=kvh * hd, out_dtypes=[F32, BF16],
        rope=(rope_p, rope_s), rope_every=2, rope_half=rope_half)
    kv32, kv16 = (kv32_p, kv32_s), (kv16_p, kv16_s)
    kv_rows = [kv[:, :4 * kvh * hd].reshape(st.nb, st.t_rows, 4, kvh, hd)[:, :st.t_valid]
               for st, kv in zip(streams, kv32)]

    hs = norm(xs, w["norm_mix"][1])
    (q_p,), (q_s,) = matmul_ws(hs[0], hs[1], w["nsa_w_in"], qw, tm=512, tn=1024, out_dtypes=[BF16],
                               rope=(rope_p, rope_s), rope_every=1, rope_half=rope_half)
    w_pairs = _cmp_w1_pairs(w["cmp_w1"], hd)
    pe_term = cmp_pe_term(w["cmp_pe"], w["cmp_w1"])
    os, wins = [], []
    for st, h, q, k32, k16 in zip(streams, hs, (q_p, q_s), kv32, kv16):
        gates = matmul(h, w["nsa_w_gate16"], 3 * NSA_HEADS, tm=tm, tn=3 * NSA_HEADS, tk=d, out_dtypes=[F32],
                       act="sigmoid")
        o, win, rounded = _nsa_branches(st, q, gates, k32, k16, w_pairs, pe_term, w,
                                        (w["ffn_w_down"], 1), hd=hd, kvh=kvh, group=group)
        os.append(o)
        wins.append(win)
        if rounded is not None:
            w_down16 = rounded
    (x_p,), (x_s,) = matmul_ws(os[0], os[1], w["nsa_w_out"], d, tm=tm, tn=512, out_dtypes=[F32], res=xs)
    xs, conv1 = ffn([x_p, x_s], 1, w_down16)
    ys = norm(xs, w["norm_final"], F32)
    convs = [jnp.stack([c0, c1]) for c0, c1 in zip(conv0, conv1)]
    return [(y, s[None], cv, kvr, win) for y, s, cv, kvr, win in zip(ys, s_ret, convs, kv_rows, wins)]


def kernel(x_prompt, x_sample, cache_kv, cache_win, state_ret, state_conv, page_table, norm_mix, norm_ffn,
           ret_w_in, ret_w_out, kv_norm, kv_w, cmp_pe, cmp_w1, cmp_w2, nsa_w_in, nsa_w_out, ffn_w_up, ffn_conv_w,
           ffn_conv_b, ffn_w_down, norm_final):
    b, t, d = x_prompt.shape
    db, dt, _ = x_sample.shape
    past = page_table.shape[1] * cache_kv.shape[1]
    qw = nsa_w_out.shape[1]
    w = dict(
        norm_mix=norm_mix, norm_ffn=norm_ffn, kv_norm=kv_norm, norm_final=norm_final,
        cmp_pe=cmp_pe, cmp_w1=cmp_w1, cmp_w2=cmp_w2, ffn_conv_w=ffn_conv_w, ffn_conv_b=ffn_conv_b,
        ret_w_in=ret_w_in[0], kv_w=kv_w, nsa_w_in=nsa_w_in[0], nsa_w_out=nsa_w_out[0], ffn_w_up=ffn_w_up,
        ret_w_out=ret_w_out, ffn_w_down=ffn_w_down, nsa_w_gate16=nsa_w_in[0][:, qw:].astype(BF16),
    )
    heads_r = RET_HEADS
    dk = d // heads_r
    ret0 = jnp.zeros((b, heads_r, dk, 2 * dk), F32)
    conv0 = jnp.zeros((state_conv.shape[0], b, 2, ffn_conv_w.shape[-1]), F32)
    tp = 2 * SUBLANES
    xs = jnp.zeros((db, tp, d), F32).at[:, :dt].set(x_sample).reshape(db * tp, d)
    sp = _Stream(x_prompt.reshape(b * t, d), b, t, t, 0, ret0, conv0, None)
    ss = _Stream(xs, db, tp, dt, past, state_ret[0], state_conv, (cache_kv, cache_win, page_table))
    (y_p, ret_p, conv_p, kv_p, win_p), (y_s, ret_s, conv_s, kv_s, win_s) = _trunk(sp, ss, w)
    y_s = y_s.reshape(db, tp, d)[:, :dt]
    return (y_p.reshape(b, t, d), y_s, kv_p, kv_s, win_p, win_s, ret_p, ret_s, conv_p, conv_s)
```
